```python
import jax
import jax.numpy as jnp
from jax import lax
import numpy as np

D_MODEL = 2048
BATCH = 2
SEQ = 4096
DEPTH = 4
DEC_BATCH = 8
DEC_SEQ = 1
PAST_LEN = 16384
PAGE_SIZE = 128

W_MIX = D_MODEL // 4
POOL_WINDOWS = (2, 4, 8, 16)
POOL_GROUP = W_MIX // len(POOL_WINDOWS)
POOL_BUF = max(POOL_WINDOWS) - 1
CONV_WIDTH = 31
CONV_BUF = CONV_WIDTH - 1
SB_HEADS = 8
SB_HEAD_DIM = W_MIX // SB_HEADS
SB_SCALE = SB_HEAD_DIM ** -0.5
SB_BIAS_INIT = -6.0
Q_BLOCK = 128
RWKV_HEAD_DIM = 64
RWKV_HEADS = W_MIX // RWKV_HEAD_DIM
DECAY_LORA = 64
A_LORA = 64
GATE_LORA = 128
N_D = 3 * W_MIX + DECAY_LORA + A_LORA + GATE_LORA
N_BRANCH = 4
OFF_A = 0
OFF_B = OFF_A + W_MIX
OFF_C = OFF_B + 2 * W_MIX
OFF_D = OFF_C + 3 * W_MIX
OFF_G = OFF_D + N_D
N_IN = OFF_G + N_BRANCH * D_MODEL
D_FF = ((8 * D_MODEL // 3 + 127) // 128) * 128
FFN_RESIDUAL = 0.5
RMS_EPS = 1e-6
LN_EPS = 1e-5
GN_EPS = 64e-5

kernel_name = 'hybrid_gated_pool_conv_stickbreak_rwkv7_step'


def rms_norm(x, g):
    xf = x.astype(jnp.float32)
    y = xf * lax.rsqrt(jnp.mean(xf * xf, axis=-1, keepdims=True) + RMS_EPS)
    return y.astype(x.dtype) * g


def swiglu(h, w_in, w_out):
    gate, up = jnp.split(h @ w_in, 2, axis=-1)
    return (jax.nn.silu(gate) * up) @ w_out


def pool_mixer(u, buf, pos0, w_pool, pool_scale):
    b, t, _ = u.shape
    ext = jnp.concatenate([buf.astype(u.dtype), u], axis=1)
    csum = jnp.cumsum(ext.astype(jnp.float32), axis=1)
    csum = jnp.pad(csum, ((0, 0), (1, 0), (0, 0)))
    pos = pos0 + jnp.arange(t)
    end = POOL_BUF + 1
    uf = u.astype(jnp.float32)
    groups = []
    for gi, win in enumerate(POOL_WINDOWS):
        ch = slice(gi * POOL_GROUP, (gi + 1) * POOL_GROUP)
        wsum = csum[:, end:end + t, ch] - csum[:, end - win:end - win + t, ch]
        count = jnp.minimum(win, pos + 1).astype(jnp.float32)[None, :, None]
        groups.append(wsum / count - uf[..., ch])
    d = jnp.stack(groups, axis=2).astype(u.dtype)
    y = jnp.einsum('btgc,gcd->btgd', d, w_pool).reshape(b, t, W_MIX)
    return y * pool_scale, ext[:, -POOL_BUF:]


def conv_mixer(p, buf, conv_w, conv_b, ln_g, ln_b):
    val, gate = jnp.split(p, 2, axis=-1)
    u = val * jax.nn.sigmoid(gate)
    ext = jnp.concatenate([buf.astype(u.dtype), u], axis=1)
    y = lax.conv_general_dilated(ext, conv_w[:, None, :].astype(ext.dtype), window_strides=(1,),
                                 padding='VALID', dimension_numbers=('NWC', 'WIO', 'NWC'),
                                 feature_group_count=W_MIX) + conv_b
    yf = y.astype(jnp.float32)
    mu = jnp.mean(yf, axis=-1, keepdims=True)
    var = jnp.mean(jnp.square(yf - mu), axis=-1, keepdims=True)
    yn = ((yf - mu) * lax.rsqrt(var + LN_EPS)).astype(y.dtype) * ln_g + ln_b
    return jax.nn.silu(yn), ext[:, -CONV_BUF:]


def sb_block(q, q_pos, k, v, k_pos, sb_bias):
    z = jnp.einsum('bqhd,bkhd->bhqk', q, k, preferred_element_type=jnp.float32) * SB_SCALE
    z = z + sb_bias.astype(jnp.float32)[None, :, None, None]
    mask = k_pos[None, :] < q_pos[:, None]
    log_fail = jnp.where(mask, jax.nn.log_sigmoid(-z), 0.0)
    suffix = lax.cumsum(log_fail, axis=3, reverse=True)
    wts = jnp.where(mask, jnp.exp(jax.nn.log_sigmoid(z) + suffix - log_fail), 0.0)
    return jnp.einsum('bhqk,bkhd->bqhd', wts.astype(v.dtype), v)


def stick_breaking(q, k, v, q_pos, k_pos, sb_bias):
    b, t, h, dh = q.shape
    if t > Q_BLOCK and t % Q_BLOCK == 0:
        nb = t // Q_BLOCK
        qb = jnp.moveaxis(q.reshape(b, nb, Q_BLOCK, h, dh), 1, 0)
        pb = q_pos.reshape(nb, Q_BLOCK)
        out = lax.map(lambda blk: sb_block(blk[0], blk[1], k, v, k_pos, sb_bias), (qb, pb))
        return jnp.moveaxis(out, 0, 1).reshape(b, t, h, dh)
    return sb_block(q, q_pos, k, v, k_pos, sb_bias)


def wkv7_scan(r, w, k, v, a, bv, s0):
    def step(s, inp):
        r_t, w_t, k_t, v_t, a_t, b_t = inp
        sa = jnp.einsum('bhij,bhj->bhi', s, a_t)
        s = s * w_t[:, :, None, :] + sa[..., None] * b_t[:, :, None, :] + v_t[..., None] * k_t[:, :, None, :]
        return s, jnp.einsum('bhij,bhj->bhi', s, r_t)
    seq = tuple(jnp.moveaxis(z, 1, 0) for z in (r, w, k, v, a, bv))
    s_fin, ys = lax.scan(step, s0, seq)
    return jnp.moveaxis(ys, 0, 1), s_fin


def rwkv7_mixer(p, shift_row, s0, lw):
    b, t, _ = p.shape
    prev = jnp.concatenate([shift_row.astype(p.dtype), p[:, :-1]], axis=1)
    xs = p + (prev - p) * lw['shift_mu']
    r, k, v, w_dn, a_dn, g_dn = jnp.split(
        xs, [W_MIX, 2 * W_MIX, 3 * W_MIX, 3 * W_MIX + DECAY_LORA, 3 * W_MIX + DECAY_LORA + A_LORA], axis=-1)
    w_log = -jax.nn.softplus(-(lw['decay_w0'] + jnp.tanh(w_dn) @ lw['decay_up']).astype(jnp.float32)) - 0.5
    decay = jnp.exp(-jnp.exp(w_log))
    a = jax.nn.sigmoid(lw['a0'] + a_dn @ lw['a_up'])
    g = jax.nn.sigmoid(g_dn) @ lw['g_up']
    heads = lambda z: z.reshape(b, t, RWKV_HEADS, RWKV_HEAD_DIM).astype(jnp.float32)
    kk = heads(k * lw['k_k'])
    kk = kk / jnp.maximum(jnp.sqrt(jnp.sum(kk * kk, axis=-1, keepdims=True)), 1e-12)
    k = k * (1.0 + (a - 1.0) * lw['k_a'])
    rh, kh, vh, ah = heads(r), heads(k), heads(v), heads(a)
    y, s_new = wkv7_scan(rh, heads(decay), kh, vh, -kk, kk * ah, s0.astype(jnp.float32))
    mu = jnp.mean(y, axis=-1, keepdims=True)
    var = jnp.mean(jnp.square(y - mu), axis=-1, keepdims=True)
    yn = ((y - mu) * lax.rsqrt(var + GN_EPS)).reshape(b, t, W_MIX) * lw['lnx_g'] + lw['lnx_b']
    bonus = jnp.sum(rh * kh * lw['r_k'].astype(jnp.float32), axis=-1, keepdims=True) * vh
    out = (yn + bonus.reshape(b, t, W_MIX)) * g
    return out.astype(p.dtype), p[:, -1:], s_new


def trunk_layer(x, pool_buf, conv_buf, shift_row, wkv0, past_k, past_v, lw):
    x = x + FFN_RESIDUAL * swiglu(rms_norm(x, lw['g_ffn1']), lw['w_ffn1_in'], lw['w_ffn1_out'])
    h = rms_norm(x, lw['g_mix'])
    b, t, _ = h.shape
    proj = h @ lw['w_in']
    p_a = proj[..., OFF_A:OFF_B]
    p_b = proj[..., OFF_B:OFF_C]
    p_c = proj[..., OFF_C:OFF_D]
    p_d = proj[..., OFF_D:OFF_G]
    p_g = proj[..., OFF_G:]
    past_len = 0 if past_k is None else past_k.shape[1]
    y_a, pool_new = pool_mixer(p_a, pool_buf, past_len, lw['w_pool'], lw['pool_scale'])
    y_b, conv_new = conv_mixer(p_b, conv_buf, lw['conv_w'], lw['conv_b'], lw['ln_g'], lw['ln_b'])
    q, k, v = (z.reshape(b, t, SB_HEADS, SB_HEAD_DIM) for z in jnp.split(p_c, 3, axis=-1))
    if past_k is None:
        keys, vals = k, v
    else:
        keys = jnp.concatenate([past_k.astype(k.dtype), k], axis=1)
        vals = jnp.concatenate([past_v.astype(v.dtype), v], axis=1)
    q_pos = past_len + jnp.arange(t)
    k_pos = jnp.arange(keys.shape[1])
    y_c = stick_breaking(q, keys, vals, q_pos, k_pos, lw['sb_bias']).reshape(b, t, W_MIX)
    y_d, shift_new, wkv_new = rwkv7_mixer(p_d, shift_row, wkv0, lw)
    branches = jnp.stack([y_a, y_b, y_c, y_d], axis=2)
    gates = jax.nn.sigmoid(p_g).reshape(b, t, N_BRANCH, D_MODEL)
    merged = jnp.sum(gates * jnp.einsum('btnc,ncd->btnd', branches, lw['w_branch']), axis=2)
    x = x + merged @ lw['w_out']
    x = x + FFN_RESIDUAL * swiglu(rms_norm(x, lw['g_ffn2']), lw['w_ffn2_in'], lw['w_ffn2_out'])
    return x, (k, v, pool_new, conv_new, shift_new, wkv_new)


def setup_inputs(seed: int = 0) -> dict:
    key = jax.random.key(seed)
    ks = list(jax.random.split(key, 48))
    nrm = lambda shape, scale: jax.random.normal(ks.pop(), shape, jnp.float32) * scale
    n_pages = PAST_LEN // PAGE_SIZE
    n_pool = (5 * DEC_BATCH * n_pages + 3) // 4
    perm = jax.random.permutation(ks.pop(), n_pool)
    page_table = perm[:DEC_BATCH * n_pages].reshape(DEC_BATCH, n_pages).astype(jnp.int32)
    return {
        'x_prompt': nrm((BATCH, SEQ, D_MODEL), 1.0),
        'x_sample': nrm((DEC_BATCH, DEC_SEQ, D_MODEL), 1.0),
        'cache_k': nrm((DEPTH, n_pool, PAGE_SIZE, SB_HEADS, SB_HEAD_DIM), 1.0),
        'cache_v': nrm((DEPTH, n_pool, PAGE_SIZE, SB_HEADS, SB_HEAD_DIM), 1.0),
        'state_pool': nrm((DEPTH, DEC_BATCH, POOL_BUF, W_MIX), 1.0),
        'state_conv': nrm((DEPTH, DEC_BATCH, CONV_BUF, W_MIX), 0.5),
        'state_shift': nrm((DEPTH, DEC_BATCH, 1, N_D), 1.0),
        'state_wkv': nrm((DEPTH, DEC_BATCH, RWKV_HEADS, RWKV_HEAD_DIM, RWKV_HEAD_DIM), 0.1),
        'page_table': page_table,
        'g_ffn1': 1.0 + nrm((DEPTH, D_MODEL), 0.02),
        'w_ffn1_in': nrm((DEPTH, D_MODEL, 2 * D_FF), D_MODEL ** -0.5),
        'w_ffn1_out': nrm((DEPTH, D_FF, D_MODEL), D_FF ** -0.5),
        'g_mix': 1.0 + nrm((DEPTH, D_MODEL), 0.02),
        'w_in': nrm((DEPTH, D_MODEL, N_IN), D_MODEL ** -0.5),
        'w_pool': nrm((DEPTH, len(POOL_WINDOWS), POOL_GROUP, POOL_GROUP), POOL_GROUP ** -0.5),
        'pool_scale': 1.0 + nrm((DEPTH, W_MIX), 0.02),
        'conv_w': nrm((DEPTH, CONV_WIDTH, W_MIX), CONV_WIDTH ** -0.5),
        'conv_b': nrm((DEPTH, W_MIX), 0.01),
        'ln_g': 1.0 + nrm((DEPTH, W_MIX), 0.02),
        'ln_b': nrm((DEPTH, W_MIX), 0.01),
        'sb_bias': SB_BIAS_INIT + nrm((DEPTH, SB_HEADS), 0.1),
        'shift_mu': jax.random.uniform(ks.pop(), (DEPTH, N_D), jnp.float32),
        'decay_w0': nrm((DEPTH, W_MIX), 0.5) - 1.0,
        'decay_up': nrm((DEPTH, DECAY_LORA, W_MIX), DECAY_LORA ** -0.5),
        'a0': nrm((DEPTH, W_MIX), 0.1),
        'a_up': nrm((DEPTH, A_LORA, W_MIX), A_LORA ** -0.5),
        'g_up': nrm((DEPTH, GATE_LORA, W_MIX), GATE_LORA ** -0.5),
        'k_k': 0.85 + nrm((DEPTH, W_MIX), 0.05),
        'k_a': 1.0 + nrm((DEPTH, W_MIX), 0.05),
        'r_k': nrm((DEPTH, RWKV_HEADS, RWKV_HEAD_DIM), 0.1),
        'lnx_g': 1.0 + nrm((DEPTH, W_MIX), 0.02),
        'lnx_b': nrm((DEPTH, W_MIX), 0.01),
        'w_branch': nrm((DEPTH, N_BRANCH, W_MIX, D_MODEL), W_MIX ** -0.5),
        'w_out': nrm((DEPTH, D_MODEL, D_MODEL), D_MODEL ** -0.5),
        'g_ffn2': 1.0 + nrm((DEPTH, D_MODEL), 0.02),
        'w_ffn2_in': nrm((DEPTH, D_MODEL, 2 * D_FF), D_MODEL ** -0.5),
        'w_ffn2_out': nrm((DEPTH, D_FF, D_MODEL), D_FF ** -0.5),
        'g_final': 1.0 + nrm((D_MODEL,), 0.02),
    }


def reference(x_prompt, x_sample, cache_k, cache_v, state_pool, state_conv, state_shift, state_wkv,
              page_table, g_ffn1, w_ffn1_in, w_ffn1_out, g_mix, w_in, w_pool, pool_scale, conv_w,
              conv_b, ln_g, ln_b, sb_bias, shift_mu, decay_w0, decay_up, a0, a_up, g_up, k_k, k_a, r_k,
              lnx_g, lnx_b, w_branch, w_out, g_ffn2, w_ffn2_in, w_ffn2_out, g_final):
    xp, xs = x_prompt, x_sample
    bp, bs = xp.shape[0], xs.shape[0]
    dt = xp.dtype
    st_p, st_s = [], []
    for l in range(DEPTH):
        lw = {'g_ffn1': g_ffn1[l], 'w_ffn1_in': w_ffn1_in[l], 'w_ffn1_out': w_ffn1_out[l],
              'g_mix': g_mix[l], 'w_in': w_in[l], 'w_pool': w_pool[l], 'pool_scale': pool_scale[l],
              'conv_w': conv_w[l], 'conv_b': conv_b[l], 'ln_g': ln_g[l], 'ln_b': ln_b[l],
              'sb_bias': sb_bias[l],
              'shift_mu': shift_mu[l], 'decay_w0': decay_w0[l], 'decay_up': decay_up[l],
              'a0': a0[l], 'a_up': a_up[l], 'g_up': g_up[l], 'k_k': k_k[l], 'k_a': k_a[l],
              'r_k': r_k[l], 'lnx_g': lnx_g[l], 'lnx_b': lnx_b[l], 'w_branch': w_branch[l],
              'w_out': w_out[l], 'g_ffn2': g_ffn2[l], 'w_ffn2_in': w_ffn2_in[l],
              'w_ffn2_out': w_ffn2_out[l]}
        xp, sp = trunk_layer(
            xp, jnp.zeros((bp, POOL_BUF, W_MIX), dt), jnp.zeros((bp, CONV_BUF, W_MIX), dt),
            jnp.zeros((bp, 1, N_D), dt),
            jnp.zeros((bp, RWKV_HEADS, RWKV_HEAD_DIM, RWKV_HEAD_DIM), jnp.float32),
            None, None, lw)
        st_p.append(sp)
        past_k = cache_k[l][page_table].reshape(bs, -1, SB_HEADS, SB_HEAD_DIM)
        past_v = cache_v[l][page_table].reshape(bs, -1, SB_HEADS, SB_HEAD_DIM)
        xs, ss = trunk_layer(xs, state_pool[l], state_conv[l], state_shift[l], state_wkv[l],
                             past_k, past_v, lw)
        st_s.append(ss)
    y_prompt = rms_norm(xp, g_final)
    y_sample = rms_norm(xs, g_final)
    k_prompt, v_prompt, pool_prompt, conv_prompt, shift_prompt, wkv_prompt = (
        jnp.stack([s[i] for s in st_p]) for i in range(6))
    k_sample, v_sample, pool_sample, conv_sample, shift_sample, wkv_sample = (
        jnp.stack([s[i] for s in st_s]) for i in range(6))
    return (y_prompt, y_sample, k_prompt, v_prompt, pool_prompt, conv_prompt, shift_prompt, wkv_prompt,
            k_sample, v_sample, pool_sample, conv_sample, shift_sample, wkv_sample)
```

```python
import functools

import jax
import jax.numpy as jnp
from jax import lax
from jax.experimental import pallas as pl
from jax.experimental.pallas import tpu as pltpu

F32 = jnp.float32
BF16 = jnp.bfloat16

LANES = 128
SUBLANES = 8
VMEM_LIMIT_BYTES = 56 * 1024 * 1024

D_MODEL = 2048
W_MIX = D_MODEL // 4
POOL_WINDOWS = (2, 4, 8, 16)
POOL_GROUP = W_MIX // len(POOL_WINDOWS)
POOL_BUF = max(POOL_WINDOWS) - 1
CONV_WIDTH = 31
CONV_BUF = CONV_WIDTH - 1
SB_HEADS = 8
SB_HEAD_DIM = W_MIX // SB_HEADS
SB_SCALE = SB_HEAD_DIM ** -0.5
RWKV_HEAD_DIM = 64
RWKV_HEADS = W_MIX // RWKV_HEAD_DIM
DECAY_LORA = 64
A_LORA = 64
GATE_LORA = 128
N_LORA = DECAY_LORA + A_LORA + GATE_LORA
N_D = 3 * W_MIX + N_LORA
N_BRANCH = 4
OFF_A = 0
OFF_B = OFF_A + W_MIX
OFF_C = OFF_B + 2 * W_MIX
OFF_D = OFF_C + 3 * W_MIX
OFF_G = OFF_D + N_D
N_IN = OFF_G + N_BRANCH * D_MODEL
D_FF = ((8 * D_MODEL // 3 + 127) // 128) * 128
FFN_RESIDUAL = 0.5
RMS_EPS = 1e-6
LN_EPS = 1e-5
GN_EPS = 64e-5
PAGE_SIZE = 128

POOL_HALO = 16
CONV_HALO = 32
SHIFT_HALO = 8
SAMPLE_ROWS = 8
ATTN_BLOCK = 256
WKV_CHUNK = 64


def _call(kernel, l, grid, in_specs, out_specs, out_shape, args, scratch=(), dims=None, prefetch=()):
    n_pre = 1 + len(prefetch)

    def body(*refs):
        kernel(*refs[n_pre:])

    grid_spec = pltpu.PrefetchScalarGridSpec(
        num_scalar_prefetch=n_pre, grid=grid, in_specs=in_specs, out_specs=out_specs,
        scratch_shapes=list(scratch))
    fn = kernel.func if isinstance(kernel, functools.partial) else kernel
    return pl.pallas_call(
        body, grid_spec=grid_spec, out_shape=out_shape, name=fn.__name__.strip("_"),
        compiler_params=pltpu.CompilerParams(
            dimension_semantics=dims or ("arbitrary",) * len(grid),
            vmem_limit_bytes=VMEM_LIMIT_BYTES),
    )(l, *prefetch, *args)


def _sigmoid(x):
    return 1.0 / (1.0 + jnp.exp(-x))


def _softplus(x):
    return jnp.maximum(x, 0.0) + jnp.log1p(jnp.exp(-jnp.abs(x)))


def _split_bf16(x):
    hi = x.astype(BF16)
    lo = (x - hi.astype(F32)).astype(BF16)
    return hi, lo


def _dot(a, b):
    return jnp.dot(a.astype(BF16), b.astype(BF16), preferred_element_type=F32)


def _dot_nt(a, b):
    return lax.dot_general(a.astype(BF16), b.astype(BF16), (((1,), (1,)), ((), ())),
                           preferred_element_type=F32)


def _dot_tn(a, b):
    return lax.dot_general(a.astype(BF16), b.astype(BF16), (((0,), (0,)), ((), ())),
                           preferred_element_type=F32)


def _rmsnorm_kernel(x_ref, g_ref, o_ref):
    x = x_ref[...]
    y = x * lax.rsqrt(jnp.mean(x * x, axis=-1, keepdims=True) + RMS_EPS)
    o_ref[...] = (y * g_ref[...]).astype(o_ref.dtype)


def rmsnorm(l, x, g, out_dtype):
    m, d = x.shape
    tm = min(512, m)
    return _call(
        _rmsnorm_kernel, l, (m // tm,),
        [pl.BlockSpec((tm, d), lambda i, l: (i, 0)),
         pl.BlockSpec((None, 1, d), lambda i, l: (l[0], 0, 0))],
        pl.BlockSpec((tm, d), lambda i, l: (i, 0)),
        jax.ShapeDtypeStruct((m, d), out_dtype), (x, g), dims=("parallel",))


def _ffn_in_kernel(x_ref, wg_ref, wu_ref, o_ref):
    w = jnp.concatenate([wg_ref[...], wu_ref[...]], axis=1).astype(BF16)
    r = jnp.dot(x_ref[...], w, preferred_element_type=F32)
    g = r[:, :LANES]
    o_ref[...] = (g * _sigmoid(g) * r[:, LANES:]).astype(o_ref.dtype)


def ffn_in(l, x, w):
    m, d = x.shape
    d_ff = w.shape[2] // 2
    nb = d_ff // LANES
    tm = min(2048, m)
    return _call(
        _ffn_in_kernel, l, (m // tm, nb),
        [pl.BlockSpec((tm, d), lambda i, j, l: (i, 0)),
         pl.BlockSpec((None, d, LANES), lambda i, j, l: (l[0], 0, j)),
         pl.BlockSpec((None, d, LANES), lambda i, j, l: (l[0], 0, j + nb))],
        pl.BlockSpec((tm, LANES), lambda i, j, l: (i, j)),
        jax.ShapeDtypeStruct((m, d_ff), BF16), (x, w, w), dims=("parallel", "parallel"))


def _mm_kernel(a_ref, w_ref, o_ref):
    o_ref[...] = jnp.dot(a_ref[...], w_ref[...].astype(BF16),
                         preferred_element_type=F32).astype(o_ref.dtype)


def _mm_res_kernel(a_ref, w_ref, r_ref, o_ref, *, scale):
    acc = jnp.dot(a_ref[...], w_ref[...].astype(BF16), preferred_element_type=F32)
    o_ref[...] = r_ref[...] + scale * acc


def matmul(l, a, w, *, tn, res=None, scale=1.0):
    m, k = a.shape
    n = w.shape[2]
    tm = min(1024, m)
    a_spec = pl.BlockSpec((tm, k), lambda i, j, l: (i, 0))
    w_spec = pl.BlockSpec((None, k, tn), lambda i, j, l: (l[0], 0, j))
    o_spec = pl.BlockSpec((tm, tn), lambda i, j, l: (i, j))
    out_shape = jax.ShapeDtypeStruct((m, n), F32)
    dims = ("parallel", "parallel")
    if res is None:
        return _call(_mm_kernel, l, (m // tm, n // tn), [a_spec, w_spec], o_spec, out_shape,
                     (a, w), dims=dims)
    return _call(functools.partial(_mm_res_kernel, scale=scale), l, (m // tm, n // tn),
                 [a_spec, w_spec, o_spec], o_spec, out_shape, (a, w, res), dims=dims)


def _merge_kernel(ya_ref, yb_ref, yc_ref, yd_ref, g0_ref, g1_ref, g2_ref, g3_ref, wb_ref, o_ref):
    acc = None
    for n, (y_ref, g_ref) in enumerate(((ya_ref, g0_ref), (yb_ref, g1_ref), (yc_ref, g2_ref),
                                        (yd_ref, g3_ref))):
        t = _sigmoid(g_ref[...]) * jnp.dot(y_ref[...], wb_ref[n].astype(BF16),
                                           preferred_element_type=F32)
        acc = t if acc is None else acc + t
    o_ref[...] = acc.astype(o_ref.dtype)


def merge(l, ys, proj, w_branch):
    m = proj.shape[0]
    tn = 256
    tm = min(1024, m)
    nj = D_MODEL // tn
    y_spec = pl.BlockSpec((tm, W_MIX), lambda i, j, l: (i, 0))
    g_specs = [pl.BlockSpec((tm, tn), functools.partial(
        lambda i, j, l, n: (i, OFF_G // tn + n * nj + j), n=n)) for n in range(N_BRANCH)]
    return _call(
        _merge_kernel, l, (m // tm, nj),
        [y_spec] * N_BRANCH + g_specs
        + [pl.BlockSpec((None, N_BRANCH, W_MIX, tn), lambda i, j, l: (l[0], 0, 0, j))],
        pl.BlockSpec((tm, tn), lambda i, j, l: (i, j)),
        jax.ShapeDtypeStruct((m, D_MODEL), BF16), (*ys, proj, proj, proj, proj, w_branch),
        dims=("parallel", "parallel"))


def _pool_kernel(u_ref, t0_ref, wp_ref, ps_ref, y_ref, ext_ref, *, tt, pos0):
    i = pl.program_id(1)

    @pl.when(i == 0)
    def _():
        ext_ref[0:POOL_HALO, :] = t0_ref[...]

    @pl.when(i > 0)
    def _():
        ext_ref[0:POOL_HALO, :] = ext_ref[tt:tt + POOL_HALO, :]

    u = u_ref[...]
    ext_ref[POOL_HALO:POOL_HALO + tt, :] = u
    pos = pos0 + i * tt + lax.broadcasted_iota(jnp.int32, (tt, 1), 0)
    ys = []
    for gi, win in enumerate(POOL_WINDOWS):
        cs = slice(gi * POOL_GROUP, (gi + 1) * POOL_GROUP)
        wsum = u[:, cs]
        for k in range(1, win):
            wsum = wsum + ext_ref[POOL_HALO - k:POOL_HALO - k + tt, cs]
        count = jnp.minimum(win, pos + 1).astype(F32)
        d = wsum / count - u[:, cs]
        ys.append(_dot(d, wp_ref[gi]))
    y_ref[...] = (jnp.concatenate(ys, axis=1) * ps_ref[...]).astype(y_ref.dtype)


def pool_mixer(l, proj3, tail0, w_pool, pool_scale, pos0):
    b, t, _ = proj3.shape
    tt = min(512, t)
    return _call(
        functools.partial(_pool_kernel, tt=tt, pos0=pos0), l, (b, t // tt),
        [pl.BlockSpec((None, tt, W_MIX), lambda b, i, l: (b, i, OFF_A // W_MIX)),
         pl.BlockSpec((None, POOL_HALO, W_MIX), lambda b, i, l: (b, 0, 0)),
         pl.BlockSpec((None, len(POOL_WINDOWS), POOL_GROUP, POOL_GROUP), lambda b, i, l: (l[0], 0, 0, 0)),
         pl.BlockSpec((None, 1, W_MIX), lambda b, i, l: (l[0], 0, 0))],
        pl.BlockSpec((None, tt, W_MIX), lambda b, i, l: (b, i, 0)),
        jax.ShapeDtypeStruct((b, t, W_MIX), BF16), (proj3, tail0, w_pool, pool_scale),
        scratch=[pltpu.VMEM((POOL_HALO + tt, W_MIX), F32)], dims=("parallel", "arbitrary"))


def _conv_kernel(val_ref, gate_ref, t0_ref, w_ref, cb_ref, lg_ref, lb_ref, y_ref, u_ref, ext_ref, *, tt):
    i = pl.program_id(1)

    @pl.when(i == 0)
    def _():
        ext_ref[0:CONV_HALO, :] = t0_ref[...]

    @pl.when(i > 0)
    def _():
        ext_ref[0:CONV_HALO, :] = ext_ref[tt:tt + CONV_HALO, :]

    u = val_ref[...] * _sigmoid(gate_ref[...])
    u_ref[...] = u
    ext_ref[CONV_HALO:CONV_HALO + tt, :] = u
    first = CONV_HALO - CONV_BUF
    acc = ext_ref[first:first + tt, :] * w_ref[0:1, :]
    for j in range(1, CONV_WIDTH):
        acc = acc + ext_ref[first + j:first + j + tt, :] * w_ref[j:j + 1, :]
    y = acc + cb_ref[...]
    mu = jnp.mean(y, axis=-1, keepdims=True)
    yc = y - mu
    var = jnp.mean(yc * yc, axis=-1, keepdims=True)
    yn = yc * lax.rsqrt(var + LN_EPS) * lg_ref[...] + lb_ref[...]
    y_ref[...] = (yn * _sigmoid(yn)).astype(y_ref.dtype)


def conv_mixer(l, proj3, tail0, conv_w, conv_b, ln_g, ln_b):
    b, t, _ = proj3.shape
    tt = min(256, t)
    vec = pl.BlockSpec((None, 1, W_MIX), lambda b, i, l: (l[0], 0, 0))
    blk = pl.BlockSpec((None, tt, W_MIX), lambda b, i, l: (b, i, 0))
    return _call(
        functools.partial(_conv_kernel, tt=tt), l, (b, t // tt),
        [pl.BlockSpec((None, tt, W_MIX), lambda b, i, l: (b, i, OFF_B // W_MIX)),
         pl.BlockSpec((None, tt, W_MIX), lambda b, i, l: (b, i, OFF_B // W_MIX + 1)),
         pl.BlockSpec((None, CONV_HALO, W_MIX), lambda b, i, l: (b, 0, 0)),
         pl.BlockSpec((None, CONV_WIDTH, W_MIX), lambda b, i, l: (l[0], 0, 0)),
         vec, vec, vec],
        [blk, blk],
        [jax.ShapeDtypeStruct((b, t, W_MIX), BF16), jax.ShapeDtypeStruct((b, t, W_MIX), F32)],
        (proj3, proj3, tail0, conv_w, conv_b, ln_g, ln_b),
        scratch=[pltpu.VMEM((CONV_HALO + tt, W_MIX), F32)], dims=("parallel", "arbitrary"))


def _sb_block(qh, kh, vh, bias, carry, acc, upper, mask):
    z = _dot_nt(qh, kh) + bias
    log_fail = -_softplus(z)
    if mask is not None:
        log_fail = jnp.where(mask, log_fail, 0.0)
    hi, lo = _split_bf16(log_fail)
    later = (jnp.dot(hi, upper, preferred_element_type=F32)
             + jnp.dot(lo, upper, preferred_element_type=F32))
    wts = jnp.exp(z + log_fail + later + carry)
    if mask is not None:
        wts = jnp.where(mask, wts, 0.0)
    acc = acc + _dot(wts, vh)
    carry = carry + later[:, 0:1] + log_fail[:, 0:1]
    return carry, acc


def _sb_prompt_kernel(bias_ref, q_ref, k_ref, v_ref, o_ref, *, blk):
    qi = pl.program_id(2)
    hp = pl.program_id(1)
    row = lax.broadcasted_iota(jnp.int32, (blk, blk), 0)
    col = lax.broadcasted_iota(jnp.int32, (blk, blk), 1)
    upper = (row > col).astype(BF16)
    causal = row > col
    q = (q_ref[...] * SB_SCALE).astype(BF16)
    heads = q.shape[1] // SB_HEAD_DIM
    outs = []
    for hh in range(heads):
        hs = slice(hh * SB_HEAD_DIM, (hh + 1) * SB_HEAD_DIM)
        qh = q[:, hs]
        bias = bias_ref[0, hp * heads + hh]

        def kv(j, hs=hs):
            start = pl.multiple_of(j * blk, blk)
            return k_ref[pl.ds(start, blk), hs], v_ref[pl.ds(start, blk), hs]

        kh, vh = kv(qi)
        carry = jnp.zeros((blk, 1), F32)
        acc = jnp.zeros((blk, SB_HEAD_DIM), F32)
        carry, acc = _sb_block(qh, kh, vh, bias, carry, acc, upper, causal)

        def body(it, state, qh=qh, bias=bias, kv=kv):
            kh, vh = kv(qi - 1 - it)
            return _sb_block(qh, kh, vh, bias, state[0], state[1], upper, None)

        carry, acc = lax.fori_loop(0, qi, body, (carry, acc))
        outs.append(acc)
    o_ref[...] = jnp.concatenate(outs, axis=1).astype(o_ref.dtype)


def sb_prompt(l, proj3, sb_bias):
    b, t, _ = proj3.shape
    blk = min(ATTN_BLOCK, t)
    pair = LANES
    npair = W_MIX // pair
    c0 = OFF_C // pair
    return _call(
        functools.partial(_sb_prompt_kernel, blk=blk), l, (b, npair, t // blk),
        [pl.BlockSpec((None, 1, SB_HEADS), lambda b, h, i, l: (l[0], 0, 0), memory_space=pltpu.SMEM),
         pl.BlockSpec((None, blk, pair), lambda b, h, i, l: (b, i, c0 + h)),
         pl.BlockSpec((None, t, pair), lambda b, h, i, l: (b, 0, c0 + npair + h)),
         pl.BlockSpec((None, t, pair), lambda b, h, i, l: (b, 0, c0 + 2 * npair + h))],
        pl.BlockSpec((None, blk, pair), lambda b, h, i, l: (b, i, h)),
        jax.ShapeDtypeStruct((b, t, W_MIX), BF16), (sb_bias, proj3, proj3, proj3),
        dims=("parallel", "parallel", "parallel"))


def _sb_decode_kernel(q_ref, bias_ref, k_ref, v_ref, o_ref, carry_ref, acc_ref, *, n_pages):
    p = pl.program_id(1)
    n_keys = PAGE_SIZE * SB_HEADS
    grp = 2 * LANES
    n_grp = n_keys // grp

    @pl.when(p == 0)
    def _():
        carry_ref[...] = jnp.zeros_like(carry_ref)
        acc_ref[...] = jnp.zeros_like(acc_ref)

    kp = k_ref[...].reshape(n_keys, SB_HEAD_DIM)
    vp = v_ref[...].reshape(n_keys, SB_HEAD_DIM)
    q = (q_ref[...] * SB_SCALE).astype(BF16)
    head = lax.broadcasted_iota(jnp.int32, (SB_HEADS, n_keys), 0)
    lane = lax.broadcasted_iota(jnp.int32, (SB_HEADS, n_keys), 1)
    own = (lane % SB_HEADS) == head
    z = _dot_nt(q, kp) + bias_ref[...]
    log_fail = jnp.where(own, -_softplus(z), 0.0)
    r = lax.broadcasted_iota(jnp.int32, (grp, grp), 0) // SB_HEADS
    c = lax.broadcasted_iota(jnp.int32, (grp, grp), 1) // SB_HEADS
    upper = (r > c).astype(BF16)
    parts = [log_fail[:, g * grp:(g + 1) * grp] for g in range(n_grp)]
    stacked = jnp.concatenate(parts, axis=0)
    hi, lo = _split_bf16(stacked)
    later_in = (jnp.dot(hi, upper, preferred_element_type=F32)
                + jnp.dot(lo, upper, preferred_element_type=F32))
    carry = carry_ref[...]
    later = []
    for g in reversed(range(n_grp)):
        later.append(later_in[g * SB_HEADS:(g + 1) * SB_HEADS] + carry)
        carry = carry + jnp.sum(parts[g], axis=-1, keepdims=True)
    later = jnp.concatenate(later[::-1], axis=1)
    carry_ref[...] = carry
    wts = jnp.where(own, jnp.exp(z + log_fail + later), 0.0)
    acc_ref[...] += _dot(wts, vp)

    @pl.when(p == n_pages - 1)
    def _():
        o_ref[...] = acc_ref[...]


def sb_decode(l, q, sb_bias_col, cache_k, cache_v, page_table):
    b = q.shape[0]
    n_pages = page_table.shape[1]
    page = pl.BlockSpec((None, None, PAGE_SIZE, SB_HEADS, SB_HEAD_DIM),
                        lambda b, p, l, pt: (l[0], pt[b, n_pages - 1 - p], 0, 0, 0))
    qo = pl.BlockSpec((None, SB_HEADS, SB_HEAD_DIM), lambda b, p, l, pt: (b, 0, 0))
    return _call(
        functools.partial(_sb_decode_kernel, n_pages=n_pages), l, (b, n_pages),
        [qo, pl.BlockSpec((None, SB_HEADS, 1), lambda b, p, l, pt: (l[0], 0, 0)), page, page],
        qo, jax.ShapeDtypeStruct((b, SB_HEADS, SB_HEAD_DIM), F32),
        (q, sb_bias_col, cache_k, cache_v),
        scratch=[pltpu.VMEM((SB_HEADS, 1), F32), pltpu.VMEM((SB_HEADS, SB_HEAD_DIM), F32)],
        dims=("parallel", "arbitrary"), prefetch=(page_table,))


def _wkv_pre_kernel(pr_ref, pk_ref, pv_ref, pl_ref, s0_ref, mu_ref, w0_ref, dup_ref, a0_ref, aup_ref,
                    gup_ref, r_ref, k_ref, v_ref, ld_ref, al_ref, g_ref, ext_ref, *, tt, t_valid):
    i = pl.program_id(1)

    @pl.when(i == 0)
    def _():
        ext_ref[0:SHIFT_HALO, :] = s0_ref[...]

    @pl.when(i > 0)
    def _():
        ext_ref[0:SHIFT_HALO, :] = ext_ref[tt:tt + SHIFT_HALO, :]

    rows = slice(SHIFT_HALO, SHIFT_HALO + tt)
    ext_ref[rows, 0:W_MIX] = pr_ref[...]
    ext_ref[rows, W_MIX:2 * W_MIX] = pk_ref[...]
    ext_ref[rows, 2 * W_MIX:3 * W_MIX] = pv_ref[...]
    ext_ref[rows, 3 * W_MIX:N_D] = pl_ref[...]
    p = ext_ref[rows, :]
    prev = ext_ref[SHIFT_HALO - 1:SHIFT_HALO - 1 + tt, :]
    xs = p + (prev - p) * mu_ref[...]
    o = 3 * W_MIX
    w_dn = xs[:, o:o + DECAY_LORA]
    a_dn = xs[:, o + DECAY_LORA:o + DECAY_LORA + A_LORA]
    g_dn = xs[:, o + DECAY_LORA + A_LORA:N_D]
    w_log = -_softplus(-(w0_ref[...] + _dot(jnp.tanh(w_dn), dup_ref[...]))) - 0.5
    log_decay = -jnp.exp(w_log)
    alpha = _sigmoid(a0_ref[...] + _dot(a_dn, aup_ref[...]))
    k = xs[:, W_MIX:2 * W_MIX]
    v = xs[:, 2 * W_MIX:3 * W_MIX]
    if t_valid is not None:
        valid = (i * tt + lax.broadcasted_iota(jnp.int32, (tt, 1), 0)) < t_valid
        k = jnp.where(valid, k, 0.0)
        v = jnp.where(valid, v, 0.0)
        log_decay = jnp.where(valid, log_decay, 0.0)
    r_ref[...] = xs[:, 0:W_MIX]
    k_ref[...] = k
    v_ref[...] = v
    ld_ref[...] = log_decay
    al_ref[...] = alpha
    g_ref[...] = _dot(_sigmoid(g_dn), gup_ref[...])


def wkv_pre(l, proj3, shift0, lw, t_valid):
    b, t, _ = proj3.shape
    tt = min(256, t)
    cd = OFF_D // W_MIX
    vec = lambda n: pl.BlockSpec((None, 1, n), lambda b, i, l: (l[0], 0, 0))
    up = lambda n: pl.BlockSpec((None, n, W_MIX), lambda b, i, l: (l[0], 0, 0))
    blk = pl.BlockSpec((None, tt, W_MIX), lambda b, i, l: (b, i, 0))
    return _call(
        functools.partial(_wkv_pre_kernel, tt=tt, t_valid=t_valid), l, (b, t // tt),
        [pl.BlockSpec((None, tt, W_MIX), lambda b, i, l: (b, i, cd)),
         pl.BlockSpec((None, tt, W_MIX), lambda b, i, l: (b, i, cd + 1)),
         pl.BlockSpec((None, tt, W_MIX), lambda b, i, l: (b, i, cd + 2)),
         pl.BlockSpec((None, tt, N_LORA), lambda b, i, l: (b, i, (OFF_D + 3 * W_MIX) // N_LORA)),
         pl.BlockSpec((None, SHIFT_HALO, N_D), lambda b, i, l: (b, 0, 0)),
         vec(N_D), vec(W_MIX), up(DECAY_LORA), vec(W_MIX), up(A_LORA), up(GATE_LORA)],
        [blk] * 6, [jax.ShapeDtypeStruct((b, t, W_MIX), F32)] * 6,
        (proj3, proj3, proj3, proj3, shift0, lw['shift_mu'], lw['decay_w0'], lw['decay_up'],
         lw['a0'], lw['a_up'], lw['g_up']),
        scratch=[pltpu.VMEM((SHIFT_HALO + tt, N_D), F32)], dims=("parallel", "arbitrary"))


def _wkv_scan_kernel(r_ref, k_ref, v_ref, ld_ref, al_ref, g_ref, s0_ref, kk_ref, ka_ref, rk_ref,
                     lg_ref, lb_ref, y_ref, so_ref, s_ref, *, ch, n_chunks):
    c = pl.program_id(1)

    @pl.when(c == 0)
    def _():
        s_ref[...] = s0_ref[...]

    row = lax.broadcasted_iota(jnp.int32, (ch, ch), 0)
    col = lax.broadcasted_iota(jnp.int32, (ch, ch), 1)
    incl = row >= col
    eye = (row == col).astype(F32)
    row2 = lax.broadcasted_iota(jnp.int32, (2 * ch, 2 * ch), 0)
    col2 = lax.broadcasted_iota(jnp.int32, (2 * ch, 2 * ch), 1) & (ch - 1)
    tri = (row2 & (ch - 1)) + (row2 // ch) > col2
    ld = ld_ref[...]
    hi, lo = _split_bf16(ld)
    lower = incl.astype(BF16)
    cum = (jnp.dot(lower, hi, preferred_element_type=F32)
           + jnp.dot(lower, lo, preferred_element_type=F32))
    cum_end = cum[ch - 1:ch, :]
    g_in = jnp.exp(cum)
    g_ex = jnp.exp(cum - ld)
    g_inv = jnp.exp(-cum)
    g_end = jnp.exp(cum_end - cum)
    g_tot = jnp.exp(cum_end)
    r = r_ref[...]
    k = k_ref[...]
    v = v_ref[...]
    al = al_ref[...]
    kk = k * kk_ref[...]
    k2 = k * (1.0 + (al - 1.0) * ka_ref[...])
    rk = r * k2 * rk_ref[...]
    outs = []
    for h in range(RWKV_HEADS):
        hs = slice(h * RWKV_HEAD_DIM, (h + 1) * RWKV_HEAD_DIM)
        kkh = kk[:, hs]
        kkh = kkh / jnp.maximum(jnp.sqrt(jnp.sum(kkh * kkh, axis=-1, keepdims=True)), 1e-12)
        a_h = -kkh
        b_h = kkh * al[:, hs]
        kh = k2[:, hs]
        vh = v[:, hs]
        at = a_h * g_ex[:, hs]
        rt = r[:, hs] * g_in[:, hs]
        bt = b_h * g_inv[:, hs]
        kt = kh * g_inv[:, hs]
        prod = _dot_nt(jnp.concatenate([at, rt], axis=0), jnp.concatenate([bt, kt], axis=0))
        prod = jnp.where(tri, prod, 0.0)
        x = prod[0:ch, 0:ch]
        inv = eye + x
        n_sq = ch.bit_length() - 2
        for _ in range(n_sq):
            x = _dot(x, x)
            inv = inv + _dot(inv, x)
        s_h = s_ref[h]
        zeros = jnp.zeros_like(vh)
        w_rhs = _dot_nt(at, s_h) + _dot(prod[0:ch], jnp.concatenate([zeros, vh], axis=0))
        u = _dot(inv, w_rhs)
        uv = jnp.concatenate([u, vh], axis=0)
        y = _dot_nt(rt, s_h) + _dot(prod[ch:2 * ch], uv)
        bk_end = jnp.concatenate([b_h * g_end[:, hs], kh * g_end[:, hs]], axis=0)
        s_ref[h] = s_h * g_tot[:, hs] + _dot_tn(uv, bk_end)
        mu = jnp.mean(y, axis=-1, keepdims=True)
        yc = y - mu
        var = jnp.mean(yc * yc, axis=-1, keepdims=True)
        yn = yc * lax.rsqrt(var + GN_EPS) * lg_ref[:, hs] + lb_ref[:, hs]
        bonus = jnp.sum(rk[:, hs], axis=-1, keepdims=True) * vh
        outs.append((yn + bonus) * g_ref[:, hs])
    y_ref[...] = jnp.concatenate(outs, axis=1).astype(y_ref.dtype)

    @pl.when(c == n_chunks - 1)
    def _():
        so_ref[...] = s_ref[...]


def wkv_scan(l, r, k, v, ld, al, g, s0, lw):
    b, t, _ = r.shape
    ch = min(WKV_CHUNK, t)
    blk = pl.BlockSpec((None, ch, W_MIX), lambda b, c, l: (b, c, 0))
    st = pl.BlockSpec((None, RWKV_HEADS, RWKV_HEAD_DIM, RWKV_HEAD_DIM), lambda b, c, l: (b, 0, 0, 0))
    vec = pl.BlockSpec((None, 1, W_MIX), lambda b, c, l: (l[0], 0, 0))
    return _call(
        functools.partial(_wkv_scan_kernel, ch=ch, n_chunks=t // ch), l, (b, t // ch),
        [blk] * 6 + [st] + [vec] * 5, [blk, st],
        [jax.ShapeDtypeStruct((b, t, W_MIX), BF16),
         jax.ShapeDtypeStruct((b, RWKV_HEADS, RWKV_HEAD_DIM, RWKV_HEAD_DIM), F32)],
        (r, k, v, ld, al, g, s0, lw['k_k'], lw['k_a'], lw['r_k'], lw['lnx_g'], lw['lnx_b']),
        scratch=[pltpu.VMEM((RWKV_HEADS, RWKV_HEAD_DIM, RWKV_HEAD_DIM), F32)],
        dims=("parallel", "arbitrary"))


def _trunk_layer(l, x, pool0, conv0, shift0, wkv0, pos0, t_valid, attend, lw):
    b, t, d = x.shape
    m = b * t
    x2 = x.reshape(m, d)
    h = rmsnorm(l, x2, lw['g_ffn1'], BF16)
    x2 = matmul(l, ffn_in(l, h, lw['w_ffn1_in']), lw['w_ffn1_out'], tn=256, res=x2, scale=FFN_RESIDUAL)
    h = rmsnorm(l, x2, lw['g_mix'], BF16)
    proj = matmul(l, h, lw['w_in'], tn=768)
    proj3 = proj.reshape(b, t, N_IN)
    y_a = pool_mixer(l, proj3, pool0, lw['w_pool'], lw['pool_scale'], pos0)
    y_b, glu = conv_mixer(l, proj3, conv0, lw['conv_w'], lw['conv_b'], lw['ln_g'], lw['ln_b'])
    y_c = attend(l, proj3)
    r, k, v, ld, al, g = wkv_pre(l, proj3, shift0, lw, t_valid)
    y_d, wkv_new = wkv_scan(l, r, k, v, ld, al, g, wkv0, lw)
    ys = [y.reshape(m, W_MIX) for y in (y_a, y_b, y_c, y_d)]
    merged = merge(l, ys, proj, lw['w_branch'])
    x2 = matmul(l, merged, lw['w_out'], tn=256, res=x2, scale=1.0)
    h = rmsnorm(l, x2, lw['g_ffn2'], BF16)
    x2 = matmul(l, ffn_in(l, h, lw['w_ffn2_in']), lw['w_ffn2_out'], tn=256, res=x2, scale=FFN_RESIDUAL)
    return x2.reshape(b, t, d), proj3, glu, wkv_new


def _front_pad(rows, height):
    return jnp.pad(rows, ((0, 0), (height - rows.shape[1], 0), (0, 0)))


def kernel(x_prompt, x_sample, cache_k, cache_v, state_pool, state_conv, state_shift, state_wkv,
           page_table, g_ffn1, w_ffn1_in, w_ffn1_out, g_mix, w_in, w_pool, pool_scale, conv_w,
           conv_b, ln_g, ln_b, sb_bias, shift_mu, decay_w0, decay_up, a0, a_up, g_up, k_k, k_a, r_k,
           lnx_g, lnx_b, w_branch, w_out, g_ffn2, w_ffn2_in, w_ffn2_out, g_final):
    depth = w_in.shape[0]
    bp, tp, d = x_prompt.shape
    bs, ts, _ = x_sample.shape
    assert d == D_MODEL and ts == 1
    past_len = page_table.shape[1] * PAGE_SIZE
    row = lambda a: a.reshape(depth, 1, -1)
    lw = {
        'g_ffn1': row(g_ffn1), 'w_ffn1_in': w_ffn1_in, 'w_ffn1_out': w_ffn1_out, 'g_mix': row(g_mix),
        'w_in': w_in, 'w_pool': w_pool, 'pool_scale': row(pool_scale), 'conv_w': conv_w,
        'conv_b': row(conv_b), 'ln_g': row(ln_g), 'ln_b': row(ln_b), 'sb_bias': row(sb_bias),
        'shift_mu': row(shift_mu), 'decay_w0': row(decay_w0), 'decay_up': decay_up, 'a0': row(a0),
        'a_up': a_up, 'g_up': g_up, 'k_k': row(k_k), 'k_a': row(k_a), 'r_k': row(r_k),
        'lnx_g': row(lnx_g), 'lnx_b': row(lnx_b), 'w_branch': w_branch, 'w_out': w_out,
        'g_ffn2': row(g_ffn2), 'w_ffn2_in': w_ffn2_in, 'w_ffn2_out': w_ffn2_out,
    }
    sb_bias_col = sb_bias.reshape(depth, SB_HEADS, 1)
    xs0 = jnp.pad(x_sample, ((0, 0), (0, SAMPLE_ROWS - ts), (0, 0)))

    def layer(carry, li):
        xp, xs = carry
        l = li.reshape(1)
        xp, proj_p, glu_p, wkv_p = _trunk_layer(
            l, xp, jnp.zeros((bp, POOL_HALO, W_MIX), F32), jnp.zeros((bp, CONV_HALO, W_MIX), F32),
            jnp.zeros((bp, SHIFT_HALO, N_D), F32),
            jnp.zeros((bp, RWKV_HEADS, RWKV_HEAD_DIM, RWKV_HEAD_DIM), F32), 0, None,
            lambda l, p3: sb_prompt(l, p3, lw['sb_bias']), lw)
        pool_s = lax.dynamic_index_in_dim(state_pool, li, 0, keepdims=False)
        conv_s = lax.dynamic_index_in_dim(state_conv, li, 0, keepdims=False)
        shift_s = lax.dynamic_index_in_dim(state_shift, li, 0, keepdims=False)
        wkv_s = lax.dynamic_index_in_dim(state_wkv, li, 0, keepdims=False)

        def attend_sample(l, p3):
            q = p3[:, 0, OFF_C:OFF_C + W_MIX].reshape(bs, SB_HEADS, SB_HEAD_DIM)
            y = sb_decode(l, q, sb_bias_col, cache_k, cache_v, page_table).reshape(bs, 1, W_MIX)
            return jnp.pad(y, ((0, 0), (0, SAMPLE_ROWS - 1), (0, 0))).astype(BF16)

        xs, proj_s, glu_s, wkv_s_new = _trunk_layer(
            l, xs, _front_pad(pool_s, POOL_HALO), _front_pad(conv_s, CONV_HALO),
            _front_pad(shift_s, SHIFT_HALO), wkv_s, past_len, ts, attend_sample, lw)

        kv = lambda p3, t: (p3[:, :t, OFF_C + W_MIX:OFF_C + 2 * W_MIX].reshape(-1, t, SB_HEADS, SB_HEAD_DIM),
                            p3[:, :t, OFF_C + 2 * W_MIX:OFF_C + 3 * W_MIX].reshape(-1, t, SB_HEADS, SB_HEAD_DIM))
        k_p, v_p = kv(proj_p, tp)
        k_s, v_s = kv(proj_s, ts)
        outs = (
            k_p, v_p, proj_p[:, tp - POOL_BUF:, OFF_A:OFF_A + W_MIX], glu_p[:, tp - CONV_BUF:],
            proj_p[:, tp - 1:, OFF_D:OFF_D + N_D], wkv_p,
            k_s, v_s,
            jnp.concatenate([pool_s, proj_s[:, :ts, OFF_A:OFF_A + W_MIX]], axis=1)[:, -POOL_BUF:],
            jnp.concatenate([conv_s, glu_s[:, :ts]], axis=1)[:, -CONV_BUF:],
            proj_s[:, ts - 1:ts, OFF_D:OFF_D + N_D], wkv_s_new)
        return (xp, xs), outs

    (xp, xs), st = lax.scan(layer, (x_prompt, xs0), jnp.arange(depth, dtype=jnp.int32))
    zero = jnp.zeros((1,), jnp.int32)
    g_fin = g_final.reshape(1, 1, d)
    y_prompt = rmsnorm(zero, xp.reshape(bp * tp, d), g_fin, F32).reshape(bp, tp, d)
    y_sample = rmsnorm(zero, xs.reshape(bs * SAMPLE_ROWS, d), g_fin, F32).reshape(bs, SAMPLE_ROWS, d)[:, :ts]
    return (y_prompt, y_sample) + tuple(st)
```

```python
import functools

import jax
import jax.numpy as jnp
from jax import lax
from jax.experimental import pallas as pl
from jax.experimental.pallas import tpu as pltpu

F32 = jnp.float32
BF16 = jnp.bfloat16

LANES = 128
SUBLANES = 8
VMEM_LIMIT_BYTES = 56 * 1024 * 1024

D_MODEL = 2048
W_MIX = D_MODEL // 4
POOL_WINDOWS = (2, 4, 8, 16)
POOL_GROUP = W_MIX // len(POOL_WINDOWS)
POOL_BUF = max(POOL_WINDOWS) - 1
CONV_WIDTH = 31
CONV_BUF = CONV_WIDTH - 1
SB_HEADS = 8
SB_HEAD_DIM = W_MIX // SB_HEADS
SB_SCALE = SB_HEAD_DIM ** -0.5
RWKV_HEAD_DIM = 64
RWKV_HEADS = W_MIX // RWKV_HEAD_DIM
DECAY_LORA = 64
A_LORA = 64
GATE_LORA = 128
N_LORA = DECAY_LORA + A_LORA + GATE_LORA
N_D = 3 * W_MIX + N_LORA
N_BRANCH = 4
OFF_A = 0
OFF_B = OFF_A + W_MIX
OFF_C = OFF_B + 2 * W_MIX
OFF_D = OFF_C + 3 * W_MIX
OFF_G = OFF_D + N_D
N_IN = OFF_G + N_BRANCH * D_MODEL
D_FF = ((8 * D_MODEL // 3 + 127) // 128) * 128
FFN_RESIDUAL = 0.5
RMS_EPS = 1e-6
LN_EPS = 1e-5
GN_EPS = 64e-5
PAGE_SIZE = 128

POOL_HALO = 16
CONV_HALO = 32
SHIFT_HALO = 8
SAMPLE_ROWS = 8
ATTN_BLOCK = 256
SB_GROUP = 4
DECODE_PAGES = 16
WKV_CHUNK = 64


def _call(kernel, l, grid, in_specs, out_specs, out_shape, args, scratch=(), dims=None, prefetch=()):
    n_pre = 1 + len(prefetch)

    def body(*refs):
        kernel(*refs[n_pre:])

    grid_spec = pltpu.PrefetchScalarGridSpec(
        num_scalar_prefetch=n_pre, grid=grid, in_specs=in_specs, out_specs=out_specs,
        scratch_shapes=list(scratch))
    fn = kernel.func if isinstance(kernel, functools.partial) else kernel
    return pl.pallas_call(
        body, grid_spec=grid_spec, out_shape=out_shape, name=fn.__name__.strip("_"),
        compiler_params=pltpu.CompilerParams(
            dimension_semantics=dims or ("arbitrary",) * len(grid),
            vmem_limit_bytes=VMEM_LIMIT_BYTES),
    )(l, *prefetch, *args)


def _sigmoid(x):
    return 1.0 / (1.0 + jnp.exp(-x))


def _softplus(x):
    return jnp.maximum(x, 0.0) + jnp.log1p(jnp.exp(-jnp.abs(x)))


def _split_bf16(x):
    hi = x.astype(BF16)
    lo = (x - hi.astype(F32)).astype(BF16)
    return hi, lo


def _dot(a, b):
    return jnp.dot(a.astype(BF16), b.astype(BF16), preferred_element_type=F32)


def _dot_nt(a, b):
    return lax.dot_general(a.astype(BF16), b.astype(BF16), (((1,), (1,)), ((), ())),
                           preferred_element_type=F32)


def _dot_tn(a, b):
    return lax.dot_general(a.astype(BF16), b.astype(BF16), (((0,), (0,)), ((), ())),
                           preferred_element_type=F32)


def _rmsnorm_kernel(x_ref, g_ref, o_ref):
    x = x_ref[...]
    y = x * lax.rsqrt(jnp.mean(x * x, axis=-1, keepdims=True) + RMS_EPS)
    o_ref[...] = (y * g_ref[...]).astype(o_ref.dtype)


def rmsnorm(l, x, g, out_dtype):
    m, d = x.shape
    tm = min(512, m)
    return _call(
        _rmsnorm_kernel, l, (m // tm,),
        [pl.BlockSpec((tm, d), lambda i, l: (i, 0)),
         pl.BlockSpec((None, 1, d), lambda i, l: (l[0], 0, 0))],
        pl.BlockSpec((tm, d), lambda i, l: (i, 0)),
        jax.ShapeDtypeStruct((m, d), out_dtype), (x, g), dims=("parallel",))


def _ffn_in_kernel(x_ref, wg_ref, wu_ref, o_ref):
    w = jnp.concatenate([wg_ref[...], wu_ref[...]], axis=1).astype(BF16)
    r = jnp.dot(x_ref[...], w, preferred_element_type=F32)
    g = r[:, :LANES]
    o_ref[...] = (g * _sigmoid(g) * r[:, LANES:]).astype(o_ref.dtype)


def ffn_in(l, x, w):
    m, d = x.shape
    d_ff = w.shape[2] // 2
    nb = d_ff // LANES
    tm = min(2048, m)
    return _call(
        _ffn_in_kernel, l, (m // tm, nb),
        [pl.BlockSpec((tm, d), lambda i, j, l: (i, 0)),
         pl.BlockSpec((None, d, LANES), lambda i, j, l: (l[0], 0, j)),
         pl.BlockSpec((None, d, LANES), lambda i, j, l: (l[0], 0, j + nb))],
        pl.BlockSpec((tm, LANES), lambda i, j, l: (i, j)),
        jax.ShapeDtypeStruct((m, d_ff), BF16), (x, w, w), dims=("parallel", "parallel"))


def _mm_kernel(a_ref, w_ref, o_ref):
    o_ref[...] = jnp.dot(a_ref[...], w_ref[...].astype(BF16),
                         preferred_element_type=F32).astype(o_ref.dtype)


def _mm_res_kernel(a_ref, w_ref, r_ref, o_ref, *, scale):
    acc = jnp.dot(a_ref[...], w_ref[...].astype(BF16), preferred_element_type=F32)
    o_ref[...] = r_ref[...] + scale * acc


def matmul(l, a, w, *, tn, res=None, scale=1.0):
    m, k = a.shape
    n = w.shape[2]
    tm = min(1024, m)
    a_spec = pl.BlockSpec((tm, k), lambda i, j, l: (i, 0))
    w_spec = pl.BlockSpec((None, k, tn), lambda i, j, l: (l[0], 0, j))
    o_spec = pl.BlockSpec((tm, tn), lambda i, j, l: (i, j))
    out_shape = jax.ShapeDtypeStruct((m, n), F32)
    dims = ("parallel", "parallel")
    if res is None:
        return _call(_mm_kernel, l, (m // tm, n // tn), [a_spec, w_spec], o_spec, out_shape,
                     (a, w), dims=dims)
    return _call(functools.partial(_mm_res_kernel, scale=scale), l, (m // tm, n // tn),
                 [a_spec, w_spec, o_spec], o_spec, out_shape, (a, w, res), dims=dims)


def _merge_kernel(ya_ref, yb_ref, yc_ref, yd_ref, g0_ref, g1_ref, g2_ref, g3_ref, wb_ref, o_ref):
    acc = None
    for n, (y_ref, g_ref) in enumerate(((ya_ref, g0_ref), (yb_ref, g1_ref), (yc_ref, g2_ref),
                                        (yd_ref, g3_ref))):
        t = _sigmoid(g_ref[...]) * jnp.dot(y_ref[...], wb_ref[n].astype(BF16),
                                           preferred_element_type=F32)
        acc = t if acc is None else acc + t
    o_ref[...] = acc.astype(o_ref.dtype)


def merge(l, ys, proj, w_branch):
    m = proj.shape[0]
    tn = 256
    tm = min(1024, m)
    nj = D_MODEL // tn
    y_spec = pl.BlockSpec((tm, W_MIX), lambda i, j, l: (i, 0))
    g_specs = [pl.BlockSpec((tm, tn), functools.partial(
        lambda i, j, l, n: (i, OFF_G // tn + n * nj + j), n=n)) for n in range(N_BRANCH)]
    return _call(
        _merge_kernel, l, (m // tm, nj),
        [y_spec] * N_BRANCH + g_specs
        + [pl.BlockSpec((None, N_BRANCH, W_MIX, tn), lambda i, j, l: (l[0], 0, 0, j))],
        pl.BlockSpec((tm, tn), lambda i, j, l: (i, j)),
        jax.ShapeDtypeStruct((m, D_MODEL), BF16), (*ys, proj, proj, proj, proj, w_branch),
        dims=("parallel", "parallel"))


def _pool_kernel(u_ref, t0_ref, wp_ref, ps_ref, y_ref, ext_ref, *, tt, pos0):
    i = pl.program_id(1)

    @pl.when(i == 0)
    def _():
        ext_ref[0:POOL_HALO, :] = t0_ref[...]

    @pl.when(i > 0)
    def _():
        ext_ref[0:POOL_HALO, :] = ext_ref[tt:tt + POOL_HALO, :]

    u = u_ref[...]
    ext_ref[POOL_HALO:POOL_HALO + tt, :] = u
    pos = pos0 + i * tt + lax.broadcasted_iota(jnp.int32, (tt, 1), 0)
    ys = []
    for gi, win in enumerate(POOL_WINDOWS):
        cs = slice(gi * POOL_GROUP, (gi + 1) * POOL_GROUP)
        wsum = u[:, cs]
        for k in range(1, win):
            wsum = wsum + ext_ref[POOL_HALO - k:POOL_HALO - k + tt, cs]
        count = jnp.minimum(win, pos + 1).astype(F32)
        d = wsum / count - u[:, cs]
        ys.append(_dot(d, wp_ref[gi]))
    y_ref[...] = (jnp.concatenate(ys, axis=1) * ps_ref[...]).astype(y_ref.dtype)


def pool_mixer(l, proj3, tail0, w_pool, pool_scale, pos0):
    b, t, _ = proj3.shape
    tt = min(512, t)
    return _call(
        functools.partial(_pool_kernel, tt=tt, pos0=pos0), l, (b, t // tt),
        [pl.BlockSpec((None, tt, W_MIX), lambda b, i, l: (b, i, OFF_A // W_MIX)),
         pl.BlockSpec((None, POOL_HALO, W_MIX), lambda b, i, l: (b, 0, 0)),
         pl.BlockSpec((None, len(POOL_WINDOWS), POOL_GROUP, POOL_GROUP), lambda b, i, l: (l[0], 0, 0, 0)),
         pl.BlockSpec((None, 1, W_MIX), lambda b, i, l: (l[0], 0, 0))],
        pl.BlockSpec((None, tt, W_MIX), lambda b, i, l: (b, i, 0)),
        jax.ShapeDtypeStruct((b, t, W_MIX), BF16), (proj3, tail0, w_pool, pool_scale),
        scratch=[pltpu.VMEM((POOL_HALO + tt, W_MIX), F32)], dims=("parallel", "arbitrary"))


def _conv_kernel(val_ref, gate_ref, t0_ref, w_ref, cb_ref, lg_ref, lb_ref, y_ref, u_ref, ext_ref, *, tt):
    i = pl.program_id(1)

    @pl.when(i == 0)
    def _():
        ext_ref[0:CONV_HALO, :] = t0_ref[...]

    @pl.when(i > 0)
    def _():
        ext_ref[0:CONV_HALO, :] = ext_ref[tt:tt + CONV_HALO, :]

    u = val_ref[...] * _sigmoid(gate_ref[...])
    u_ref[...] = u
    ext_ref[CONV_HALO:CONV_HALO + tt, :] = u
    first = CONV_HALO - CONV_BUF
    acc = ext_ref[first:first + tt, :] * w_ref[0:1, :]
    for j in range(1, CONV_WIDTH):
        acc = acc + ext_ref[first + j:first + j + tt, :] * w_ref[j:j + 1, :]
    y = acc + cb_ref[...]
    mu = jnp.mean(y, axis=-1, keepdims=True)
    yc = y - mu
    var = jnp.mean(yc * yc, axis=-1, keepdims=True)
    yn = yc * lax.rsqrt(var + LN_EPS) * lg_ref[...] + lb_ref[...]
    y_ref[...] = (yn * _sigmoid(yn)).astype(y_ref.dtype)


def conv_mixer(l, proj3, tail0, conv_w, conv_b, ln_g, ln_b):
    b, t, _ = proj3.shape
    tt = min(256, t)
    vec = pl.BlockSpec((None, 1, W_MIX), lambda b, i, l: (l[0], 0, 0))
    blk = pl.BlockSpec((None, tt, W_MIX), lambda b, i, l: (b, i, 0))
    return _call(
        functools.partial(_conv_kernel, tt=tt), l, (b, t // tt),
        [pl.BlockSpec((None, tt, W_MIX), lambda b, i, l: (b, i, OFF_B // W_MIX)),
         pl.BlockSpec((None, tt, W_MIX), lambda b, i, l: (b, i, OFF_B // W_MIX + 1)),
         pl.BlockSpec((None, CONV_HALO, W_MIX), lambda b, i, l: (b, 0, 0)),
         pl.BlockSpec((None, CONV_WIDTH, W_MIX), lambda b, i, l: (l[0], 0, 0)),
         vec, vec, vec],
        [blk, blk],
        [jax.ShapeDtypeStruct((b, t, W_MIX), BF16), jax.ShapeDtypeStruct((b, t, W_MIX), F32)],
        (proj3, proj3, tail0, conv_w, conv_b, ln_g, ln_b),
        scratch=[pltpu.VMEM((CONV_HALO + tt, W_MIX), F32)], dims=("parallel", "arbitrary"))


def _sb_heads(x):
    n = x.shape[1] // SB_HEAD_DIM
    return jnp.stack([x[:, h * SB_HEAD_DIM:(h + 1) * SB_HEAD_DIM] for h in range(n)], axis=0)


def _sb_block(q3, k3, v3, bias, carry, acc, upper, mask):
    nh, tq, _ = q3.shape
    tk = k3.shape[1]
    z = _bdot_nt(q3, k3)
    z = jnp.stack([z[h] + bias[h] for h in range(nh)], axis=0)
    log_fail = -(jnp.maximum(z, 0.0) + jnp.log(1.0 + jnp.exp(-jnp.abs(z))))
    if mask is not None:
        log_fail = jnp.where(mask, log_fail, 0.0)
    hi, lo = _split_bf16(log_fail.reshape(nh * tq, tk))
    later = (jnp.dot(hi, upper, preferred_element_type=F32)
             + jnp.dot(lo, upper, preferred_element_type=F32)).reshape(nh, tq, tk)
    wts = jnp.exp(z + log_fail + later + carry)
    if mask is not None:
        wts = jnp.where(mask, wts, 0.0)
    acc = acc + _bdot(wts, v3)
    carry = carry + later[:, :, 0:1] + log_fail[:, :, 0:1]
    return carry, acc


def _sb_prompt_kernel(bias_ref, q_ref, k_ref, v_ref, o_ref, *, blk):
    qi = pl.program_id(2)
    grp = pl.program_id(1)
    row = lax.broadcasted_iota(jnp.int32, (blk, blk), 0)
    col = lax.broadcasted_iota(jnp.int32, (blk, blk), 1)
    upper = (row > col).astype(BF16)
    causal = row > col
    q3 = _sb_heads((q_ref[...] * SB_SCALE).astype(BF16))
    nh = q3.shape[0]
    bias = [bias_ref[0, grp * nh + h] for h in range(nh)]

    def kv(j):
        start = pl.multiple_of(j * blk, blk)
        return (_sb_heads(k_ref[pl.ds(start, blk), :].astype(BF16)),
                _sb_heads(v_ref[pl.ds(start, blk), :].astype(BF16)))

    carry = jnp.zeros((nh, blk, 1), F32)
    acc = jnp.zeros((nh, blk, SB_HEAD_DIM), F32)
    carry, acc = _sb_block(q3, *kv(qi), bias, carry, acc, upper, causal)

    def body(it, state):
        return _sb_block(q3, *kv(qi - 1 - it), bias, state[0], state[1], upper, None)

    carry, acc = lax.fori_loop(0, qi, body, (carry, acc))
    o_ref[...] = jnp.concatenate([acc[h] for h in range(nh)], axis=1).astype(o_ref.dtype)


def sb_prompt(l, proj3, sb_bias):
    b, t, _ = proj3.shape
    blk = min(ATTN_BLOCK, t)
    wid = SB_GROUP * SB_HEAD_DIM
    ngrp = W_MIX // wid
    c0 = OFF_C // wid
    return _call(
        functools.partial(_sb_prompt_kernel, blk=blk), l, (b, ngrp, t // blk),
        [pl.BlockSpec((None, 1, SB_HEADS), lambda b, h, i, l: (l[0], 0, 0), memory_space=pltpu.SMEM),
         pl.BlockSpec((None, blk, wid), lambda b, h, i, l: (b, i, c0 + h)),
         pl.BlockSpec((None, t, wid), lambda b, h, i, l: (b, 0, c0 + ngrp + h)),
         pl.BlockSpec((None, t, wid), lambda b, h, i, l: (b, 0, c0 + 2 * ngrp + h))],
        pl.BlockSpec((None, blk, wid), lambda b, h, i, l: (b, i, h)),
        jax.ShapeDtypeStruct((b, t, W_MIX), BF16), (sb_bias, proj3, proj3, proj3),
        dims=("parallel", "parallel", "parallel"))


def _sb_decode_kernel(q_ref, bias_ref, *refs, n_steps, pages):
    k_refs, v_refs = refs[:pages], refs[pages:2 * pages]
    o_ref, carry_ref, acc_ref = refs[2 * pages:]
    p = pl.program_id(1)

    @pl.when(p == 0)
    def _():
        carry_ref[...] = jnp.zeros_like(carry_ref)
        acc_ref[...] = jnp.zeros_like(acc_ref)

    q = (q_ref[...] * SB_SCALE).astype(BF16)
    bias = bias_ref[...]
    z = jnp.concatenate([_dot(q, k_ref[...].reshape(W_MIX, PAGE_SIZE)) + bias for k_ref in k_refs], axis=0)
    log_fail = -_softplus(z)
    row = lax.broadcasted_iota(jnp.int32, (PAGE_SIZE, PAGE_SIZE), 0)
    col = lax.broadcasted_iota(jnp.int32, (PAGE_SIZE, PAGE_SIZE), 1)
    upper = (row > col).astype(BF16)
    hi, lo = _split_bf16(log_fail)
    later = (jnp.dot(hi, upper, preferred_element_type=F32)
             + jnp.dot(lo, upper, preferred_element_type=F32))
    total = jnp.sum(log_fail, axis=-1, keepdims=True)
    arg = z + log_fail + later
    carry = carry_ref[...]
    acc = acc_ref[...]
    for i in reversed(range(pages)):
        rows = slice(i * SB_HEADS, (i + 1) * SB_HEADS)
        wts = jnp.exp(arg[rows] + carry)
        acc = acc + _dot_nt(wts, v_refs[i][...].reshape(W_MIX, PAGE_SIZE))
        carry = carry + total[rows]
    carry_ref[...] = carry
    acc_ref[...] = acc

    @pl.when(p == n_steps - 1)
    def _():
        head = lax.broadcasted_iota(jnp.int32, acc.shape, 0)
        col_head = lax.broadcasted_iota(jnp.int32, acc.shape, 1) // SB_HEAD_DIM
        o_ref[...] = jnp.sum(jnp.where(head == col_head, acc, 0.0), axis=0, keepdims=True)


def sb_decode(l, q_blocks, sb_bias_col, cache_kt, cache_vt, page_table):
    b = q_blocks.shape[0]
    n_pages = page_table.shape[1]
    pages = DECODE_PAGES if n_pages % DECODE_PAGES == 0 else 1
    n_steps = n_pages // pages

    def page(i):
        return pl.BlockSpec((None, None, SB_HEADS, SB_HEAD_DIM, PAGE_SIZE),
                            lambda b, p, l, pt: (l[0], pt[b, n_pages - pages * (p + 1) + i], 0, 0, 0))

    return _call(
        functools.partial(_sb_decode_kernel, n_steps=n_steps, pages=pages), l, (b, n_steps),
        [pl.BlockSpec((None, SB_HEADS, W_MIX), lambda b, p, l, pt: (b, 0, 0)),
         pl.BlockSpec((None, SB_HEADS, 1), lambda b, p, l, pt: (l[0], 0, 0))]
        + [page(i) for i in range(pages)] * 2,
        pl.BlockSpec((None, 1, W_MIX), lambda b, p, l, pt: (b, 0, 0)),
        jax.ShapeDtypeStruct((b, 1, W_MIX), F32),
        (q_blocks, sb_bias_col) + (cache_kt,) * pages + (cache_vt,) * pages,
        scratch=[pltpu.VMEM((SB_HEADS, 1), F32), pltpu.VMEM((SB_HEADS, W_MIX), F32)],
        dims=("parallel", "arbitrary"), prefetch=(page_table,))


def _wkv_pre_kernel(pr_ref, pk_ref, pv_ref, pl_ref, s0_ref, mu_ref, w0_ref, dup_ref, a0_ref, aup_ref,
                    gup_ref, r_ref, k_ref, v_ref, ld_ref, al_ref, g_ref, ext_ref, *, tt, t_valid):
    i = pl.program_id(1)

    @pl.when(i == 0)
    def _():
        ext_ref[0:SHIFT_HALO, :] = s0_ref[...]

    @pl.when(i > 0)
    def _():
        ext_ref[0:SHIFT_HALO, :] = ext_ref[tt:tt + SHIFT_HALO, :]

    rows = slice(SHIFT_HALO, SHIFT_HALO + tt)
    ext_ref[rows, 0:W_MIX] = pr_ref[...]
    ext_ref[rows, W_MIX:2 * W_MIX] = pk_ref[...]
    ext_ref[rows, 2 * W_MIX:3 * W_MIX] = pv_ref[...]
    ext_ref[rows, 3 * W_MIX:N_D] = pl_ref[...]
    p = ext_ref[rows, :]
    prev = ext_ref[SHIFT_HALO - 1:SHIFT_HALO - 1 + tt, :]
    xs = p + (prev - p) * mu_ref[...]
    o = 3 * W_MIX
    w_dn = xs[:, o:o + DECAY_LORA]
    a_dn = xs[:, o + DECAY_LORA:o + DECAY_LORA + A_LORA]
    g_dn = xs[:, o + DECAY_LORA + A_LORA:N_D]
    w_log = -_softplus(-(w0_ref[...] + _dot(jnp.tanh(w_dn), dup_ref[...]))) - 0.5
    log_decay = -jnp.exp(w_log)
    alpha = _sigmoid(a0_ref[...] + _dot(a_dn, aup_ref[...]))
    k = xs[:, W_MIX:2 * W_MIX]
    v = xs[:, 2 * W_MIX:3 * W_MIX]
    if t_valid is not None:
        valid = (i * tt + lax.broadcasted_iota(jnp.int32, (tt, 1), 0)) < t_valid
        k = jnp.where(valid, k, 0.0)
        v = jnp.where(valid, v, 0.0)
        log_decay = jnp.where(valid, log_decay, 0.0)
    r_ref[...] = xs[:, 0:W_MIX]
    k_ref[...] = k
    v_ref[...] = v
    ld_ref[...] = log_decay
    al_ref[...] = alpha
    g_ref[...] = _dot(_sigmoid(g_dn), gup_ref[...])


def wkv_pre(l, proj3, shift0, lw, t_valid):
    b, t, _ = proj3.shape
    tt = min(256, t)
    cd = OFF_D // W_MIX
    vec = lambda n: pl.BlockSpec((None, 1, n), lambda b, i, l: (l[0], 0, 0))
    up = lambda n: pl.BlockSpec((None, n, W_MIX), lambda b, i, l: (l[0], 0, 0))
    blk = pl.BlockSpec((None, tt, W_MIX), lambda b, i, l: (b, i, 0))
    return _call(
        functools.partial(_wkv_pre_kernel, tt=tt, t_valid=t_valid), l, (b, t // tt),
        [pl.BlockSpec((None, tt, W_MIX), lambda b, i, l: (b, i, cd)),
         pl.BlockSpec((None, tt, W_MIX), lambda b, i, l: (b, i, cd + 1)),
         pl.BlockSpec((None, tt, W_MIX), lambda b, i, l: (b, i, cd + 2)),
         pl.BlockSpec((None, tt, N_LORA), lambda b, i, l: (b, i, (OFF_D + 3 * W_MIX) // N_LORA)),
         pl.BlockSpec((None, SHIFT_HALO, N_D), lambda b, i, l: (b, 0, 0)),
         vec(N_D), vec(W_MIX), up(DECAY_LORA), vec(W_MIX), up(A_LORA), up(GATE_LORA)],
        [blk] * 6, [jax.ShapeDtypeStruct((b, t, W_MIX), F32)] * 6,
        (proj3, proj3, proj3, proj3, shift0, lw['shift_mu'], lw['decay_w0'], lw['decay_up'],
         lw['a0'], lw['a_up'], lw['g_up']),
        scratch=[pltpu.VMEM((SHIFT_HALO + tt, N_D), F32)], dims=("parallel", "arbitrary"))


def _bdot(a, b):
    return lax.dot_general(a.astype(BF16), b.astype(BF16), (((2,), (1,)), ((0,), (0,))),
                           preferred_element_type=F32)


def _bdot_nt(a, b):
    return lax.dot_general(a.astype(BF16), b.astype(BF16), (((2,), (2,)), ((0,), (0,))),
                           preferred_element_type=F32)


def _bdot_tn(a, b):
    return lax.dot_general(a.astype(BF16), b.astype(BF16), (((1,), (1,)), ((0,), (0,))),
                           preferred_element_type=F32)


def _heads(x):
    return jnp.stack([x[:, h * RWKV_HEAD_DIM:(h + 1) * RWKV_HEAD_DIM] for h in range(RWKV_HEADS)], axis=0)


def _head_sum(x):
    r = lax.broadcasted_iota(jnp.int32, (W_MIX, W_MIX), 0) // RWKV_HEAD_DIM
    c = lax.broadcasted_iota(jnp.int32, (W_MIX, W_MIX), 1) // RWKV_HEAD_DIM
    same = (r == c).astype(BF16)
    hi, lo = _split_bf16(x)
    return jnp.dot(hi, same, preferred_element_type=F32) + jnp.dot(lo, same, preferred_element_type=F32)


def _wkv_scan_kernel(r_ref, k_ref, v_ref, ld_ref, al_ref, g_ref, s0_ref, kk_ref, ka_ref, rk_ref,
                     lg_ref, lb_ref, y_ref, so_ref, s_ref, *, ch, n_chunks):
    c = pl.program_id(1)

    @pl.when(c == 0)
    def _():
        s_ref[...] = s0_ref[...]

    row = lax.broadcasted_iota(jnp.int32, (ch, ch), 0)
    col = lax.broadcasted_iota(jnp.int32, (ch, ch), 1)
    incl = row >= col
    eye = (row == col).astype(F32)
    row2 = lax.broadcasted_iota(jnp.int32, (2 * ch, 2 * ch), 0)
    col2 = lax.broadcasted_iota(jnp.int32, (2 * ch, 2 * ch), 1) & (ch - 1)
    tri = (row2 & (ch - 1)) + (row2 // ch) > col2
    ld = ld_ref[...]
    hi, lo = _split_bf16(ld)
    lower = incl.astype(BF16)
    cum = (jnp.dot(lower, hi, preferred_element_type=F32)
           + jnp.dot(lower, lo, preferred_element_type=F32))
    cum_end = cum[ch - 1:ch, :]
    g_in = jnp.exp(cum)
    g_ex = jnp.exp(cum - ld)
    g_inv = jnp.exp(-cum)
    g_end = jnp.exp(cum_end - cum)
    g_tot = jnp.exp(cum_end)
    r = r_ref[...]
    k = k_ref[...]
    v = v_ref[...]
    al = al_ref[...]
    kk = k * kk_ref[...]
    k2 = k * (1.0 + (al - 1.0) * ka_ref[...])
    rk = r * k2 * rk_ref[...]
    kkn = kk / jnp.maximum(jnp.sqrt(_head_sum(kk * kk)), 1e-12)
    b = kkn * al
    a2 = _heads(jnp.concatenate([-kkn * g_ex, r * g_in], axis=0))
    bm = _heads(jnp.concatenate([b * g_inv, k2 * g_inv], axis=0))
    bk_end = _heads(jnp.concatenate([b * g_end, k2 * g_end], axis=0))
    v3 = _heads(v)
    prod = jnp.where(tri, _bdot_nt(a2, bm), 0.0)
    x = prod[:, 0:ch, 0:ch]
    inv = eye + x
    n_sq = ch.bit_length() - 2
    for _ in range(n_sq):
        x = _bdot(x, x)
        inv = inv + _bdot(inv, x)
    s = s_ref[...]
    w_rhs = _bdot_nt(a2[:, 0:ch], s) + _bdot(prod[:, 0:ch], jnp.concatenate([jnp.zeros_like(v3), v3], axis=1))
    u = _bdot(inv, w_rhs)
    uv = jnp.concatenate([u, v3], axis=1)
    y3 = _bdot_nt(a2[:, ch:2 * ch], s) + _bdot(prod[:, ch:2 * ch], uv)
    s_ref[...] = s * _heads(g_tot) + _bdot_tn(uv, bk_end)
    mu = jnp.mean(y3, axis=-1, keepdims=True)
    yc = y3 - mu
    var = jnp.mean(yc * yc, axis=-1, keepdims=True)
    yn3 = yc * lax.rsqrt(var + GN_EPS)
    yn = jnp.concatenate([yn3[h] for h in range(RWKV_HEADS)], axis=1)
    bonus = _head_sum(rk) * v
    y_ref[...] = ((yn * lg_ref[...] + lb_ref[...] + bonus) * g_ref[...]).astype(y_ref.dtype)

    @pl.when(c == n_chunks - 1)
    def _():
        so_ref[...] = s_ref[...]


def wkv_scan(l, r, k, v, ld, al, g, s0, lw):
    b, t, _ = r.shape
    ch = min(WKV_CHUNK, t)
    blk = pl.BlockSpec((None, ch, W_MIX), lambda b, c, l: (b, c, 0))
    st = pl.BlockSpec((None, RWKV_HEADS, RWKV_HEAD_DIM, RWKV_HEAD_DIM), lambda b, c, l: (b, 0, 0, 0))
    vec = pl.BlockSpec((None, 1, W_MIX), lambda b, c, l: (l[0], 0, 0))
    return _call(
        functools.partial(_wkv_scan_kernel, ch=ch, n_chunks=t // ch), l, (b, t // ch),
        [blk] * 6 + [st] + [vec] * 5, [blk, st],
        [jax.ShapeDtypeStruct((b, t, W_MIX), BF16),
         jax.ShapeDtypeStruct((b, RWKV_HEADS, RWKV_HEAD_DIM, RWKV_HEAD_DIM), F32)],
        (r, k, v, ld, al, g, s0, lw['k_k'], lw['k_a'], lw['r_k'], lw['lnx_g'], lw['lnx_b']),
        scratch=[pltpu.VMEM((RWKV_HEADS, RWKV_HEAD_DIM, RWKV_HEAD_DIM), F32)],
        dims=("parallel", "arbitrary"))


def _split_heads_kernel(k_ref, v_ref, ko_ref, vo_ref):
    ko_ref[...] = k_ref[...].T.reshape(ko_ref.shape)
    vo_ref[...] = v_ref[...].T.reshape(vo_ref.shape)


def split_heads(l, proj3):
    b, t, _ = proj3.shape
    tt = min(512, t)
    col = lambda j: pl.BlockSpec((None, tt, W_MIX), lambda b, i, l: (b, i, OFF_C // W_MIX + j))
    out = pl.BlockSpec((None, SB_HEADS, SB_HEAD_DIM, tt), lambda b, i, l: (b, 0, 0, i))
    shape = jax.ShapeDtypeStruct((b, SB_HEADS, SB_HEAD_DIM, t), F32)
    return _call(_split_heads_kernel, l, (b, t // tt), [col(1), col(2)], [out, out], [shape, shape],
                 (proj3, proj3), dims=("parallel", "parallel"))


def _trunk_layer(l, x, pool0, conv0, shift0, wkv0, pos0, t_valid, attend, lw):
    b, t, d = x.shape
    m = b * t
    x2 = x.reshape(m, d)
    h = rmsnorm(l, x2, lw['g_ffn1'], BF16)
    x2 = matmul(l, ffn_in(l, h, lw['w_ffn1_in']), lw['w_ffn1_out'], tn=256, res=x2, scale=FFN_RESIDUAL)
    h = rmsnorm(l, x2, lw['g_mix'], BF16)
    proj = matmul(l, h, lw['w_in'], tn=768)
    proj3 = proj.reshape(b, t, N_IN)
    y_a = pool_mixer(l, proj3, pool0, lw['w_pool'], lw['pool_scale'], pos0)
    y_b, glu = conv_mixer(l, proj3, conv0, lw['conv_w'], lw['conv_b'], lw['ln_g'], lw['ln_b'])
    y_c = attend(l, proj3)
    r, k, v, ld, al, g = wkv_pre(l, proj3, shift0, lw, t_valid)
    y_d, wkv_new = wkv_scan(l, r, k, v, ld, al, g, wkv0, lw)
    ys = [y.reshape(m, W_MIX) for y in (y_a, y_b, y_c, y_d)]
    merged = merge(l, ys, proj, lw['w_branch'])
    x2 = matmul(l, merged, lw['w_out'], tn=256, res=x2, scale=1.0)
    h = rmsnorm(l, x2, lw['g_ffn2'], BF16)
    x2 = matmul(l, ffn_in(l, h, lw['w_ffn2_in']), lw['w_ffn2_out'], tn=256, res=x2, scale=FFN_RESIDUAL)
    return x2.reshape(b, t, d), proj3, glu, wkv_new


def _front_pad(rows, height):
    return jnp.pad(rows, ((0, 0), (height - rows.shape[1], 0), (0, 0)))


def kernel(x_prompt, x_sample, cache_k, cache_v, state_pool, state_conv, state_shift, state_wkv,
           page_table, g_ffn1, w_ffn1_in, w_ffn1_out, g_mix, w_in, w_pool, pool_scale, conv_w,
           conv_b, ln_g, ln_b, sb_bias, shift_mu, decay_w0, decay_up, a0, a_up, g_up, k_k, k_a, r_k,
           lnx_g, lnx_b, w_branch, w_out, g_ffn2, w_ffn2_in, w_ffn2_out, g_final):
    depth = w_in.shape[0]
    bp, tp, d = x_prompt.shape
    bs, ts, _ = x_sample.shape
    assert d == D_MODEL and ts == 1
    past_len = page_table.shape[1] * PAGE_SIZE
    row = lambda a: a.reshape(depth, 1, -1)
    lw = {
        'g_ffn1': row(g_ffn1), 'w_ffn1_in': w_ffn1_in, 'w_ffn1_out': w_ffn1_out, 'g_mix': row(g_mix),
        'w_in': w_in, 'w_pool': w_pool, 'pool_scale': row(pool_scale), 'conv_w': conv_w,
        'conv_b': row(conv_b), 'ln_g': row(ln_g), 'ln_b': row(ln_b), 'sb_bias': row(sb_bias),
        'shift_mu': row(shift_mu), 'decay_w0': row(decay_w0), 'decay_up': decay_up, 'a0': row(a0),
        'a_up': a_up, 'g_up': g_up, 'k_k': row(k_k), 'k_a': row(k_a), 'r_k': row(r_k),
        'lnx_g': row(lnx_g), 'lnx_b': row(lnx_b), 'w_branch': w_branch, 'w_out': w_out,
        'g_ffn2': row(g_ffn2), 'w_ffn2_in': w_ffn2_in, 'w_ffn2_out': w_ffn2_out,
    }
    sb_bias_col = sb_bias.reshape(depth, SB_HEADS, 1)
    cache_kt = jnp.transpose(cache_k, (0, 1, 3, 4, 2))
    cache_vt = jnp.transpose(cache_v, (0, 1, 3, 4, 2))
    own_head = (jnp.arange(W_MIX)[None, :] // SB_HEAD_DIM) == jnp.arange(SB_HEADS)[:, None]
    xs0 = jnp.pad(x_sample, ((0, 0), (0, SAMPLE_ROWS - ts), (0, 0)))

    def layer(carry, li):
        xp, xs = carry
        l = li.reshape(1)
        xp, proj_p, glu_p, wkv_p = _trunk_layer(
            l, xp, jnp.zeros((bp, POOL_HALO, W_MIX), F32), jnp.zeros((bp, CONV_HALO, W_MIX), F32),
            jnp.zeros((bp, SHIFT_HALO, N_D), F32),
            jnp.zeros((bp, RWKV_HEADS, RWKV_HEAD_DIM, RWKV_HEAD_DIM), F32), 0, None,
            lambda l, p3: sb_prompt(l, p3, lw['sb_bias']), lw)
        pool_s = lax.dynamic_index_in_dim(state_pool, li, 0, keepdims=False)
        conv_s = lax.dynamic_index_in_dim(state_conv, li, 0, keepdims=False)
        shift_s = lax.dynamic_index_in_dim(state_shift, li, 0, keepdims=False)
        wkv_s = lax.dynamic_index_in_dim(state_wkv, li, 0, keepdims=False)

        def attend_sample(l, p3):
            q_blocks = jnp.where(own_head, p3[:, 0:1, OFF_C:OFF_C + W_MIX], 0.0)
            y = sb_decode(l, q_blocks, sb_bias_col, cache_kt, cache_vt, page_table)
            return jnp.pad(y, ((0, 0), (0, SAMPLE_ROWS - 1), (0, 0))).astype(BF16)

        xs, proj_s, glu_s, wkv_s_new = _trunk_layer(
            l, xs, _front_pad(pool_s, POOL_HALO), _front_pad(conv_s, CONV_HALO),
            _front_pad(shift_s, SHIFT_HALO), wkv_s, past_len, ts, attend_sample, lw)

        k_p, v_p = split_heads(l, proj_p)
        k_s, v_s = (proj_s[:, :ts, OFF_C + j * W_MIX:OFF_C + (j + 1) * W_MIX].reshape(
            bs, ts, SB_HEADS, SB_HEAD_DIM) for j in (1, 2))
        outs = (
            k_p, v_p, proj_p[:, tp - POOL_BUF:, OFF_A:OFF_A + W_MIX], glu_p[:, tp - CONV_BUF:],
            proj_p[:, tp - 1:, OFF_D:OFF_D + N_D], wkv_p,
            k_s, v_s,
            jnp.concatenate([pool_s, proj_s[:, :ts, OFF_A:OFF_A + W_MIX]], axis=1)[:, -POOL_BUF:],
            jnp.concatenate([conv_s, glu_s[:, :ts]], axis=1)[:, -CONV_BUF:],
            proj_s[:, ts - 1:ts, OFF_D:OFF_D + N_D], wkv_s_new)
        return (xp, xs), outs

    (xp, xs), st = lax.scan(layer, (x_prompt, xs0), jnp.arange(depth, dtype=jnp.int32))
    zero = jnp.zeros((1,), jnp.int32)
    g_fin = g_final.reshape(1, 1, d)
    y_prompt = rmsnorm(zero, xp.reshape(bp * tp, d), g_fin, F32).reshape(bp, tp, d)
    y_sample = rmsnorm(zero, xs.reshape(bs * SAMPLE_ROWS, d), g_fin, F32).reshape(bs, SAMPLE_ROWS, d)[:, :ts]
    to_cache_layout = lambda a: jnp.transpose(a, (0, 1, 4, 2, 3))
    return (y_prompt, y_sample, to_cache_layout(st[0]), to_cache_layout(st[1])) + tuple(st[2:])
```

```python
import functools

import jax
import jax.numpy as jnp
from jax import lax
from jax.experimental import pallas as pl
from jax.experimental.pallas import tpu as pltpu

F32 = jnp.float32
BF16 = jnp.bfloat16

LANES = 128
SUBLANES = 8
VMEM_LIMIT_BYTES = 56 * 1024 * 1024

D_MODEL = 2048
W_MIX = D_MODEL // 4
POOL_WINDOWS = (2, 4, 8, 16)
POOL_GROUP = W_MIX // len(POOL_WINDOWS)
POOL_BUF = max(POOL_WINDOWS) - 1
CONV_WIDTH = 31
CONV_BUF = CONV_WIDTH - 1
SB_HEADS = 8
SB_HEAD_DIM = W_MIX // SB_HEADS
SB_SCALE = SB_HEAD_DIM ** -0.5
RWKV_HEAD_DIM = 64
RWKV_HEADS = W_MIX // RWKV_HEAD_DIM
DECAY_LORA = 64
A_LORA = 64
GATE_LORA = 128
N_LORA = DECAY_LORA + A_LORA + GATE_LORA
N_D = 3 * W_MIX + N_LORA
N_BRANCH = 4
OFF_A = 0
OFF_B = OFF_A + W_MIX
OFF_C = OFF_B + 2 * W_MIX
OFF_D = OFF_C + 3 * W_MIX
OFF_G = OFF_D + N_D
N_IN = OFF_G + N_BRANCH * D_MODEL
D_FF = ((8 * D_MODEL // 3 + 127) // 128) * 128
FFN_RESIDUAL = 0.5
RMS_EPS = 1e-6
LN_EPS = 1e-5
GN_EPS = 64e-5
PAGE_SIZE = 128

POOL_HALO = 16
CONV_HALO = 32
SHIFT_HALO = 8
SAMPLE_ROWS = 8
ATTN_BLOCK = 256
SB_GROUP = 4
DECODE_PAGES = 16
WKV_CHUNK = 64
WKV_CHUNKS_PER_STEP = 2


def _call(kernel, l, grid, in_specs, out_specs, out_shape, args, scratch=(), dims=None, prefetch=()):
    n_pre = 1 + len(prefetch)

    def body(*refs):
        kernel(*refs[n_pre:])

    grid_spec = pltpu.PrefetchScalarGridSpec(
        num_scalar_prefetch=n_pre, grid=grid, in_specs=in_specs, out_specs=out_specs,
        scratch_shapes=list(scratch))
    fn = kernel.func if isinstance(kernel, functools.partial) else kernel
    return pl.pallas_call(
        body, grid_spec=grid_spec, out_shape=out_shape, name=fn.__name__.strip("_"),
        compiler_params=pltpu.CompilerParams(
            dimension_semantics=dims or ("arbitrary",) * len(grid),
            vmem_limit_bytes=VMEM_LIMIT_BYTES),
    )(l, *prefetch, *args)


def _sigmoid(x):
    return 1.0 / (1.0 + jnp.exp(-x))


def _softplus(x):
    return jnp.maximum(x, 0.0) + jnp.log1p(jnp.exp(-jnp.abs(x)))


def _split_bf16(x):
    hi = x.astype(BF16)
    lo = (x - hi.astype(F32)).astype(BF16)
    return hi, lo


def _dot(a, b):
    return jnp.dot(a.astype(BF16), b.astype(BF16), preferred_element_type=F32)


def _dot_nt(a, b):
    return lax.dot_general(a.astype(BF16), b.astype(BF16), (((1,), (1,)), ((), ())),
                           preferred_element_type=F32)


def _dot_tn(a, b):
    return lax.dot_general(a.astype(BF16), b.astype(BF16), (((0,), (0,)), ((), ())),
                           preferred_element_type=F32)


def _rmsnorm_kernel(x_ref, g_ref, o_ref):
    x = x_ref[...]
    y = x * lax.rsqrt(jnp.mean(x * x, axis=-1, keepdims=True) + RMS_EPS)
    o_ref[...] = (y * g_ref[...]).astype(o_ref.dtype)


def rmsnorm(l, x, g, out_dtype):
    m, d = x.shape
    tm = min(512, m)
    return _call(
        _rmsnorm_kernel, l, (m // tm,),
        [pl.BlockSpec((tm, d), lambda i, l: (i, 0)),
         pl.BlockSpec((None, 1, d), lambda i, l: (l[0], 0, 0))],
        pl.BlockSpec((tm, d), lambda i, l: (i, 0)),
        jax.ShapeDtypeStruct((m, d), out_dtype), (x, g), dims=("parallel",))


def _ffn_in_kernel(x_ref, wg_ref, wu_ref, o_ref):
    w = jnp.concatenate([wg_ref[...], wu_ref[...]], axis=1).astype(BF16)
    r = jnp.dot(x_ref[...], w, preferred_element_type=F32)
    g = r[:, :LANES]
    o_ref[...] = (g * _sigmoid(g) * r[:, LANES:]).astype(o_ref.dtype)


def ffn_in(l, x, w):
    m, d = x.shape
    d_ff = w.shape[2] // 2
    nb = d_ff // LANES
    tm = min(2048, m)
    return _call(
        _ffn_in_kernel, l, (m // tm, nb),
        [pl.BlockSpec((tm, d), lambda i, j, l: (i, 0)),
         pl.BlockSpec((None, d, LANES), lambda i, j, l: (l[0], 0, j)),
         pl.BlockSpec((None, d, LANES), lambda i, j, l: (l[0], 0, j + nb))],
        pl.BlockSpec((tm, LANES), lambda i, j, l: (i, j)),
        jax.ShapeDtypeStruct((m, d_ff), BF16), (x, w, w), dims=("parallel", "parallel"))


def _mm_kernel(a_ref, w_ref, o_ref):
    o_ref[...] = jnp.dot(a_ref[...], w_ref[...].astype(BF16),
                         preferred_element_type=F32).astype(o_ref.dtype)


def _mm_res_kernel(a_ref, w_ref, r_ref, o_ref, *, scale):
    acc = jnp.dot(a_ref[...], w_ref[...].astype(BF16), preferred_element_type=F32)
    o_ref[...] = r_ref[...] + scale * acc


def matmul(l, a, w, *, tn, res=None, scale=1.0, n=None, tm=1024):
    m, k = a.shape
    n = w.shape[2] if n is None else n
    tm = min(tm, m)
    a_spec = pl.BlockSpec((tm, k), lambda i, j, l: (i, 0))
    w_spec = pl.BlockSpec((None, k, tn), lambda i, j, l: (l[0], 0, j))
    o_spec = pl.BlockSpec((tm, tn), lambda i, j, l: (i, j))
    out_shape = jax.ShapeDtypeStruct((m, n), F32)
    dims = ("parallel", "parallel")
    if res is None:
        return _call(_mm_kernel, l, (m // tm, n // tn), [a_spec, w_spec], o_spec, out_shape,
                     (a, w), dims=dims)
    return _call(functools.partial(_mm_res_kernel, scale=scale), l, (m // tm, n // tn),
                 [a_spec, w_spec, o_spec], o_spec, out_shape, (a, w, res), dims=dims)


def _merge_kernel(h_ref, ya_ref, yb_ref, yc_ref, yd_ref, g0_ref, g1_ref, g2_ref, g3_ref, wb_ref, o_ref):
    h = h_ref[...]
    acc = None
    for n, (y_ref, wg_ref) in enumerate(((ya_ref, g0_ref), (yb_ref, g1_ref), (yc_ref, g2_ref),
                                         (yd_ref, g3_ref))):
        gate = jnp.dot(h, wg_ref[...].astype(BF16), preferred_element_type=F32)
        t = _sigmoid(gate) * jnp.dot(y_ref[...], wb_ref[n].astype(BF16), preferred_element_type=F32)
        acc = t if acc is None else acc + t
    o_ref[...] = acc.astype(o_ref.dtype)


def merge(l, h, ys, w_in, w_branch):
    m = h.shape[0]
    tn = 256
    tm = min(1024, m)
    nj = D_MODEL // tn
    y_spec = pl.BlockSpec((tm, W_MIX), lambda i, j, l: (i, 0))
    g_specs = [pl.BlockSpec((None, D_MODEL, tn), functools.partial(
        lambda i, j, l, n: (l[0], 0, OFF_G // tn + n * nj + j), n=n)) for n in range(N_BRANCH)]
    return _call(
        _merge_kernel, l, (m // tm, nj),
        [pl.BlockSpec((tm, D_MODEL), lambda i, j, l: (i, 0))] + [y_spec] * N_BRANCH + g_specs
        + [pl.BlockSpec((None, N_BRANCH, W_MIX, tn), lambda i, j, l: (l[0], 0, 0, j))],
        pl.BlockSpec((tm, tn), lambda i, j, l: (i, j)),
        jax.ShapeDtypeStruct((m, D_MODEL), BF16), (h, *ys, w_in, w_in, w_in, w_in, w_branch),
        dims=("parallel", "parallel"))


def _pool_kernel(u_ref, t0_ref, wp_ref, ps_ref, y_ref, ext_ref, *, tt, pos0):
    i = pl.program_id(1)

    @pl.when(i == 0)
    def _():
        ext_ref[0:POOL_HALO, :] = t0_ref[...]

    @pl.when(i > 0)
    def _():
        ext_ref[0:POOL_HALO, :] = ext_ref[tt:tt + POOL_HALO, :]

    u = u_ref[...]
    ext_ref[POOL_HALO:POOL_HALO + tt, :] = u
    pos = pos0 + i * tt + lax.broadcasted_iota(jnp.int32, (tt, 1), 0)
    ys = []
    for gi, win in enumerate(POOL_WINDOWS):
        cs = slice(gi * POOL_GROUP, (gi + 1) * POOL_GROUP)
        wsum = u[:, cs]
        for k in range(1, win):
            wsum = wsum + ext_ref[POOL_HALO - k:POOL_HALO - k + tt, cs]
        count = jnp.minimum(win, pos + 1).astype(F32)
        d = wsum / count - u[:, cs]
        ys.append(_dot(d, wp_ref[gi]))
    y_ref[...] = (jnp.concatenate(ys, axis=1) * ps_ref[...]).astype(y_ref.dtype)


def pool_mixer(l, proj3, tail0, w_pool, pool_scale, pos0):
    b, t, _ = proj3.shape
    tt = min(512, t)
    return _call(
        functools.partial(_pool_kernel, tt=tt, pos0=pos0), l, (b, t // tt),
        [pl.BlockSpec((None, tt, W_MIX), lambda b, i, l: (b, i, OFF_A // W_MIX)),
         pl.BlockSpec((None, POOL_HALO, W_MIX), lambda b, i, l: (b, 0, 0)),
         pl.BlockSpec((None, len(POOL_WINDOWS), POOL_GROUP, POOL_GROUP), lambda b, i, l: (l[0], 0, 0, 0)),
         pl.BlockSpec((None, 1, W_MIX), lambda b, i, l: (l[0], 0, 0))],
        pl.BlockSpec((None, tt, W_MIX), lambda b, i, l: (b, i, 0)),
        jax.ShapeDtypeStruct((b, t, W_MIX), BF16), (proj3, tail0, w_pool, pool_scale),
        scratch=[pltpu.VMEM((POOL_HALO + tt, W_MIX), F32)], dims=("parallel", "arbitrary"))


def _conv_kernel(val_ref, gate_ref, t0_ref, w_ref, cb_ref, lg_ref, lb_ref, y_ref, u_ref, ext_ref, *, tt):
    i = pl.program_id(1)

    @pl.when(i == 0)
    def _():
        ext_ref[0:CONV_HALO, :] = t0_ref[...]

    @pl.when(i > 0)
    def _():
        ext_ref[0:CONV_HALO, :] = ext_ref[tt:tt + CONV_HALO, :]

    u = val_ref[...] * _sigmoid(gate_ref[...])
    u_ref[...] = u
    ext_ref[CONV_HALO:CONV_HALO + tt, :] = u
    first = CONV_HALO - CONV_BUF
    acc = ext_ref[first:first + tt, :] * w_ref[0:1, :]
    for j in range(1, CONV_WIDTH):
        acc = acc + ext_ref[first + j:first + j + tt, :] * w_ref[j:j + 1, :]
    y = acc + cb_ref[...]
    mu = jnp.mean(y, axis=-1, keepdims=True)
    yc = y - mu
    var = jnp.mean(yc * yc, axis=-1, keepdims=True)
    yn = yc * lax.rsqrt(var + LN_EPS) * lg_ref[...] + lb_ref[...]
    y_ref[...] = (yn * _sigmoid(yn)).astype(y_ref.dtype)


def conv_mixer(l, proj3, tail0, conv_w, conv_b, ln_g, ln_b):
    b, t, _ = proj3.shape
    tt = min(256, t)
    vec = pl.BlockSpec((None, 1, W_MIX), lambda b, i, l: (l[0], 0, 0))
    blk = pl.BlockSpec((None, tt, W_MIX), lambda b, i, l: (b, i, 0))
    return _call(
        functools.partial(_conv_kernel, tt=tt), l, (b, t // tt),
        [pl.BlockSpec((None, tt, W_MIX), lambda b, i, l: (b, i, OFF_B // W_MIX)),
         pl.BlockSpec((None, tt, W_MIX), lambda b, i, l: (b, i, OFF_B // W_MIX + 1)),
         pl.BlockSpec((None, CONV_HALO, W_MIX), lambda b, i, l: (b, 0, 0)),
         pl.BlockSpec((None, CONV_WIDTH, W_MIX), lambda b, i, l: (l[0], 0, 0)),
         vec, vec, vec],
        [blk, blk],
        [jax.ShapeDtypeStruct((b, t, W_MIX), BF16), jax.ShapeDtypeStruct((b, t, W_MIX), F32)],
        (proj3, proj3, tail0, conv_w, conv_b, ln_g, ln_b),
        scratch=[pltpu.VMEM((CONV_HALO + tt, W_MIX), F32)], dims=("parallel", "arbitrary"))


def _sb_heads(x):
    n = x.shape[1] // SB_HEAD_DIM
    return jnp.stack([x[:, h * SB_HEAD_DIM:(h + 1) * SB_HEAD_DIM] for h in range(n)], axis=0)


def _sb_block(q3, k3, v3, bias, carry, acc, neg_upper, mask):
    nh, tq, _ = q3.shape
    tk = k3.shape[1]
    z = _bdot_nt(q3, k3)
    z = jnp.stack([z[h] + bias[h] for h in range(nh)], axis=0)
    fail = jnp.maximum(z, 0.0) + jnp.log(1.0 + jnp.exp(-jnp.abs(z)))
    if mask is not None:
        fail = jnp.where(mask, fail, 0.0)
    later = jnp.dot(fail.astype(BF16).reshape(nh * tq, tk), neg_upper,
                    preferred_element_type=F32).reshape(nh, tq, tk)
    wts = jnp.exp(z - fail + later + carry)
    if mask is not None:
        wts = jnp.where(mask, wts, 0.0)
    acc = acc + _bdot(wts, v3)
    carry = carry + later[:, :, 0:1] - fail[:, :, 0:1]
    return carry, acc


def _sb_prompt_kernel(bias_ref, q_ref, k_ref, v_ref, o_ref, kb_ref, vb_ref, *, blk, n_blk):
    qi = pl.program_id(2)
    grp = pl.program_id(1)

    @pl.when(qi == 0)
    def _():
        def fill(j, _):
            rows = pl.ds(pl.multiple_of(j * blk, blk), blk)
            kb_ref[:, rows, :] = _sb_heads(k_ref[rows, :].astype(BF16))
            vb_ref[:, rows, :] = _sb_heads(v_ref[rows, :].astype(BF16))
            return 0
        lax.fori_loop(0, n_blk, fill, 0)

    row = lax.broadcasted_iota(jnp.int32, (blk, blk), 0)
    col = lax.broadcasted_iota(jnp.int32, (blk, blk), 1)
    causal = row > col
    neg_upper = jnp.where(causal, -1.0, 0.0).astype(BF16)
    q3 = _sb_heads((q_ref[...] * SB_SCALE).astype(BF16))
    nh = q3.shape[0]
    bias = [bias_ref[0, grp * nh + h] for h in range(nh)]

    def kv(j):
        rows = pl.ds(pl.multiple_of(j * blk, blk), blk)
        return kb_ref[:, rows, :], vb_ref[:, rows, :]

    carry = jnp.zeros((nh, blk, 1), F32)
    acc = jnp.zeros((nh, blk, SB_HEAD_DIM), F32)
    carry, acc = _sb_block(q3, *kv(qi), bias, carry, acc, neg_upper, causal)

    def body(it, state):
        return _sb_block(q3, *kv(qi - 1 - it), bias, state[0], state[1], neg_upper, None)

    carry, acc = lax.fori_loop(0, qi, body, (carry, acc))
    o_ref[...] = jnp.concatenate([acc[h] for h in range(nh)], axis=1).astype(o_ref.dtype)


def sb_prompt(l, proj3, sb_bias):
    b, t, _ = proj3.shape
    blk = min(ATTN_BLOCK, t)
    wid = SB_GROUP * SB_HEAD_DIM
    ngrp = W_MIX // wid
    c0 = OFF_C // wid
    return _call(
        functools.partial(_sb_prompt_kernel, blk=blk, n_blk=t // blk), l, (b, ngrp, t // blk),
        [pl.BlockSpec((None, 1, SB_HEADS), lambda b, h, i, l: (l[0], 0, 0), memory_space=pltpu.SMEM),
         pl.BlockSpec((None, blk, wid), lambda b, h, i, l: (b, i, c0 + h)),
         pl.BlockSpec((None, t, wid), lambda b, h, i, l: (b, 0, c0 + ngrp + h)),
         pl.BlockSpec((None, t, wid), lambda b, h, i, l: (b, 0, c0 + 2 * ngrp + h))],
        pl.BlockSpec((None, blk, wid), lambda b, h, i, l: (b, i, h)),
        jax.ShapeDtypeStruct((b, t, W_MIX), BF16), (sb_bias, proj3, proj3, proj3),
        scratch=[pltpu.VMEM((SB_GROUP, t, SB_HEAD_DIM), BF16)] * 2,
        dims=("parallel", "parallel", "arbitrary"))


def _sb_decode_kernel(q_ref, bias_ref, *refs, n_steps, pages):
    k_refs, v_refs = refs[:pages], refs[pages:2 * pages]
    o_ref, carry_ref, acc_ref = refs[2 * pages:]
    p = pl.program_id(1)

    @pl.when(p == 0)
    def _():
        carry_ref[...] = jnp.zeros_like(carry_ref)
        acc_ref[...] = jnp.zeros_like(acc_ref)

    q = (q_ref[...] * SB_SCALE).astype(BF16)
    bias = bias_ref[...]
    z = jnp.concatenate([_dot(q, k_ref[...].reshape(W_MIX, PAGE_SIZE)) + bias for k_ref in k_refs], axis=0)
    log_fail = -_softplus(z)
    row = lax.broadcasted_iota(jnp.int32, (PAGE_SIZE, PAGE_SIZE), 0)
    col = lax.broadcasted_iota(jnp.int32, (PAGE_SIZE, PAGE_SIZE), 1)
    upper = (row > col).astype(BF16)
    later = _dot(log_fail, upper)
    total = later[:, 0:1] + log_fail[:, 0:1]
    arg = z + log_fail + later
    carry = carry_ref[...]
    acc = acc_ref[...]
    for i in reversed(range(pages)):
        rows = slice(i * SB_HEADS, (i + 1) * SB_HEADS)
        wts = jnp.exp(arg[rows] + carry)
        acc = acc + _dot_nt(wts, v_refs[i][...].reshape(W_MIX, PAGE_SIZE))
        carry = carry + total[rows]
    carry_ref[...] = carry
    acc_ref[...] = acc

    @pl.when(p == n_steps - 1)
    def _():
        head = lax.broadcasted_iota(jnp.int32, acc.shape, 0)
        col_head = lax.broadcasted_iota(jnp.int32, acc.shape, 1) // SB_HEAD_DIM
        o_ref[...] = jnp.sum(jnp.where(head == col_head, acc, 0.0), axis=0, keepdims=True)


def sb_decode(l, q_blocks, sb_bias_col, cache_kt, cache_vt, page_table):
    b = q_blocks.shape[0]
    n_pages = page_table.shape[1]
    pages = DECODE_PAGES if n_pages % DECODE_PAGES == 0 else 1
    n_steps = n_pages // pages

    def page(i):
        return pl.BlockSpec((None, None, SB_HEADS, SB_HEAD_DIM, PAGE_SIZE),
                            lambda b, p, l, pt: (l[0], pt[b, n_pages - pages * (p + 1) + i], 0, 0, 0))

    return _call(
        functools.partial(_sb_decode_kernel, n_steps=n_steps, pages=pages), l, (b, n_steps),
        [pl.BlockSpec((None, SB_HEADS, W_MIX), lambda b, p, l, pt: (b, 0, 0)),
         pl.BlockSpec((None, SB_HEADS, 1), lambda b, p, l, pt: (l[0], 0, 0))]
        + [page(i) for i in range(pages)] * 2,
        pl.BlockSpec((None, 1, W_MIX), lambda b, p, l, pt: (b, 0, 0)),
        jax.ShapeDtypeStruct((b, 1, W_MIX), F32),
        (q_blocks, sb_bias_col) + (cache_kt,) * pages + (cache_vt,) * pages,
        scratch=[pltpu.VMEM((SB_HEADS, 1), F32), pltpu.VMEM((SB_HEADS, W_MIX), F32)],
        dims=("parallel", "arbitrary"), prefetch=(page_table,))


def _wkv_pre_kernel(pr_ref, pk_ref, pv_ref, pl_ref, s0_ref, mu_ref, w0_ref, dup_ref, a0_ref, aup_ref,
                    gup_ref, r_ref, k_ref, v_ref, ld_ref, al_ref, g_ref, ext_ref, *, tt, t_valid):
    i = pl.program_id(1)

    @pl.when(i == 0)
    def _():
        ext_ref[0:SHIFT_HALO, :] = s0_ref[...]

    @pl.when(i > 0)
    def _():
        ext_ref[0:SHIFT_HALO, :] = ext_ref[tt:tt + SHIFT_HALO, :]

    rows = slice(SHIFT_HALO, SHIFT_HALO + tt)
    ext_ref[rows, 0:W_MIX] = pr_ref[...]
    ext_ref[rows, W_MIX:2 * W_MIX] = pk_ref[...]
    ext_ref[rows, 2 * W_MIX:3 * W_MIX] = pv_ref[...]
    ext_ref[rows, 3 * W_MIX:N_D] = pl_ref[...]
    p = ext_ref[rows, :]
    prev = ext_ref[SHIFT_HALO - 1:SHIFT_HALO - 1 + tt, :]
    xs = p + (prev - p) * mu_ref[...]
    o = 3 * W_MIX
    w_dn = xs[:, o:o + DECAY_LORA]
    a_dn = xs[:, o + DECAY_LORA:o + DECAY_LORA + A_LORA]
    g_dn = xs[:, o + DECAY_LORA + A_LORA:N_D]
    w_log = -_softplus(-(w0_ref[...] + _dot(jnp.tanh(w_dn), dup_ref[...]))) - 0.5
    log_decay = -jnp.exp(w_log)
    alpha = _sigmoid(a0_ref[...] + _dot(a_dn, aup_ref[...]))
    k = xs[:, W_MIX:2 * W_MIX]
    v = xs[:, 2 * W_MIX:3 * W_MIX]
    if t_valid is not None:
        valid = (i * tt + lax.broadcasted_iota(jnp.int32, (tt, 1), 0)) < t_valid
        k = jnp.where(valid, k, 0.0)
        v = jnp.where(valid, v, 0.0)
        log_decay = jnp.where(valid, log_decay, 0.0)
    r_ref[...] = xs[:, 0:W_MIX]
    k_ref[...] = k
    v_ref[...] = v
    ld_ref[...] = log_decay
    al_ref[...] = alpha
    g_ref[...] = _dot(_sigmoid(g_dn), gup_ref[...])


def wkv_pre(l, proj3, shift0, lw, t_valid):
    b, t, _ = proj3.shape
    tt = min(256, t)
    cd = OFF_D // W_MIX
    vec = lambda n: pl.BlockSpec((None, 1, n), lambda b, i, l: (l[0], 0, 0))
    up = lambda n: pl.BlockSpec((None, n, W_MIX), lambda b, i, l: (l[0], 0, 0))
    blk = pl.BlockSpec((None, tt, W_MIX), lambda b, i, l: (b, i, 0))
    return _call(
        functools.partial(_wkv_pre_kernel, tt=tt, t_valid=t_valid), l, (b, t // tt),
        [pl.BlockSpec((None, tt, W_MIX), lambda b, i, l: (b, i, cd)),
         pl.BlockSpec((None, tt, W_MIX), lambda b, i, l: (b, i, cd + 1)),
         pl.BlockSpec((None, tt, W_MIX), lambda b, i, l: (b, i, cd + 2)),
         pl.BlockSpec((None, tt, N_LORA), lambda b, i, l: (b, i, (OFF_D + 3 * W_MIX) // N_LORA)),
         pl.BlockSpec((None, SHIFT_HALO, N_D), lambda b, i, l: (b, 0, 0)),
         vec(N_D), vec(W_MIX), up(DECAY_LORA), vec(W_MIX), up(A_LORA), up(GATE_LORA)],
        [blk] * 6, [jax.ShapeDtypeStruct((b, t, W_MIX), F32)] * 6,
        (proj3, proj3, proj3, proj3, shift0, lw['shift_mu'], lw['decay_w0'], lw['decay_up'],
         lw['a0'], lw['a_up'], lw['g_up']),
        scratch=[pltpu.VMEM((SHIFT_HALO + tt, N_D), F32)], dims=("parallel", "arbitrary"))


def _bdot(a, b):
    return lax.dot_general(a.astype(BF16), b.astype(BF16), (((2,), (1,)), ((0,), (0,))),
                           preferred_element_type=F32)


def _bdot_nt(a, b):
    return lax.dot_general(a.astype(BF16), b.astype(BF16), (((2,), (2,)), ((0,), (0,))),
                           preferred_element_type=F32)


def _bdot_tn(a, b):
    return lax.dot_general(a.astype(BF16), b.astype(BF16), (((1,), (1,)), ((0,), (0,))),
                           preferred_element_type=F32)


def _heads(x):
    return jnp.stack([x[:, h * RWKV_HEAD_DIM:(h + 1) * RWKV_HEAD_DIM] for h in range(RWKV_HEADS)], axis=0)


def _head_sums(xs):
    r = lax.broadcasted_iota(jnp.int32, (W_MIX, W_MIX), 0) // RWKV_HEAD_DIM
    c = lax.broadcasted_iota(jnp.int32, (W_MIX, W_MIX), 1) // RWKV_HEAD_DIM
    same = (r == c).astype(BF16)
    rows = xs[0].shape[0]
    parts = [p for x in xs for p in _split_bf16(x)]
    out = jnp.dot(jnp.concatenate(parts, axis=0), same, preferred_element_type=F32)
    return [out[(2 * i) * rows:(2 * i + 1) * rows] + out[(2 * i + 1) * rows:(2 * i + 2) * rows]
            for i in range(len(xs))]


def _wkv_chunk(r, k, v, ld, al, g, s, kk_w, ka_w, rk_w, ln_g, ln_b):
    ch = r.shape[0]
    row = lax.broadcasted_iota(jnp.int32, (ch, ch), 0)
    col = lax.broadcasted_iota(jnp.int32, (ch, ch), 1)
    incl = row >= col
    eye = (row == col).astype(F32)
    row2 = lax.broadcasted_iota(jnp.int32, (2 * ch, 2 * ch), 0)
    col2 = lax.broadcasted_iota(jnp.int32, (2 * ch, 2 * ch), 1) & (ch - 1)
    tri = (row2 & (ch - 1)) + (row2 // ch) > col2
    hi, lo = _split_bf16(ld)
    lower = incl.astype(BF16)
    cum = (jnp.dot(lower, hi, preferred_element_type=F32)
           + jnp.dot(lower, lo, preferred_element_type=F32))
    cum_end = cum[ch - 1:ch, :]
    g_in = jnp.exp(cum)
    g_ex = jnp.exp(cum - ld)
    g_inv = jnp.exp(-cum)
    g_end = jnp.exp(cum_end - cum)
    g_tot = jnp.exp(cum_end)
    kk = k * kk_w
    k2 = k * (1.0 + (al - 1.0) * ka_w)
    kk_sq, rk_sum = _head_sums([kk * kk, r * k2 * rk_w])
    kkn = kk / jnp.maximum(jnp.sqrt(kk_sq), 1e-12)
    b = kkn * al
    a2 = _heads(jnp.concatenate([-kkn * g_ex, r * g_in], axis=0))
    bm = _heads(jnp.concatenate([b * g_inv, k2 * g_inv], axis=0))
    bk_end = _heads(jnp.concatenate([b * g_end, k2 * g_end], axis=0))
    v3 = _heads(v)
    prod = jnp.where(tri, _bdot_nt(a2, bm), 0.0)
    x = prod[:, 0:ch, 0:ch]
    inv = eye + x
    n_sq = ch.bit_length() - 2
    for _ in range(n_sq):
        x = _bdot(x, x)
        inv = inv + _bdot(inv, x)
    w_rhs = _bdot_nt(a2[:, 0:ch], s) + _bdot(prod[:, 0:ch], jnp.concatenate([jnp.zeros_like(v3), v3], axis=1))
    u = _bdot(inv, w_rhs)
    uv = jnp.concatenate([u, v3], axis=1)
    y3 = _bdot_nt(a2[:, ch:2 * ch], s) + _bdot(prod[:, ch:2 * ch], uv)
    s_new = s * _heads(g_tot) + _bdot_tn(uv, bk_end)
    mu = jnp.mean(y3, axis=-1, keepdims=True)
    yc = y3 - mu
    var = jnp.mean(yc * yc, axis=-1, keepdims=True)
    yn3 = yc * lax.rsqrt(var + GN_EPS)
    yn = jnp.concatenate([yn3[h] for h in range(RWKV_HEADS)], axis=1)
    return (yn * ln_g + ln_b + rk_sum * v) * g, s_new


def _wkv_scan_kernel(r_ref, k_ref, v_ref, ld_ref, al_ref, g_ref, s0_ref, kk_ref, ka_ref, rk_ref,
                     lg_ref, lb_ref, y_ref, so_ref, s_ref, *, ch, n_steps):
    c = pl.program_id(1)

    @pl.when(c == 0)
    def _():
        s_ref[...] = s0_ref[...]

    s = s_ref[...]
    for j in range(r_ref.shape[0] // ch):
        rows = slice(j * ch, (j + 1) * ch)
        y, s = _wkv_chunk(r_ref[rows, :], k_ref[rows, :], v_ref[rows, :], ld_ref[rows, :], al_ref[rows, :],
                          g_ref[rows, :], s, kk_ref[...], ka_ref[...], rk_ref[...], lg_ref[...], lb_ref[...])
        y_ref[rows, :] = y.astype(y_ref.dtype)
    s_ref[...] = s

    @pl.when(c == n_steps - 1)
    def _():
        so_ref[...] = s


def wkv_scan(l, r, k, v, ld, al, g, s0, lw):
    b, t, _ = r.shape
    ch = min(WKV_CHUNK, t)
    rows = ch * WKV_CHUNKS_PER_STEP if t % (ch * WKV_CHUNKS_PER_STEP) == 0 else ch
    blk = pl.BlockSpec((None, rows, W_MIX), lambda b, c, l: (b, c, 0))
    st = pl.BlockSpec((None, RWKV_HEADS, RWKV_HEAD_DIM, RWKV_HEAD_DIM), lambda b, c, l: (b, 0, 0, 0))
    vec = pl.BlockSpec((None, 1, W_MIX), lambda b, c, l: (l[0], 0, 0))
    return _call(
        functools.partial(_wkv_scan_kernel, ch=ch, n_steps=t // rows), l, (b, t // rows),
        [blk] * 6 + [st] + [vec] * 5, [blk, st],
        [jax.ShapeDtypeStruct((b, t, W_MIX), BF16),
         jax.ShapeDtypeStruct((b, RWKV_HEADS, RWKV_HEAD_DIM, RWKV_HEAD_DIM), F32)],
        (r, k, v, ld, al, g, s0, lw['k_k'], lw['k_a'], lw['r_k'], lw['lnx_g'], lw['lnx_b']),
        scratch=[pltpu.VMEM((RWKV_HEADS, RWKV_HEAD_DIM, RWKV_HEAD_DIM), F32)],
        dims=("parallel", "arbitrary"))


def _split_heads_kernel(k_ref, v_ref, ko_ref, vo_ref):
    ko_ref[...] = k_ref[...].T.reshape(ko_ref.shape)
    vo_ref[...] = v_ref[...].T.reshape(vo_ref.shape)


def split_heads(l, proj3):
    b, t, _ = proj3.shape
    tt = min(512, t)
    col = lambda j: pl.BlockSpec((None, tt, W_MIX), lambda b, i, l: (b, i, OFF_C // W_MIX + j))
    out = pl.BlockSpec((None, SB_HEADS, SB_HEAD_DIM, tt), lambda b, i, l: (b, 0, 0, i))
    shape = jax.ShapeDtypeStruct((b, SB_HEADS, SB_HEAD_DIM, t), F32)
    return _call(_split_heads_kernel, l, (b, t // tt), [col(1), col(2)], [out, out], [shape, shape],
                 (proj3, proj3), dims=("parallel", "parallel"))


def _trunk_layer(l, x, pool0, conv0, shift0, wkv0, pos0, t_valid, attend, lw):
    b, t, d = x.shape
    m = b * t
    x2 = x.reshape(m, d)
    h = rmsnorm(l, x2, lw['g_ffn1'], BF16)
    x2 = matmul(l, ffn_in(l, h, lw['w_ffn1_in']), lw['w_ffn1_out'], tn=256, res=x2, scale=FFN_RESIDUAL)
    h = rmsnorm(l, x2, lw['g_mix'], BF16)
    proj = matmul(l, h, lw['w_in'], tn=256, n=OFF_G, tm=2048)
    proj3 = proj.reshape(b, t, OFF_G)
    y_a = pool_mixer(l, proj3, pool0, lw['w_pool'], lw['pool_scale'], pos0)
    y_b, glu = conv_mixer(l, proj3, conv0, lw['conv_w'], lw['conv_b'], lw['ln_g'], lw['ln_b'])
    y_c = attend(l, proj3)
    r, k, v, ld, al, g = wkv_pre(l, proj3, shift0, lw, t_valid)
    y_d, wkv_new = wkv_scan(l, r, k, v, ld, al, g, wkv0, lw)
    ys = [y.reshape(m, W_MIX) for y in (y_a, y_b, y_c, y_d)]
    merged = merge(l, h, ys, lw['w_in'], lw['w_branch'])
    x2 = matmul(l, merged, lw['w_out'], tn=256, res=x2, scale=1.0)
    h = rmsnorm(l, x2, lw['g_ffn2'], BF16)
    x2 = matmul(l, ffn_in(l, h, lw['w_ffn2_in']), lw['w_ffn2_out'], tn=256, res=x2, scale=FFN_RESIDUAL)
    return x2.reshape(b, t, d), proj3, glu, wkv_new


def _front_pad(rows, height):
    return jnp.pad(rows, ((0, 0), (height - rows.shape[1], 0), (0, 0)))


def kernel(x_prompt, x_sample, cache_k, cache_v, state_pool, state_conv, state_shift, state_wkv,
           page_table, g_ffn1, w_ffn1_in, w_ffn1_out, g_mix, w_in, w_pool, pool_scale, conv_w,
           conv_b, ln_g, ln_b, sb_bias, shift_mu, decay_w0, decay_up, a0, a_up, g_up, k_k, k_a, r_k,
           lnx_g, lnx_b, w_branch, w_out, g_ffn2, w_ffn2_in, w_ffn2_out, g_final):
    depth = w_in.shape[0]
    bp, tp, d = x_prompt.shape
    bs, ts, _ = x_sample.shape
    assert d == D_MODEL and ts == 1
    past_len = page_table.shape[1] * PAGE_SIZE
    row = lambda a: a.reshape(depth, 1, -1)
    lw = {
        'g_ffn1': row(g_ffn1), 'w_ffn1_in': w_ffn1_in, 'w_ffn1_out': w_ffn1_out, 'g_mix': row(g_mix),
        'w_in': w_in, 'w_pool': w_pool, 'pool_scale': row(pool_scale), 'conv_w': conv_w,
        'conv_b': row(conv_b), 'ln_g': row(ln_g), 'ln_b': row(ln_b), 'sb_bias': row(sb_bias),
        'shift_mu': row(shift_mu), 'decay_w0': row(decay_w0), 'decay_up': decay_up, 'a0': row(a0),
        'a_up': a_up, 'g_up': g_up, 'k_k': row(k_k), 'k_a': row(k_a), 'r_k': row(r_k),
        'lnx_g': row(lnx_g), 'lnx_b': row(lnx_b), 'w_branch': w_branch, 'w_out': w_out,
        'g_ffn2': row(g_ffn2), 'w_ffn2_in': w_ffn2_in, 'w_ffn2_out': w_ffn2_out,
    }
    sb_bias_col = sb_bias.reshape(depth, SB_HEADS, 1)
    cache_kt = jnp.transpose(cache_k, (0, 1, 3, 4, 2))
    cache_vt = jnp.transpose(cache_v, (0, 1, 3, 4, 2))
    own_head = (jnp.arange(W_MIX)[None, :] // SB_HEAD_DIM) == jnp.arange(SB_HEADS)[:, None]
    xs0 = jnp.pad(x_sample, ((0, 0), (0, SAMPLE_ROWS - ts), (0, 0)))

    def layer(carry, li):
        xp, xs = carry
        l = li.reshape(1)
        xp, proj_p, glu_p, wkv_p = _trunk_layer(
            l, xp, jnp.zeros((bp, POOL_HALO, W_MIX), F32), jnp.zeros((bp, CONV_HALO, W_MIX), F32),
            jnp.zeros((bp, SHIFT_HALO, N_D), F32),
            jnp.zeros((bp, RWKV_HEADS, RWKV_HEAD_DIM, RWKV_HEAD_DIM), F32), 0, None,
            lambda l, p3: sb_prompt(l, p3, lw['sb_bias']), lw)
        pool_s = lax.dynamic_index_in_dim(state_pool, li, 0, keepdims=False)
        conv_s = lax.dynamic_index_in_dim(state_conv, li, 0, keepdims=False)
        shift_s = lax.dynamic_index_in_dim(state_shift, li, 0, keepdims=False)
        wkv_s = lax.dynamic_index_in_dim(state_wkv, li, 0, keepdims=False)

        def attend_sample(l, p3):
            q_blocks = jnp.where(own_head, p3[:, 0:1, OFF_C:OFF_C + W_MIX], 0.0)
            y = sb_decode(l, q_blocks, sb_bias_col, cache_kt, cache_vt, page_table)
            return jnp.pad(y, ((0, 0), (0, SAMPLE_ROWS - 1), (0, 0))).astype(BF16)

        xs, proj_s, glu_s, wkv_s_new = _trunk_layer(
            l, xs, _front_pad(pool_s, POOL_HALO), _front_pad(conv_s, CONV_HALO),
            _front_pad(shift_s, SHIFT_HALO), wkv_s, past_len, ts, attend_sample, lw)

        k_p, v_p = split_heads(l, proj_p)
        k_s, v_s = (proj_s[:, :ts, OFF_C + j * W_MIX:OFF_C + (j + 1) * W_MIX].reshape(
            bs, ts, SB_HEADS, SB_HEAD_DIM) for j in (1, 2))
        outs = (
            k_p, v_p, proj_p[:, tp - POOL_BUF:, OFF_A:OFF_A + W_MIX], glu_p[:, tp - CONV_BUF:],
            proj_p[:, tp - 1:, OFF_D:OFF_D + N_D], wkv_p,
            k_s, v_s,
            jnp.concatenate([pool_s, proj_s[:, :ts, OFF_A:OFF_A + W_MIX]], axis=1)[:, -POOL_BUF:],
            jnp.concatenate([conv_s, glu_s[:, :ts]], axis=1)[:, -CONV_BUF:],
            proj_s[:, ts - 1:ts, OFF_D:OFF_D + N_D], wkv_s_new)
        return (xp, xs), outs

    (xp, xs), st = lax.scan(layer, (x_prompt, xs0), jnp.arange(depth, dtype=jnp.int32))
    zero = jnp.zeros((1,), jnp.int32)
    g_fin = g_final.reshape(1, 1, d)
    y_prompt = rmsnorm(zero, xp.reshape(bp * tp, d), g_fin, F32).reshape(bp, tp, d)
    y_sample = rmsnorm(zero, xs.reshape(bs * SAMPLE_ROWS, d), g_fin, F32).reshape(bs, SAMPLE_ROWS, d)[:, :ts]
    to_cache_layout = lambda a: jnp.transpose(a, (0, 1, 4, 2, 3))
    return (y_prompt, y_sample, to_cache_layout(st[0]), to_cache_layout(st[1])) + tuple(st[2:])
```

```python
import functools

import jax
import jax.numpy as jnp
from jax import lax
from jax.experimental import pallas as pl
from jax.experimental.pallas import tpu as pltpu

F32 = jnp.float32
BF16 = jnp.bfloat16
LOG2E = 1.4426950408889634

LANES = 128
SUBLANES = 8
VMEM_LIMIT_BYTES = 56 * 1024 * 1024

D_MODEL = 2048
W_MIX = D_MODEL // 4
POOL_WINDOWS = (2, 4, 8, 16)
POOL_GROUP = W_MIX // len(POOL_WINDOWS)
POOL_BUF = max(POOL_WINDOWS) - 1
CONV_WIDTH = 31
CONV_BUF = CONV_WIDTH - 1
SB_HEADS = 8
SB_HEAD_DIM = W_MIX // SB_HEADS
SB_SCALE = SB_HEAD_DIM ** -0.5
RWKV_HEAD_DIM = 64
RWKV_HEADS = W_MIX // RWKV_HEAD_DIM
DECAY_LORA = 64
A_LORA = 64
GATE_LORA = 128
N_LORA = DECAY_LORA + A_LORA + GATE_LORA
N_D = 3 * W_MIX + N_LORA
N_BRANCH = 4
OFF_A = 0
OFF_B = OFF_A + W_MIX
OFF_C = OFF_B + 2 * W_MIX
OFF_D = OFF_C + 3 * W_MIX
OFF_G = OFF_D + N_D
N_IN = OFF_G + N_BRANCH * D_MODEL
D_FF = ((8 * D_MODEL // 3 + 127) // 128) * 128
FFN_RESIDUAL = 0.5
RMS_EPS = 1e-6
LN_EPS = 1e-5
GN_EPS = 64e-5
PAGE_SIZE = 128

POOL_HALO = 16
CONV_HALO = 32
SHIFT_HALO = 8
SAMPLE_ROWS = 8
ATTN_BLOCK = 256
SB_GROUP = 4
DECODE_PAGES = 16
WKV_CHUNK = 64
WKV_CHUNKS_PER_STEP = 2


def _call(kernel, l, grid, in_specs, out_specs, out_shape, args, scratch=(), dims=None, prefetch=()):
    n_pre = 1 + len(prefetch)

    def body(*refs):
        kernel(*refs[n_pre:])

    grid_spec = pltpu.PrefetchScalarGridSpec(
        num_scalar_prefetch=n_pre, grid=grid, in_specs=in_specs, out_specs=out_specs,
        scratch_shapes=list(scratch))
    fn = kernel.func if isinstance(kernel, functools.partial) else kernel
    return pl.pallas_call(
        body, grid_spec=grid_spec, out_shape=out_shape, name=fn.__name__.strip("_"),
        compiler_params=pltpu.CompilerParams(
            dimension_semantics=dims or ("arbitrary",) * len(grid),
            vmem_limit_bytes=VMEM_LIMIT_BYTES),
    )(l, *prefetch, *args)


def _sigmoid(x):
    return 1.0 / (1.0 + jnp.exp(-x))


def _softplus(x):
    return jnp.maximum(x, 0.0) + jnp.log1p(jnp.exp(-jnp.abs(x)))


def _split_bf16(x):
    hi = x.astype(BF16)
    lo = (x - hi.astype(F32)).astype(BF16)
    return hi, lo


def _dot(a, b):
    return jnp.dot(a.astype(BF16), b.astype(BF16), preferred_element_type=F32)


def _dot_nt(a, b):
    return lax.dot_general(a.astype(BF16), b.astype(BF16), (((1,), (1,)), ((), ())),
                           preferred_element_type=F32)


def _dot_tn(a, b):
    return lax.dot_general(a.astype(BF16), b.astype(BF16), (((0,), (0,)), ((), ())),
                           preferred_element_type=F32)


def _rmsnorm_kernel(x_ref, g_ref, o_ref):
    x = x_ref[...]
    y = x * lax.rsqrt(jnp.mean(x * x, axis=-1, keepdims=True) + RMS_EPS)
    o_ref[...] = (y * g_ref[...]).astype(o_ref.dtype)


def rmsnorm(l, x, g, out_dtype):
    m, d = x.shape
    tm = min(512, m)
    return _call(
        _rmsnorm_kernel, l, (m // tm,),
        [pl.BlockSpec((tm, d), lambda i, l: (i, 0)),
         pl.BlockSpec((None, 1, d), lambda i, l: (l[0], 0, 0))],
        pl.BlockSpec((tm, d), lambda i, l: (i, 0)),
        jax.ShapeDtypeStruct((m, d), out_dtype), (x, g), dims=("parallel",))


def _side_spec(rows, cols, col_block):
    if col_block:
        return pl.BlockSpec((rows, cols), lambda i, j, l: (0, j))
    return pl.BlockSpec((rows, cols), lambda i, j, l: (0, 0))


def _side_out(rows, cols):
    return pl.BlockSpec((None, rows, cols), lambda i, j, l: (jnp.minimum(i, 1), 0, j))


def _side_slabs(row_tiles):
    return min(2, row_tiles)


def _on_first_row_tile(out_ref, compute):
    @pl.when(pl.program_id(0) == 0)
    def _():
        out_ref[...] = compute().astype(out_ref.dtype)

    @pl.when(pl.program_id(0) != 0)
    def _():
        out_ref[...] = jnp.zeros_like(out_ref)


def _ffn_in_kernel(x_ref, wg_ref, wu_ref, xs_ref, o_ref, os_ref):
    w = jnp.concatenate([wg_ref[...], wu_ref[...]], axis=1).astype(BF16)

    def act(x):
        r = jnp.dot(x, w, preferred_element_type=F32)
        g = r[:, :LANES]
        return g * _sigmoid(g) * r[:, LANES:]

    o_ref[...] = act(x_ref[...]).astype(o_ref.dtype)
    _on_first_row_tile(os_ref, lambda: act(xs_ref[...]))


def ffn_in(l, x, xs, w):
    m, d = x.shape
    ms = xs.shape[0]
    d_ff = w.shape[2] // 2
    nb = d_ff // LANES
    tm = min(2048, m)
    out, out_s = _call(
        _ffn_in_kernel, l, (m // tm, nb),
        [pl.BlockSpec((tm, d), lambda i, j, l: (i, 0)),
         pl.BlockSpec((None, d, LANES), lambda i, j, l: (l[0], 0, j)),
         pl.BlockSpec((None, d, LANES), lambda i, j, l: (l[0], 0, j + nb)),
         _side_spec(ms, d, False)],
        [pl.BlockSpec((tm, LANES), lambda i, j, l: (i, j)), _side_out(ms, LANES)],
        [jax.ShapeDtypeStruct((m, d_ff), BF16), jax.ShapeDtypeStruct((_side_slabs(m // tm), ms, d_ff), BF16)],
        (x, w, w, xs), dims=("arbitrary", "arbitrary"))
    return out, out_s[0]


def _mm_kernel(a_ref, w_ref, as_ref, o_ref, os_ref):
    w = w_ref[...].astype(BF16)
    o_ref[...] = jnp.dot(a_ref[...], w, preferred_element_type=F32)
    _on_first_row_tile(os_ref, lambda: jnp.dot(as_ref[...], w, preferred_element_type=F32))


def _mm_res_kernel(a_ref, w_ref, r_ref, as_ref, rs_ref, o_ref, os_ref, *, scale):
    w = w_ref[...].astype(BF16)
    o_ref[...] = r_ref[...] + scale * jnp.dot(a_ref[...], w, preferred_element_type=F32)
    _on_first_row_tile(
        os_ref, lambda: rs_ref[...] + scale * jnp.dot(as_ref[...], w, preferred_element_type=F32))


def matmul(l, a, a_s, w, *, tn, res=None, res_s=None, scale=1.0, n=None, tm=1024):
    m, k = a.shape
    ms = a_s.shape[0]
    n = w.shape[2] if n is None else n
    tm = min(tm, m)
    a_spec = pl.BlockSpec((tm, k), lambda i, j, l: (i, 0))
    w_spec = pl.BlockSpec((None, k, tn), lambda i, j, l: (l[0], 0, j))
    o_spec = pl.BlockSpec((tm, tn), lambda i, j, l: (i, j))
    out_specs = [o_spec, _side_out(ms, tn)]
    out_shape = [jax.ShapeDtypeStruct((m, n), F32), jax.ShapeDtypeStruct((_side_slabs(m // tm), ms, n), F32)]
    dims = ("arbitrary", "arbitrary")
    if res is None:
        out, out_s = _call(_mm_kernel, l, (m // tm, n // tn), [a_spec, w_spec, _side_spec(ms, k, False)],
                           out_specs, out_shape, (a, w, a_s), dims=dims)
    else:
        out, out_s = _call(
            functools.partial(_mm_res_kernel, scale=scale), l, (m // tm, n // tn),
            [a_spec, w_spec, o_spec, _side_spec(ms, k, False), _side_spec(ms, tn, True)],
            out_specs, out_shape, (a, w, res, a_s, res_s), dims=dims)
    return out, out_s[0]


def _merge_kernel(h_ref, ya_ref, yb_ref, yc_ref, yd_ref, g0_ref, g1_ref, g2_ref, g3_ref, wb_ref,
                  hs_ref, sa_ref, sb_ref, sc_ref, sd_ref, o_ref, os_ref):
    wg = [g_ref[...].astype(BF16) for g_ref in (g0_ref, g1_ref, g2_ref, g3_ref)]
    wb = [wb_ref[n].astype(BF16) for n in range(N_BRANCH)]

    def gated_sum(h, ys):
        acc = None
        for n in range(N_BRANCH):
            gate = jnp.dot(h, wg[n], preferred_element_type=F32)
            t = _sigmoid(gate) * jnp.dot(ys[n], wb[n], preferred_element_type=F32)
            acc = t if acc is None else acc + t
        return acc

    o_ref[...] = gated_sum(h_ref[...], [r[...] for r in (ya_ref, yb_ref, yc_ref, yd_ref)]).astype(o_ref.dtype)
    _on_first_row_tile(os_ref, lambda: gated_sum(hs_ref[...], [r[...] for r in (sa_ref, sb_ref, sc_ref, sd_ref)]))


def merge(l, h, ys, h_s, ys_s, w_in, w_branch):
    m = h.shape[0]
    ms = h_s.shape[0]
    tn = 256
    tm = min(1024, m)
    nj = D_MODEL // tn
    y_spec = pl.BlockSpec((tm, W_MIX), lambda i, j, l: (i, 0))
    g_specs = [pl.BlockSpec((None, D_MODEL, tn), functools.partial(
        lambda i, j, l, n: (l[0], 0, OFF_G // tn + n * nj + j), n=n)) for n in range(N_BRANCH)]
    out, out_s = _call(
        _merge_kernel, l, (m // tm, nj),
        [pl.BlockSpec((tm, D_MODEL), lambda i, j, l: (i, 0))] + [y_spec] * N_BRANCH + g_specs
        + [pl.BlockSpec((None, N_BRANCH, W_MIX, tn), lambda i, j, l: (l[0], 0, 0, j))]
        + [_side_spec(ms, D_MODEL, False)] + [_side_spec(ms, W_MIX, False)] * N_BRANCH,
        [pl.BlockSpec((tm, tn), lambda i, j, l: (i, j)), _side_out(ms, tn)],
        [jax.ShapeDtypeStruct((m, D_MODEL), BF16),
         jax.ShapeDtypeStruct((_side_slabs(m // tm), ms, D_MODEL), BF16)],
        (h, *ys, w_in, w_in, w_in, w_in, w_branch, h_s, *ys_s), dims=("arbitrary", "arbitrary"))
    return out, out_s[0]


def _pool_kernel(u_ref, t0_ref, wp_ref, ps_ref, y_ref, ext_ref, *, tt, pos0):
    i = pl.program_id(1)

    @pl.when(i == 0)
    def _():
        ext_ref[0:POOL_HALO, :] = t0_ref[...]

    @pl.when(i > 0)
    def _():
        ext_ref[0:POOL_HALO, :] = ext_ref[tt:tt + POOL_HALO, :]

    u = u_ref[...]
    ext_ref[POOL_HALO:POOL_HALO + tt, :] = u
    pos = pos0 + i * tt + lax.broadcasted_iota(jnp.int32, (tt, 1), 0)
    ys = []
    for gi, win in enumerate(POOL_WINDOWS):
        cs = slice(gi * POOL_GROUP, (gi + 1) * POOL_GROUP)
        wsum = u[:, cs]
        for k in range(1, win):
            wsum = wsum + ext_ref[POOL_HALO - k:POOL_HALO - k + tt, cs]
        count = jnp.minimum(win, pos + 1).astype(F32)
        d = wsum / count - u[:, cs]
        ys.append(_dot(d, wp_ref[gi]))
    y_ref[...] = (jnp.concatenate(ys, axis=1) * ps_ref[...]).astype(y_ref.dtype)


def pool_mixer(l, proj3, tail0, w_pool, pool_scale, pos0):
    b, t, _ = proj3.shape
    tt = min(512, t)
    return _call(
        functools.partial(_pool_kernel, tt=tt, pos0=pos0), l, (b, t // tt),
        [pl.BlockSpec((None, tt, W_MIX), lambda b, i, l: (b, i, OFF_A // W_MIX)),
         pl.BlockSpec((None, POOL_HALO, W_MIX), lambda b, i, l: (b, 0, 0)),
         pl.BlockSpec((None, len(POOL_WINDOWS), POOL_GROUP, POOL_GROUP), lambda b, i, l: (l[0], 0, 0, 0)),
         pl.BlockSpec((None, 1, W_MIX), lambda b, i, l: (l[0], 0, 0))],
        pl.BlockSpec((None, tt, W_MIX), lambda b, i, l: (b, i, 0)),
        jax.ShapeDtypeStruct((b, t, W_MIX), BF16), (proj3, tail0, w_pool, pool_scale),
        scratch=[pltpu.VMEM((POOL_HALO + tt, W_MIX), F32)], dims=("parallel", "arbitrary"))


def _conv_kernel(val_ref, gate_ref, t0_ref, w_ref, cb_ref, lg_ref, lb_ref, y_ref, u_ref, ext_ref, sh_ref,
                 *, tt):
    i = pl.program_id(1)

    @pl.when(i == 0)
    def _():
        ext_ref[0:CONV_HALO, :] = t0_ref[...]

    @pl.when(i > 0)
    def _():
        ext_ref[0:CONV_HALO, :] = ext_ref[tt:tt + CONV_HALO, :]

    u = val_ref[...] * _sigmoid(gate_ref[...])
    u_ref[...] = u
    ext_ref[CONV_HALO:CONV_HALO + tt, :] = u
    first = CONV_HALO - CONV_BUF
    last = first + CONV_WIDTH - 1
    for b in range(SUBLANES):
        top = max(s for s in range(first, last + 1) if s % SUBLANES == b)
        sh_ref[b, 0:top - b + tt, :] = ext_ref[b:top + tt, :]
    acc = cb_ref[...]
    for j in range(CONV_WIDTH):
        s = first + j
        base = s - s % SUBLANES
        acc = acc + sh_ref[s % SUBLANES, base:base + tt, :] * w_ref[j:j + 1, :]
    y = acc
    mu = jnp.mean(y, axis=-1, keepdims=True)
    yc = y - mu
    var = jnp.mean(yc * yc, axis=-1, keepdims=True)
    yn = yc * lax.rsqrt(var + LN_EPS) * lg_ref[...] + lb_ref[...]
    y_ref[...] = (yn * _sigmoid(yn)).astype(y_ref.dtype)


def conv_mixer(l, proj3, tail0, conv_w, conv_b, ln_g, ln_b):
    b, t, _ = proj3.shape
    tt = min(256, t)
    vec = pl.BlockSpec((None, 1, W_MIX), lambda b, i, l: (l[0], 0, 0))
    blk = pl.BlockSpec((None, tt, W_MIX), lambda b, i, l: (b, i, 0))
    return _call(
        functools.partial(_conv_kernel, tt=tt), l, (b, t // tt),
        [pl.BlockSpec((None, tt, W_MIX), lambda b, i, l: (b, i, OFF_B // W_MIX)),
         pl.BlockSpec((None, tt, W_MIX), lambda b, i, l: (b, i, OFF_B // W_MIX + 1)),
         pl.BlockSpec((None, CONV_HALO, W_MIX), lambda b, i, l: (b, 0, 0)),
         pl.BlockSpec((None, CONV_WIDTH, W_MIX), lambda b, i, l: (l[0], 0, 0)),
         vec, vec, vec],
        [blk, blk],
        [jax.ShapeDtypeStruct((b, t, W_MIX), BF16), jax.ShapeDtypeStruct((b, t, W_MIX), F32)],
        (proj3, proj3, tail0, conv_w, conv_b, ln_g, ln_b),
        scratch=[pltpu.VMEM((CONV_HALO + tt, W_MIX), F32),
                 pltpu.VMEM((SUBLANES, CONV_HALO + tt, W_MIX), F32)], dims=("parallel", "arbitrary"))


def _sb_heads(x):
    n = x.shape[1] // SB_HEAD_DIM
    return jnp.stack([x[:, h * SB_HEAD_DIM:(h + 1) * SB_HEAD_DIM] for h in range(n)], axis=0)


def _sb_block(q3, k3, v3, bias, carry, acc, neg_upper, mask):
    nh, tq, _ = q3.shape
    tk = k3.shape[1]
    z = _bdot_nt(q3, k3)
    z = jnp.stack([z[h] + bias[h] for h in range(nh)], axis=0)
    fail = jnp.maximum(z, 0.0) + jnp.log(1.0 + jnp.exp2(jnp.abs(z) * (-LOG2E)))
    if mask is not None:
        fail = jnp.where(mask, fail, 0.0)
    later = jnp.dot(fail.astype(BF16).reshape(nh * tq, tk), neg_upper,
                    preferred_element_type=F32).reshape(nh, tq, tk)
    wts = jnp.exp(z - fail + later + carry)
    if mask is not None:
        wts = jnp.where(mask, wts, 0.0)
    acc = acc + _bdot(wts, v3)
    carry = carry + later[:, :, 0:1] - fail[:, :, 0:1]
    return carry, acc


def _sb_prompt_kernel(bias_ref, q_ref, k_ref, v_ref, o_ref, kb_ref, vb_ref, *, blk, n_blk):
    qi = pl.program_id(2)
    grp = pl.program_id(1)

    @pl.when(qi == 0)
    def _():
        def fill(j, _):
            rows = pl.ds(pl.multiple_of(j * blk, blk), blk)
            kb_ref[:, rows, :] = _sb_heads(k_ref[rows, :].astype(BF16))
            vb_ref[:, rows, :] = _sb_heads(v_ref[rows, :].astype(BF16))
            return 0
        lax.fori_loop(0, n_blk, fill, 0)

    row = lax.broadcasted_iota(jnp.int32, (blk, blk), 0)
    col = lax.broadcasted_iota(jnp.int32, (blk, blk), 1)
    causal = row > col
    neg_upper = jnp.where(causal, -1.0, 0.0).astype(BF16)
    q3 = _sb_heads((q_ref[...] * SB_SCALE).astype(BF16))
    nh = q3.shape[0]
    bias = [bias_ref[0, grp * nh + h] for h in range(nh)]

    def kv(j):
        rows = pl.ds(pl.multiple_of(j * blk, blk), blk)
        return kb_ref[:, rows, :], vb_ref[:, rows, :]

    carry = jnp.zeros((nh, blk, 1), F32)
    acc = jnp.zeros((nh, blk, SB_HEAD_DIM), F32)
    carry, acc = _sb_block(q3, *kv(qi), bias, carry, acc, neg_upper, causal)

    def body(it, state):
        return _sb_block(q3, *kv(qi - 1 - it), bias, state[0], state[1], neg_upper, None)

    carry, acc = lax.fori_loop(0, qi, body, (carry, acc))
    o_ref[...] = jnp.concatenate([acc[h] for h in range(nh)], axis=1).astype(o_ref.dtype)


def sb_prompt(l, proj3, sb_bias):
    b, t, _ = proj3.shape
    blk = min(ATTN_BLOCK, t)
    wid = SB_GROUP * SB_HEAD_DIM
    ngrp = W_MIX // wid
    c0 = OFF_C // wid
    return _call(
        functools.partial(_sb_prompt_kernel, blk=blk, n_blk=t // blk), l, (b, ngrp, t // blk),
        [pl.BlockSpec((None, 1, SB_HEADS), lambda b, h, i, l: (l[0], 0, 0), memory_space=pltpu.SMEM),
         pl.BlockSpec((None, blk, wid), lambda b, h, i, l: (b, i, c0 + h)),
         pl.BlockSpec((None, t, wid), lambda b, h, i, l: (b, 0, c0 + ngrp + h)),
         pl.BlockSpec((None, t, wid), lambda b, h, i, l: (b, 0, c0 + 2 * ngrp + h))],
        pl.BlockSpec((None, blk, wid), lambda b, h, i, l: (b, i, h)),
        jax.ShapeDtypeStruct((b, t, W_MIX), BF16), (sb_bias, proj3, proj3, proj3),
        scratch=[pltpu.VMEM((SB_GROUP, t, SB_HEAD_DIM), BF16)] * 2,
        dims=("parallel", "parallel", "arbitrary"))


def _sb_decode_kernel(q_ref, bias_ref, *refs, n_steps, pages):
    k_refs, v_refs = refs[:pages], refs[pages:2 * pages]
    o_ref, carry_ref, acc_ref = refs[2 * pages:]
    p = pl.program_id(1)

    @pl.when(p == 0)
    def _():
        carry_ref[...] = jnp.zeros_like(carry_ref)
        acc_ref[...] = jnp.zeros_like(acc_ref)

    q = (q_ref[...] * SB_SCALE).astype(BF16)
    bias = bias_ref[...]
    z = jnp.concatenate([_dot(q, k_ref[...].reshape(W_MIX, PAGE_SIZE)) + bias for k_ref in k_refs], axis=0)
    log_fail = -_softplus(z)
    row = lax.broadcasted_iota(jnp.int32, (PAGE_SIZE, PAGE_SIZE), 0)
    col = lax.broadcasted_iota(jnp.int32, (PAGE_SIZE, PAGE_SIZE), 1)
    upper = (row > col).astype(BF16)
    later = _dot(log_fail, upper)
    total = later[:, 0:1] + log_fail[:, 0:1]
    arg = z + log_fail + later
    carry = carry_ref[...]
    acc = acc_ref[...]
    for i in reversed(range(pages)):
        rows = slice(i * SB_HEADS, (i + 1) * SB_HEADS)
        wts = jnp.exp(arg[rows] + carry)
        acc = acc + _dot_nt(wts, v_refs[i][...].reshape(W_MIX, PAGE_SIZE))
        carry = carry + total[rows]
    carry_ref[...] = carry
    acc_ref[...] = acc

    @pl.when(p == n_steps - 1)
    def _():
        head = lax.broadcasted_iota(jnp.int32, acc.shape, 0)
        col_head = lax.broadcasted_iota(jnp.int32, acc.shape, 1) // SB_HEAD_DIM
        o_ref[...] = jnp.sum(jnp.where(head == col_head, acc, 0.0), axis=0, keepdims=True)


def sb_decode(l, q_blocks, sb_bias_col, cache_kt, cache_vt, page_table):
    b = q_blocks.shape[0]
    n_pages = page_table.shape[1]
    pages = DECODE_PAGES if n_pages % DECODE_PAGES == 0 else 1
    n_steps = n_pages // pages

    def page(i):
        return pl.BlockSpec((None, None, SB_HEADS, SB_HEAD_DIM, PAGE_SIZE),
                            lambda b, p, l, pt: (l[0], pt[b, n_pages - pages * (p + 1) + i], 0, 0, 0))

    return _call(
        functools.partial(_sb_decode_kernel, n_steps=n_steps, pages=pages), l, (b, n_steps),
        [pl.BlockSpec((None, SB_HEADS, W_MIX), lambda b, p, l, pt: (b, 0, 0)),
         pl.BlockSpec((None, SB_HEADS, 1), lambda b, p, l, pt: (l[0], 0, 0))]
        + [page(i) for i in range(pages)] * 2,
        pl.BlockSpec((None, 1, W_MIX), lambda b, p, l, pt: (b, 0, 0)),
        jax.ShapeDtypeStruct((b, 1, W_MIX), F32),
        (q_blocks, sb_bias_col) + (cache_kt,) * pages + (cache_vt,) * pages,
        scratch=[pltpu.VMEM((SB_HEADS, 1), F32), pltpu.VMEM((SB_HEADS, W_MIX), F32)],
        dims=("parallel", "arbitrary"), prefetch=(page_table,))


def _wkv_pre_kernel(pr_ref, pk_ref, pv_ref, pl_ref, s0_ref, mu_ref, w0_ref, dup_ref, a0_ref, aup_ref,
                    gup_ref, r_ref, k_ref, v_ref, ld_ref, al_ref, g_ref, ext_ref, *, tt, t_valid):
    i = pl.program_id(1)

    @pl.when(i == 0)
    def _():
        ext_ref[0:SHIFT_HALO, :] = s0_ref[...]

    @pl.when(i > 0)
    def _():
        ext_ref[0:SHIFT_HALO, :] = ext_ref[tt:tt + SHIFT_HALO, :]

    rows = slice(SHIFT_HALO, SHIFT_HALO + tt)
    ext_ref[rows, 0:W_MIX] = pr_ref[...]
    ext_ref[rows, W_MIX:2 * W_MIX] = pk_ref[...]
    ext_ref[rows, 2 * W_MIX:3 * W_MIX] = pv_ref[...]
    ext_ref[rows, 3 * W_MIX:N_D] = pl_ref[...]
    p = ext_ref[rows, :]
    prev = ext_ref[SHIFT_HALO - 1:SHIFT_HALO - 1 + tt, :]
    xs = p + (prev - p) * mu_ref[...]
    o = 3 * W_MIX
    w_dn = xs[:, o:o + DECAY_LORA]
    a_dn = xs[:, o + DECAY_LORA:o + DECAY_LORA + A_LORA]
    g_dn = xs[:, o + DECAY_LORA + A_LORA:N_D]
    w_log = -_softplus(-(w0_ref[...] + _dot(jnp.tanh(w_dn), dup_ref[...]))) - 0.5
    log_decay = -jnp.exp(w_log)
    alpha = _sigmoid(a0_ref[...] + _dot(a_dn, aup_ref[...]))
    k = xs[:, W_MIX:2 * W_MIX]
    v = xs[:, 2 * W_MIX:3 * W_MIX]
    if t_valid is not None:
        valid = (i * tt + lax.broadcasted_iota(jnp.int32, (tt, 1), 0)) < t_valid
        k = jnp.where(valid, k, 0.0)
        v = jnp.where(valid, v, 0.0)
        log_decay = jnp.where(valid, log_decay, 0.0)
    r_ref[...] = xs[:, 0:W_MIX]
    k_ref[...] = k
    v_ref[...] = v
    ld_ref[...] = log_decay
    al_ref[...] = alpha
    g_ref[...] = _dot(_sigmoid(g_dn), gup_ref[...])


def wkv_pre(l, proj3, shift0, lw, t_valid):
    b, t, _ = proj3.shape
    tt = min(256, t)
    cd = OFF_D // W_MIX
    vec = lambda n: pl.BlockSpec((None, 1, n), lambda b, i, l: (l[0], 0, 0))
    up = lambda n: pl.BlockSpec((None, n, W_MIX), lambda b, i, l: (l[0], 0, 0))
    blk = pl.BlockSpec((None, tt, W_MIX), lambda b, i, l: (b, i, 0))
    return _call(
        functools.partial(_wkv_pre_kernel, tt=tt, t_valid=t_valid), l, (b, t // tt),
        [pl.BlockSpec((None, tt, W_MIX), lambda b, i, l: (b, i, cd)),
         pl.BlockSpec((None, tt, W_MIX), lambda b, i, l: (b, i, cd + 1)),
         pl.BlockSpec((None, tt, W_MIX), lambda b, i, l: (b, i, cd + 2)),
         pl.BlockSpec((None, tt, N_LORA), lambda b, i, l: (b, i, (OFF_D + 3 * W_MIX) // N_LORA)),
         pl.BlockSpec((None, SHIFT_HALO, N_D), lambda b, i, l: (b, 0, 0)),
         vec(N_D), vec(W_MIX), up(DECAY_LORA), vec(W_MIX), up(A_LORA), up(GATE_LORA)],
        [blk] * 6, [jax.ShapeDtypeStruct((b, t, W_MIX), F32)] * 6,
        (proj3, proj3, proj3, proj3, shift0, lw['shift_mu'], lw['decay_w0'], lw['decay_up'],
         lw['a0'], lw['a_up'], lw['g_up']),
        scratch=[pltpu.VMEM((SHIFT_HALO + tt, N_D), F32)], dims=("parallel", "arbitrary"))


def _bdot(a, b):
    return lax.dot_general(a.astype(BF16), b.astype(BF16), (((2,), (1,)), ((0,), (0,))),
                           preferred_element_type=F32)


def _bdot_nt(a, b):
    return lax.dot_general(a.astype(BF16), b.astype(BF16), (((2,), (2,)), ((0,), (0,))),
                           preferred_element_type=F32)


def _bdot_tn(a, b):
    return lax.dot_general(a.astype(BF16), b.astype(BF16), (((1,), (1,)), ((0,), (0,))),
                           preferred_element_type=F32)


def _heads(x):
    return jnp.stack([x[:, h * RWKV_HEAD_DIM:(h + 1) * RWKV_HEAD_DIM] for h in range(RWKV_HEADS)], axis=0)


def _head_sums(xs):
    r = lax.broadcasted_iota(jnp.int32, (W_MIX, W_MIX), 0) // RWKV_HEAD_DIM
    c = lax.broadcasted_iota(jnp.int32, (W_MIX, W_MIX), 1) // RWKV_HEAD_DIM
    same = (r == c).astype(BF16)
    rows = xs[0].shape[0]
    parts = [p for x in xs for p in _split_bf16(x)]
    out = jnp.dot(jnp.concatenate(parts, axis=0), same, preferred_element_type=F32)
    return [out[(2 * i) * rows:(2 * i + 1) * rows] + out[(2 * i + 1) * rows:(2 * i + 2) * rows]
            for i in range(len(xs))]


def _wkv_chunk(r, k, v, ld, al, g, s, kk_w, ka_w, rk_w, ln_g, ln_b):
    ch = r.shape[0]
    row = lax.broadcasted_iota(jnp.int32, (ch, ch), 0)
    col = lax.broadcasted_iota(jnp.int32, (ch, ch), 1)
    incl = row >= col
    eye = (row == col).astype(F32)
    row2 = lax.broadcasted_iota(jnp.int32, (2 * ch, 2 * ch), 0)
    col2 = lax.broadcasted_iota(jnp.int32, (2 * ch, 2 * ch), 1) & (ch - 1)
    tri = (row2 & (ch - 1)) + (row2 // ch) > col2
    hi, lo = _split_bf16(ld)
    lower = incl.astype(BF16)
    cum = (jnp.dot(lower, hi, preferred_element_type=F32)
           + jnp.dot(lower, lo, preferred_element_type=F32))
    cum_end = cum[ch - 1:ch, :]
    g_in = jnp.exp(cum)
    g_ex = jnp.exp(cum - ld)
    g_inv = jnp.exp(-cum)
    g_end = jnp.exp(cum_end - cum)
    g_tot = jnp.exp(cum_end)
    kk = k * kk_w
    k2 = k * (1.0 + (al - 1.0) * ka_w)
    kk_sq, rk_sum = _head_sums([kk * kk, r * k2 * rk_w])
    kkn = kk / jnp.maximum(jnp.sqrt(kk_sq), 1e-12)
    b = kkn * al
    a2 = _heads(jnp.concatenate([-kkn * g_ex, r * g_in], axis=0))
    bm = _heads(jnp.concatenate([b * g_inv, k2 * g_inv], axis=0))
    bk_end = _heads(jnp.concatenate([b * g_end, k2 * g_end], axis=0))
    v3 = _heads(v)
    prod = jnp.where(tri, _bdot_nt(a2, bm), 0.0)
    x = prod[:, 0:ch, 0:ch]
    inv = eye + x
    n_sq = ch.bit_length() - 2
    for _ in range(n_sq):
        x = _bdot(x, x)
        inv = inv + _bdot(inv, x)
    w_rhs = _bdot_nt(a2[:, 0:ch], s) + _bdot(prod[:, 0:ch], jnp.concatenate([jnp.zeros_like(v3), v3], axis=1))
    u = _bdot(inv, w_rhs)
    uv = jnp.concatenate([u, v3], axis=1)
    y3 = _bdot_nt(a2[:, ch:2 * ch], s) + _bdot(prod[:, ch:2 * ch], uv)
    s_new = s * _heads(g_tot) + _bdot_tn(uv, bk_end)
    mu = jnp.mean(y3, axis=-1, keepdims=True)
    yc = y3 - mu
    var = jnp.mean(yc * yc, axis=-1, keepdims=True)
    yn3 = yc * lax.rsqrt(var + GN_EPS)
    yn = jnp.concatenate([yn3[h] for h in range(RWKV_HEADS)], axis=1)
    return (yn * ln_g + ln_b + rk_sum * v) * g, s_new


def _wkv_scan_kernel(r_ref, k_ref, v_ref, ld_ref, al_ref, g_ref, s0_ref, kk_ref, ka_ref, rk_ref,
                     lg_ref, lb_ref, y_ref, so_ref, s_ref, *, ch, n_steps):
    c = pl.program_id(1)

    @pl.when(c == 0)
    def _():
        s_ref[...] = s0_ref[...]

    s = s_ref[...]
    for j in range(r_ref.shape[0] // ch):
        rows = slice(j * ch, (j + 1) * ch)
        y, s = _wkv_chunk(r_ref[rows, :], k_ref[rows, :], v_ref[rows, :], ld_ref[rows, :], al_ref[rows, :],
                          g_ref[rows, :], s, kk_ref[...], ka_ref[...], rk_ref[...], lg_ref[...], lb_ref[...])
        y_ref[rows, :] = y.astype(y_ref.dtype)
    s_ref[...] = s

    @pl.when(c == n_steps - 1)
    def _():
        so_ref[...] = s


def wkv_scan(l, r, k, v, ld, al, g, s0, lw):
    b, t, _ = r.shape
    ch = min(WKV_CHUNK, t)
    rows = ch * WKV_CHUNKS_PER_STEP if t % (ch * WKV_CHUNKS_PER_STEP) == 0 else ch
    blk = pl.BlockSpec((None, rows, W_MIX), lambda b, c, l: (b, c, 0))
    st = pl.BlockSpec((None, RWKV_HEADS, RWKV_HEAD_DIM, RWKV_HEAD_DIM), lambda b, c, l: (b, 0, 0, 0))
    vec = pl.BlockSpec((None, 1, W_MIX), lambda b, c, l: (l[0], 0, 0))
    return _call(
        functools.partial(_wkv_scan_kernel, ch=ch, n_steps=t // rows), l, (b, t // rows),
        [blk] * 6 + [st] + [vec] * 5, [blk, st],
        [jax.ShapeDtypeStruct((b, t, W_MIX), BF16),
         jax.ShapeDtypeStruct((b, RWKV_HEADS, RWKV_HEAD_DIM, RWKV_HEAD_DIM), F32)],
        (r, k, v, ld, al, g, s0, lw['k_k'], lw['k_a'], lw['r_k'], lw['lnx_g'], lw['lnx_b']),
        scratch=[pltpu.VMEM((RWKV_HEADS, RWKV_HEAD_DIM, RWKV_HEAD_DIM), F32)],
        dims=("parallel", "arbitrary"))


def _split_heads_kernel(k_ref, v_ref, ko_ref, vo_ref):
    ko_ref[...] = k_ref[...].T.reshape(ko_ref.shape)
    vo_ref[...] = v_ref[...].T.reshape(vo_ref.shape)


def split_heads(l, proj3):
    b, t, _ = proj3.shape
    tt = min(512, t)
    col = lambda j: pl.BlockSpec((None, tt, W_MIX), lambda b, i, l: (b, i, OFF_C // W_MIX + j))
    out = pl.BlockSpec((None, SB_HEADS, SB_HEAD_DIM, tt), lambda b, i, l: (b, 0, 0, i))
    shape = jax.ShapeDtypeStruct((b, SB_HEADS, SB_HEAD_DIM, t), F32)
    return _call(_split_heads_kernel, l, (b, t // tt), [col(1), col(2)], [out, out], [shape, shape],
                 (proj3, proj3), dims=("parallel", "parallel"))


def _mixers(l, proj3, pool0, conv0, shift0, wkv0, pos0, t_valid, attend, lw):
    b, t, _ = proj3.shape
    y_a = pool_mixer(l, proj3, pool0, lw['w_pool'], lw['pool_scale'], pos0)
    y_b, glu = conv_mixer(l, proj3, conv0, lw['conv_w'], lw['conv_b'], lw['ln_g'], lw['ln_b'])
    y_c = attend(l, proj3)
    r, k, v, ld, al, g = wkv_pre(l, proj3, shift0, lw, t_valid)
    y_d, wkv_new = wkv_scan(l, r, k, v, ld, al, g, wkv0, lw)
    return [y.reshape(b * t, W_MIX) for y in (y_a, y_b, y_c, y_d)], glu, wkv_new


def _trunk_layer(l, xp, xs, mix_p, mix_s, lw):
    dp, ds = xp.shape, xs.shape
    xp, xs = xp.reshape(-1, dp[2]), xs.reshape(-1, ds[2])

    def ffn(xp, xs, g, w_a, w_b):
        act_p, act_s = ffn_in(l, rmsnorm(l, xp, g, BF16), rmsnorm(l, xs, g, BF16), w_a)
        return matmul(l, act_p, act_s, w_b, tn=256, res=xp, res_s=xs, scale=FFN_RESIDUAL)

    xp, xs = ffn(xp, xs, lw['g_ffn1'], lw['w_ffn1_in'], lw['w_ffn1_out'])
    hp, hs = rmsnorm(l, xp, lw['g_mix'], BF16), rmsnorm(l, xs, lw['g_mix'], BF16)
    proj_p, proj_s = matmul(l, hp, hs, lw['w_in'], tn=256, n=OFF_G, tm=2048)
    proj_p, proj_s = proj_p.reshape(dp[0], dp[1], OFF_G), proj_s.reshape(ds[0], ds[1], OFF_G)
    ys_p, glu_p, wkv_p = mix_p(l, proj_p)
    ys_s, glu_s, wkv_s = mix_s(l, proj_s)
    merged_p, merged_s = merge(l, hp, ys_p, hs, ys_s, lw['w_in'], lw['w_branch'])
    xp, xs = matmul(l, merged_p, merged_s, lw['w_out'], tn=256, res=xp, res_s=xs, scale=1.0)
    xp, xs = ffn(xp, xs, lw['g_ffn2'], lw['w_ffn2_in'], lw['w_ffn2_out'])
    return xp.reshape(dp), xs.reshape(ds), (proj_p, glu_p, wkv_p), (proj_s, glu_s, wkv_s)


def _front_pad(rows, height):
    return jnp.pad(rows, ((0, 0), (height - rows.shape[1], 0), (0, 0)))


def kernel(x_prompt, x_sample, cache_k, cache_v, state_pool, state_conv, state_shift, state_wkv,
           page_table, g_ffn1, w_ffn1_in, w_ffn1_out, g_mix, w_in, w_pool, pool_scale, conv_w,
           conv_b, ln_g, ln_b, sb_bias, shift_mu, decay_w0, decay_up, a0, a_up, g_up, k_k, k_a, r_k,
           lnx_g, lnx_b, w_branch, w_out, g_ffn2, w_ffn2_in, w_ffn2_out, g_final):
    depth = w_in.shape[0]
    bp, tp, d = x_prompt.shape
    bs, ts, _ = x_sample.shape
    assert d == D_MODEL and ts == 1
    past_len = page_table.shape[1] * PAGE_SIZE
    row = lambda a: a.reshape(depth, 1, -1)
    lw = {
        'g_ffn1': row(g_ffn1), 'w_ffn1_in': w_ffn1_in, 'w_ffn1_out': w_ffn1_out, 'g_mix': row(g_mix),
        'w_in': w_in, 'w_pool': w_pool, 'pool_scale': row(pool_scale), 'conv_w': conv_w,
        'conv_b': row(conv_b), 'ln_g': row(ln_g), 'ln_b': row(ln_b), 'sb_bias': row(sb_bias),
        'shift_mu': row(shift_mu), 'decay_w0': row(decay_w0), 'decay_up': decay_up, 'a0': row(a0),
        'a_up': a_up, 'g_up': g_up, 'k_k': row(k_k), 'k_a': row(k_a), 'r_k': row(r_k),
        'lnx_g': row(lnx_g), 'lnx_b': row(lnx_b), 'w_branch': w_branch, 'w_out': w_out,
        'g_ffn2': row(g_ffn2), 'w_ffn2_in': w_ffn2_in, 'w_ffn2_out': w_ffn2_out,
    }
    sb_bias_col = sb_bias.reshape(depth, SB_HEADS, 1)
    cache_kt = jnp.transpose(cache_k, (0, 1, 3, 4, 2))
    cache_vt = jnp.transpose(cache_v, (0, 1, 3, 4, 2))
    own_head = (jnp.arange(W_MIX)[None, :] // SB_HEAD_DIM) == jnp.arange(SB_HEADS)[:, None]
    xs0 = jnp.pad(x_sample, ((0, 0), (0, SAMPLE_ROWS - ts), (0, 0)))

    def layer(carry, li):
        xp, xs = carry
        l = li.reshape(1)
        pool_s = lax.dynamic_index_in_dim(state_pool, li, 0, keepdims=False)
        conv_s = lax.dynamic_index_in_dim(state_conv, li, 0, keepdims=False)
        shift_s = lax.dynamic_index_in_dim(state_shift, li, 0, keepdims=False)
        wkv_s = lax.dynamic_index_in_dim(state_wkv, li, 0, keepdims=False)

        def attend_sample(l, p3):
            q_blocks = jnp.where(own_head, p3[:, 0:1, OFF_C:OFF_C + W_MIX], 0.0)
            y = sb_decode(l, q_blocks, sb_bias_col, cache_kt, cache_vt, page_table)
            return jnp.pad(y, ((0, 0), (0, SAMPLE_ROWS - 1), (0, 0))).astype(BF16)

        mix_p = lambda l, p3: _mixers(
            l, p3, jnp.zeros((bp, POOL_HALO, W_MIX), F32), jnp.zeros((bp, CONV_HALO, W_MIX), F32),
            jnp.zeros((bp, SHIFT_HALO, N_D), F32),
            jnp.zeros((bp, RWKV_HEADS, RWKV_HEAD_DIM, RWKV_HEAD_DIM), F32), 0, None,
            lambda l, p3: sb_prompt(l, p3, lw['sb_bias']), lw)
        mix_s = lambda l, p3: _mixers(
            l, p3, _front_pad(pool_s, POOL_HALO), _front_pad(conv_s, CONV_HALO),
            _front_pad(shift_s, SHIFT_HALO), wkv_s, past_len, ts, attend_sample, lw)
        xp, xs, (proj_p, glu_p, wkv_p), (proj_s, glu_s, wkv_s_new) = _trunk_layer(l, xp, xs, mix_p, mix_s, lw)

        k_p, v_p = split_heads(l, proj_p)
        k_s, v_s = (proj_s[:, :ts, OFF_C + j * W_MIX:OFF_C + (j + 1) * W_MIX].reshape(
            bs, ts, SB_HEADS, SB_HEAD_DIM) for j in (1, 2))
        outs = (
            k_p, v_p, proj_p[:, tp - POOL_BUF:, OFF_A:OFF_A + W_MIX], glu_p[:, tp - CONV_BUF:],
            proj_p[:, tp - 1:, OFF_D:OFF_D + N_D], wkv_p,
            k_s, v_s,
            jnp.concatenate([pool_s, proj_s[:, :ts, OFF_A:OFF_A + W_MIX]], axis=1)[:, -POOL_BUF:],
            jnp.concatenate([conv_s, glu_s[:, :ts]], axis=1)[:, -CONV_BUF:],
            proj_s[:, ts - 1:ts, OFF_D:OFF_D + N_D], wkv_s_new)
        return (xp, xs), outs

    (xp, xs), st = lax.scan(layer, (x_prompt, xs0), jnp.arange(depth, dtype=jnp.int32))
    zero = jnp.zeros((1,), jnp.int32)
    g_fin = g_final.reshape(1, 1, d)
    y_prompt = rmsnorm(zero, xp.reshape(bp * tp, d), g_fin, F32).reshape(bp, tp, d)
    y_sample = rmsnorm(zero, xs.reshape(bs * SAMPLE_ROWS, d), g_fin, F32).reshape(bs, SAMPLE_ROWS, d)[:, :ts]
    to_cache_layout = lambda a: jnp.transpose(a, (0, 1, 4, 2, 3))
    return (y_prompt, y_sample, to_cache_layout(st[0]), to_cache_layout(st[1])) + tuple(st[2:])
```

```python
import functools

import jax
import jax.numpy as jnp
from jax import lax
from jax.experimental import pallas as pl
from jax.experimental.pallas import tpu as pltpu

F32 = jnp.float32
BF16 = jnp.bfloat16
LOG2E = 1.4426950408889634

LANES = 128
SUBLANES = 8
VMEM_LIMIT_BYTES = 56 * 1024 * 1024

D_MODEL = 2048
W_MIX = D_MODEL // 4
POOL_WINDOWS = (2, 4, 8, 16)
POOL_GROUP = W_MIX // len(POOL_WINDOWS)
POOL_BUF = max(POOL_WINDOWS) - 1
CONV_WIDTH = 31
CONV_BUF = CONV_WIDTH - 1
SB_HEADS = 8
SB_HEAD_DIM = W_MIX // SB_HEADS
SB_SCALE = SB_HEAD_DIM ** -0.5
RWKV_HEAD_DIM = 64
RWKV_HEADS = W_MIX // RWKV_HEAD_DIM
DECAY_LORA = 64
A_LORA = 64
GATE_LORA = 128
N_LORA = DECAY_LORA + A_LORA + GATE_LORA
N_D = 3 * W_MIX + N_LORA
N_BRANCH = 4
OFF_A = 0
OFF_B = OFF_A + W_MIX
OFF_C = OFF_B + 2 * W_MIX
OFF_D = OFF_C + 3 * W_MIX
OFF_G = OFF_D + N_D
N_IN = OFF_G + N_BRANCH * D_MODEL
D_FF = ((8 * D_MODEL // 3 + 127) // 128) * 128
FFN_RESIDUAL = 0.5
RMS_EPS = 1e-6
LN_EPS = 1e-5
GN_EPS = 64e-5
PAGE_SIZE = 128

POOL_HALO = 16
CONV_HALO = 32
SHIFT_HALO = 8
SAMPLE_ROWS = 8
ATTN_BLOCK = 256
SB_GROUP = 4
DECODE_PAGES = 16
FFN_ROW_TILE = 4096
WKV_CHUNK = 64
WKV_CHUNKS_PER_STEP = 2


def _call(kernel, l, grid, in_specs, out_specs, out_shape, args, scratch=(), dims=None, prefetch=()):
    n_pre = 1 + len(prefetch)

    def body(*refs):
        kernel(*refs[n_pre:])

    grid_spec = pltpu.PrefetchScalarGridSpec(
        num_scalar_prefetch=n_pre, grid=grid, in_specs=in_specs, out_specs=out_specs,
        scratch_shapes=list(scratch))
    fn = kernel.func if isinstance(kernel, functools.partial) else kernel
    return pl.pallas_call(
        body, grid_spec=grid_spec, out_shape=out_shape, name=fn.__name__.strip("_"),
        compiler_params=pltpu.CompilerParams(
            dimension_semantics=dims or ("arbitrary",) * len(grid),
            vmem_limit_bytes=VMEM_LIMIT_BYTES),
    )(l, *prefetch, *args)


def _sigmoid(x):
    return 1.0 / (1.0 + jnp.exp(-x))


def _softplus(x):
    return jnp.maximum(x, 0.0) + jnp.log1p(jnp.exp(-jnp.abs(x)))


def _split_bf16(x):
    hi = x.astype(BF16)
    lo = (x - hi.astype(F32)).astype(BF16)
    return hi, lo


def _dot(a, b):
    return jnp.dot(a.astype(BF16), b.astype(BF16), preferred_element_type=F32)


def _dot_nt(a, b):
    return lax.dot_general(a.astype(BF16), b.astype(BF16), (((1,), (1,)), ((), ())),
                           preferred_element_type=F32)


def _dot_tn(a, b):
    return lax.dot_general(a.astype(BF16), b.astype(BF16), (((0,), (0,)), ((), ())),
                           preferred_element_type=F32)


def _rmsnorm_kernel(x_ref, g_ref, o_ref):
    x = x_ref[...]
    y = x * lax.rsqrt(jnp.mean(x * x, axis=-1, keepdims=True) + RMS_EPS)
    o_ref[...] = (y * g_ref[...]).astype(o_ref.dtype)


def rmsnorm(l, x, g, out_dtype):
    m, d = x.shape
    tm = min(512, m)
    return _call(
        _rmsnorm_kernel, l, (m // tm,),
        [pl.BlockSpec((tm, d), lambda i, l: (i, 0)),
         pl.BlockSpec((None, 1, d), lambda i, l: (l[0], 0, 0))],
        pl.BlockSpec((tm, d), lambda i, l: (i, 0)),
        jax.ShapeDtypeStruct((m, d), out_dtype), (x, g), dims=("parallel",))


def _side_spec(rows, cols, col_block):
    if col_block:
        return pl.BlockSpec((rows, cols), lambda i, j, l: (0, j))
    return pl.BlockSpec((rows, cols), lambda i, j, l: (0, 0))


def _side_out(rows, cols):
    return pl.BlockSpec((None, rows, cols), lambda i, j, l: (jnp.minimum(i, 1), 0, j))


def _side_slabs(row_tiles):
    return min(2, row_tiles)


def _on_first_row_tile(out_ref, compute):
    @pl.when(pl.program_id(0) == 0)
    def _():
        out_ref[...] = compute().astype(out_ref.dtype)

    @pl.when(pl.program_id(0) != 0)
    def _():
        out_ref[...] = jnp.zeros_like(out_ref)


def _ffn_in_kernel(x_ref, wg_ref, wu_ref, xs_ref, o_ref, os_ref):
    w = jnp.concatenate([wg_ref[...], wu_ref[...]], axis=1).astype(BF16)

    def act(x):
        r = jnp.dot(x, w, preferred_element_type=F32)
        g = r[:, :LANES]
        return g * _sigmoid(g) * r[:, LANES:]

    o_ref[...] = act(x_ref[...]).astype(o_ref.dtype)
    _on_first_row_tile(os_ref, lambda: act(xs_ref[...]))


def ffn_in(l, x, xs, w):
    m, d = x.shape
    ms = xs.shape[0]
    d_ff = w.shape[2] // 2
    nb = d_ff // LANES
    tm = min(FFN_ROW_TILE, m)
    out, out_s = _call(
        _ffn_in_kernel, l, (m // tm, nb),
        [pl.BlockSpec((tm, d), lambda i, j, l: (i, 0), pipeline_mode=pl.Buffered(1)),
         pl.BlockSpec((None, d, LANES), lambda i, j, l: (l[0], 0, j)),
         pl.BlockSpec((None, d, LANES), lambda i, j, l: (l[0], 0, j + nb)),
         _side_spec(ms, d, False)],
        [pl.BlockSpec((tm, LANES), lambda i, j, l: (i, j)), _side_out(ms, LANES)],
        [jax.ShapeDtypeStruct((m, d_ff), BF16), jax.ShapeDtypeStruct((_side_slabs(m // tm), ms, d_ff), BF16)],
        (x, w, w, xs), dims=("arbitrary", "arbitrary"))
    return out, out_s[0]


def _mm_kernel(a_ref, w_ref, as_ref, o_ref, os_ref):
    w = w_ref[...].astype(BF16)
    o_ref[...] = jnp.dot(a_ref[...], w, preferred_element_type=F32)
    _on_first_row_tile(os_ref, lambda: jnp.dot(as_ref[...], w, preferred_element_type=F32))


def _mm_res_kernel(a_ref, w_ref, r_ref, as_ref, rs_ref, o_ref, os_ref, *, scale):
    w = w_ref[...].astype(BF16)
    o_ref[...] = r_ref[...] + scale * jnp.dot(a_ref[...], w, preferred_element_type=F32)
    _on_first_row_tile(
        os_ref, lambda: rs_ref[...] + scale * jnp.dot(as_ref[...], w, preferred_element_type=F32))


def matmul(l, a, a_s, w, *, tn, res=None, res_s=None, scale=1.0, n=None, tm=1024):
    m, k = a.shape
    ms = a_s.shape[0]
    n = w.shape[2] if n is None else n
    tm = min(tm, m)
    a_spec = pl.BlockSpec((tm, k), lambda i, j, l: (i, 0))
    w_spec = pl.BlockSpec((None, k, tn), lambda i, j, l: (l[0], 0, j))
    o_spec = pl.BlockSpec((tm, tn), lambda i, j, l: (i, j))
    out_specs = [o_spec, _side_out(ms, tn)]
    out_shape = [jax.ShapeDtypeStruct((m, n), F32), jax.ShapeDtypeStruct((_side_slabs(m // tm), ms, n), F32)]
    dims = ("arbitrary", "arbitrary")
    if res is None:
        out, out_s = _call(_mm_kernel, l, (m // tm, n // tn), [a_spec, w_spec, _side_spec(ms, k, False)],
                           out_specs, out_shape, (a, w, a_s), dims=dims)
    else:
        out, out_s = _call(
            functools.partial(_mm_res_kernel, scale=scale), l, (m // tm, n // tn),
            [a_spec, w_spec, o_spec, _side_spec(ms, k, False), _side_spec(ms, tn, True)],
            out_specs, out_shape, (a, w, res, a_s, res_s), dims=dims)
    return out, out_s[0]


def _merge_kernel(h_ref, ya_ref, yb_ref, yc_ref, yd_ref, g0_ref, g1_ref, g2_ref, g3_ref, wb_ref,
                  hs_ref, sa_ref, sb_ref, sc_ref, sd_ref, o_ref, os_ref):
    wg = [g_ref[...].astype(BF16) for g_ref in (g0_ref, g1_ref, g2_ref, g3_ref)]
    wb = [wb_ref[n].astype(BF16) for n in range(N_BRANCH)]

    def gated_sum(h, ys):
        acc = None
        for n in range(N_BRANCH):
            gate = jnp.dot(h, wg[n], preferred_element_type=F32)
            t = _sigmoid(gate) * jnp.dot(ys[n], wb[n], preferred_element_type=F32)
            acc = t if acc is None else acc + t
        return acc

    o_ref[...] = gated_sum(h_ref[...], [r[...] for r in (ya_ref, yb_ref, yc_ref, yd_ref)]).astype(o_ref.dtype)
    _on_first_row_tile(os_ref, lambda: gated_sum(hs_ref[...], [r[...] for r in (sa_ref, sb_ref, sc_ref, sd_ref)]))


def merge(l, h, ys, h_s, ys_s, w_in, w_branch):
    m = h.shape[0]
    ms = h_s.shape[0]
    tn = 256
    tm = min(1024, m)
    nj = D_MODEL // tn
    y_spec = pl.BlockSpec((tm, W_MIX), lambda i, j, l: (i, 0))
    g_specs = [pl.BlockSpec((None, D_MODEL, tn), functools.partial(
        lambda i, j, l, n: (l[0], 0, OFF_G // tn + n * nj + j), n=n)) for n in range(N_BRANCH)]
    out, out_s = _call(
        _merge_kernel, l, (m // tm, nj),
        [pl.BlockSpec((tm, D_MODEL), lambda i, j, l: (i, 0))] + [y_spec] * N_BRANCH + g_specs
        + [pl.BlockSpec((None, N_BRANCH, W_MIX, tn), lambda i, j, l: (l[0], 0, 0, j))]
        + [_side_spec(ms, D_MODEL, False)] + [_side_spec(ms, W_MIX, False)] * N_BRANCH,
        [pl.BlockSpec((tm, tn), lambda i, j, l: (i, j)), _side_out(ms, tn)],
        [jax.ShapeDtypeStruct((m, D_MODEL), BF16),
         jax.ShapeDtypeStruct((_side_slabs(m // tm), ms, D_MODEL), BF16)],
        (h, *ys, w_in, w_in, w_in, w_in, w_branch, h_s, *ys_s), dims=("arbitrary", "arbitrary"))
    return out, out_s[0]


def _pool_kernel(u_ref, t0_ref, wp_ref, ps_ref, y_ref, ext_ref, *, tt, pos0):
    i = pl.program_id(1)

    @pl.when(i == 0)
    def _():
        ext_ref[0:POOL_HALO, :] = t0_ref[...]

    @pl.when(i > 0)
    def _():
        ext_ref[0:POOL_HALO, :] = ext_ref[tt:tt + POOL_HALO, :]

    u = u_ref[...]
    ext_ref[POOL_HALO:POOL_HALO + tt, :] = u
    pos = pos0 + i * tt + lax.broadcasted_iota(jnp.int32, (tt, 1), 0)
    ys = []
    for gi, win in enumerate(POOL_WINDOWS):
        cs = slice(gi * POOL_GROUP, (gi + 1) * POOL_GROUP)
        wsum = u[:, cs]
        for k in range(1, win):
            wsum = wsum + ext_ref[POOL_HALO - k:POOL_HALO - k + tt, cs]
        count = jnp.minimum(win, pos + 1).astype(F32)
        d = wsum / count - u[:, cs]
        ys.append(_dot(d, wp_ref[gi]))
    y_ref[...] = (jnp.concatenate(ys, axis=1) * ps_ref[...]).astype(y_ref.dtype)


def pool_mixer(l, proj3, tail0, w_pool, pool_scale, pos0):
    b, t, _ = proj3.shape
    tt = min(512, t)
    return _call(
        functools.partial(_pool_kernel, tt=tt, pos0=pos0), l, (b, t // tt),
        [pl.BlockSpec((None, tt, W_MIX), lambda b, i, l: (b, i, OFF_A // W_MIX)),
         pl.BlockSpec((None, POOL_HALO, W_MIX), lambda b, i, l: (b, 0, 0)),
         pl.BlockSpec((None, len(POOL_WINDOWS), POOL_GROUP, POOL_GROUP), lambda b, i, l: (l[0], 0, 0, 0)),
         pl.BlockSpec((None, 1, W_MIX), lambda b, i, l: (l[0], 0, 0))],
        pl.BlockSpec((None, tt, W_MIX), lambda b, i, l: (b, i, 0)),
        jax.ShapeDtypeStruct((b, t, W_MIX), BF16), (proj3, tail0, w_pool, pool_scale),
        scratch=[pltpu.VMEM((POOL_HALO + tt, W_MIX), F32)], dims=("parallel", "arbitrary"))


def _conv_kernel(val_ref, gate_ref, t0_ref, w_ref, cb_ref, lg_ref, lb_ref, y_ref, u_ref, ext_ref, sh_ref,
                 *, tt):
    i = pl.program_id(1)

    @pl.when(i == 0)
    def _():
        ext_ref[0:CONV_HALO, :] = t0_ref[...]

    @pl.when(i > 0)
    def _():
        ext_ref[0:CONV_HALO, :] = ext_ref[tt:tt + CONV_HALO, :]

    u = val_ref[...] * _sigmoid(gate_ref[...])
    u_ref[...] = u
    ext_ref[CONV_HALO:CONV_HALO + tt, :] = u
    first = CONV_HALO - CONV_BUF
    last = first + CONV_WIDTH - 1
    for b in range(SUBLANES):
        top = max(s for s in range(first, last + 1) if s % SUBLANES == b)
        sh_ref[b, 0:top - b + tt, :] = ext_ref[b:top + tt, :]
    acc = cb_ref[...]
    for j in range(CONV_WIDTH):
        s = first + j
        base = s - s % SUBLANES
        acc = acc + sh_ref[s % SUBLANES, base:base + tt, :] * w_ref[j:j + 1, :]
    y = acc
    mu = jnp.mean(y, axis=-1, keepdims=True)
    yc = y - mu
    var = jnp.mean(yc * yc, axis=-1, keepdims=True)
    yn = yc * lax.rsqrt(var + LN_EPS) * lg_ref[...] + lb_ref[...]
    y_ref[...] = (yn * _sigmoid(yn)).astype(y_ref.dtype)


def conv_mixer(l, proj3, tail0, conv_w, conv_b, ln_g, ln_b):
    b, t, _ = proj3.shape
    tt = min(256, t)
    vec = pl.BlockSpec((None, 1, W_MIX), lambda b, i, l: (l[0], 0, 0))
    blk = pl.BlockSpec((None, tt, W_MIX), lambda b, i, l: (b, i, 0))
    return _call(
        functools.partial(_conv_kernel, tt=tt), l, (b, t // tt),
        [pl.BlockSpec((None, tt, W_MIX), lambda b, i, l: (b, i, OFF_B // W_MIX)),
         pl.BlockSpec((None, tt, W_MIX), lambda b, i, l: (b, i, OFF_B // W_MIX + 1)),
         pl.BlockSpec((None, CONV_HALO, W_MIX), lambda b, i, l: (b, 0, 0)),
         pl.BlockSpec((None, CONV_WIDTH, W_MIX), lambda b, i, l: (l[0], 0, 0)),
         vec, vec, vec],
        [blk, blk],
        [jax.ShapeDtypeStruct((b, t, W_MIX), BF16), jax.ShapeDtypeStruct((b, t, W_MIX), F32)],
        (proj3, proj3, tail0, conv_w, conv_b, ln_g, ln_b),
        scratch=[pltpu.VMEM((CONV_HALO + tt, W_MIX), F32),
                 pltpu.VMEM((SUBLANES, CONV_HALO + tt, W_MIX), F32)], dims=("parallel", "arbitrary"))


def _sb_heads(x):
    n = x.shape[1] // SB_HEAD_DIM
    return jnp.stack([x[:, h * SB_HEAD_DIM:(h + 1) * SB_HEAD_DIM] for h in range(n)], axis=0)


def _sb_block(q3, k3, v3, bias, carry, acc, neg_upper, mask):
    nh, tq, _ = q3.shape
    tk = k3.shape[1]
    z = _bdot_nt(q3, k3)
    z = jnp.stack([z[h] + bias[h] for h in range(nh)], axis=0)
    fail = jnp.maximum(z, 0.0) + jnp.log(1.0 + jnp.exp2(jnp.abs(z) * (-LOG2E)))
    if mask is not None:
        fail = jnp.where(mask, fail, 0.0)
    later = jnp.dot(fail.astype(BF16).reshape(nh * tq, tk), neg_upper,
                    preferred_element_type=F32).reshape(nh, tq, tk)
    wts = jnp.exp(z - fail + later + carry)
    if mask is not None:
        wts = jnp.where(mask, wts, 0.0)
    acc = acc + _bdot(wts, v3)
    carry = carry + later[:, :, 0:1] - fail[:, :, 0:1]
    return carry, acc


def _sb_prompt_kernel(bias_ref, q_ref, k_ref, v_ref, o_ref, kb_ref, vb_ref, *, blk, n_blk):
    qi = pl.program_id(2)
    grp = pl.program_id(1)

    @pl.when(qi == 0)
    def _():
        def fill(j, _):
            rows = pl.ds(pl.multiple_of(j * blk, blk), blk)
            kb_ref[:, rows, :] = _sb_heads(k_ref[rows, :].astype(BF16))
            vb_ref[:, rows, :] = _sb_heads(v_ref[rows, :].astype(BF16))
            return 0
        lax.fori_loop(0, n_blk, fill, 0)

    row = lax.broadcasted_iota(jnp.int32, (blk, blk), 0)
    col = lax.broadcasted_iota(jnp.int32, (blk, blk), 1)
    causal = row > col
    neg_upper = jnp.where(causal, -1.0, 0.0).astype(BF16)
    q3 = _sb_heads((q_ref[...] * SB_SCALE).astype(BF16))
    nh = q3.shape[0]
    bias = [bias_ref[0, grp * nh + h] for h in range(nh)]

    def kv(j):
        rows = pl.ds(pl.multiple_of(j * blk, blk), blk)
        return kb_ref[:, rows, :], vb_ref[:, rows, :]

    carry = jnp.zeros((nh, blk, 1), F32)
    acc = jnp.zeros((nh, blk, SB_HEAD_DIM), F32)
    carry, acc = _sb_block(q3, *kv(qi), bias, carry, acc, neg_upper, causal)

    def body(it, state):
        return _sb_block(q3, *kv(qi - 1 - it), bias, state[0], state[1], neg_upper, None)

    carry, acc = lax.fori_loop(0, qi, body, (carry, acc))
    o_ref[...] = jnp.concatenate([acc[h] for h in range(nh)], axis=1).astype(o_ref.dtype)


def sb_prompt(l, proj3, sb_bias):
    b, t, _ = proj3.shape
    blk = min(ATTN_BLOCK, t)
    wid = SB_GROUP * SB_HEAD_DIM
    ngrp = W_MIX // wid
    c0 = OFF_C // wid
    return _call(
        functools.partial(_sb_prompt_kernel, blk=blk, n_blk=t // blk), l, (b, ngrp, t // blk),
        [pl.BlockSpec((None, 1, SB_HEADS), lambda b, h, i, l: (l[0], 0, 0), memory_space=pltpu.SMEM),
         pl.BlockSpec((None, blk, wid), lambda b, h, i, l: (b, i, c0 + h)),
         pl.BlockSpec((None, t, wid), lambda b, h, i, l: (b, 0, c0 + ngrp + h)),
         pl.BlockSpec((None, t, wid), lambda b, h, i, l: (b, 0, c0 + 2 * ngrp + h))],
        pl.BlockSpec((None, blk, wid), lambda b, h, i, l: (b, i, h)),
        jax.ShapeDtypeStruct((b, t, W_MIX), BF16), (sb_bias, proj3, proj3, proj3),
        scratch=[pltpu.VMEM((SB_GROUP, t, SB_HEAD_DIM), BF16)] * 2,
        dims=("parallel", "parallel", "arbitrary"))


def _sb_decode_kernel(q_ref, bias_ref, *refs, n_steps, pages):
    k_refs, v_refs = refs[:pages], refs[pages:2 * pages]
    o_ref, carry_ref, acc_ref = refs[2 * pages:]
    p = pl.program_id(1)

    @pl.when(p == 0)
    def _():
        carry_ref[...] = jnp.zeros_like(carry_ref)
        acc_ref[...] = jnp.zeros_like(acc_ref)

    q = (q_ref[...] * SB_SCALE).astype(BF16)
    bias = bias_ref[...]
    z = jnp.concatenate([_dot(q, k_ref[...].reshape(W_MIX, PAGE_SIZE)) + bias for k_ref in k_refs], axis=0)
    log_fail = -_softplus(z)
    row = lax.broadcasted_iota(jnp.int32, (PAGE_SIZE, PAGE_SIZE), 0)
    col = lax.broadcasted_iota(jnp.int32, (PAGE_SIZE, PAGE_SIZE), 1)
    upper = (row > col).astype(BF16)
    later = _dot(log_fail, upper)
    total = later[:, 0:1] + log_fail[:, 0:1]
    arg = z + log_fail + later
    carry = carry_ref[...]
    acc = acc_ref[...]
    for i in reversed(range(pages)):
        rows = slice(i * SB_HEADS, (i + 1) * SB_HEADS)
        wts = jnp.exp(arg[rows] + carry)
        acc = acc + _dot_nt(wts, v_refs[i][...].reshape(W_MIX, PAGE_SIZE))
        carry = carry + total[rows]
    carry_ref[...] = carry
    acc_ref[...] = acc

    @pl.when(p == n_steps - 1)
    def _():
        head = lax.broadcasted_iota(jnp.int32, acc.shape, 0)
        col_head = lax.broadcasted_iota(jnp.int32, acc.shape, 1) // SB_HEAD_DIM
        o_ref[...] = jnp.sum(jnp.where(head == col_head, acc, 0.0), axis=0, keepdims=True)


def sb_decode(l, q_blocks, sb_bias_col, cache_kt, cache_vt, page_table):
    b = q_blocks.shape[0]
    n_pages = page_table.shape[1]
    pages = DECODE_PAGES if n_pages % DECODE_PAGES == 0 else 1
    n_steps = n_pages // pages

    def page(i):
        return pl.BlockSpec((None, None, SB_HEADS, SB_HEAD_DIM, PAGE_SIZE),
                            lambda b, p, l, pt: (l[0], pt[b, n_pages - pages * (p + 1) + i], 0, 0, 0))

    return _call(
        functools.partial(_sb_decode_kernel, n_steps=n_steps, pages=pages), l, (b, n_steps),
        [pl.BlockSpec((None, SB_HEADS, W_MIX), lambda b, p, l, pt: (b, 0, 0)),
         pl.BlockSpec((None, SB_HEADS, 1), lambda b, p, l, pt: (l[0], 0, 0))]
        + [page(i) for i in range(pages)] * 2,
        pl.BlockSpec((None, 1, W_MIX), lambda b, p, l, pt: (b, 0, 0)),
        jax.ShapeDtypeStruct((b, 1, W_MIX), F32),
        (q_blocks, sb_bias_col) + (cache_kt,) * pages + (cache_vt,) * pages,
        scratch=[pltpu.VMEM((SB_HEADS, 1), F32), pltpu.VMEM((SB_HEADS, W_MIX), F32)],
        dims=("parallel", "arbitrary"), prefetch=(page_table,))


def _wkv_inputs(pr_ref, pk_ref, pv_ref, pl_ref, sh0_ref, mu_ref, w0_ref, dup_ref, a0_ref, aup_ref,
                gup_ref, ext_ref, *, tt, t_valid):
    i = pl.program_id(1)

    @pl.when(i == 0)
    def _():
        ext_ref[0:SHIFT_HALO, :] = sh0_ref[...]

    @pl.when(i > 0)
    def _():
        ext_ref[0:SHIFT_HALO, :] = ext_ref[tt:tt + SHIFT_HALO, :]

    rows = slice(SHIFT_HALO, SHIFT_HALO + tt)
    ext_ref[rows, 0:W_MIX] = pr_ref[...]
    ext_ref[rows, W_MIX:2 * W_MIX] = pk_ref[...]
    ext_ref[rows, 2 * W_MIX:3 * W_MIX] = pv_ref[...]
    ext_ref[rows, 3 * W_MIX:N_D] = pl_ref[...]
    p = ext_ref[rows, :]
    prev = ext_ref[SHIFT_HALO - 1:SHIFT_HALO - 1 + tt, :]
    xs = p + (prev - p) * mu_ref[...]
    o = 3 * W_MIX
    w_dn = xs[:, o:o + DECAY_LORA]
    a_dn = xs[:, o + DECAY_LORA:o + DECAY_LORA + A_LORA]
    g_dn = xs[:, o + DECAY_LORA + A_LORA:N_D]
    w_log = -_softplus(-(w0_ref[...] + _dot(jnp.tanh(w_dn), dup_ref[...]))) - 0.5
    log_decay = -jnp.exp(w_log)
    alpha = _sigmoid(a0_ref[...] + _dot(a_dn, aup_ref[...]))
    k = xs[:, W_MIX:2 * W_MIX]
    v = xs[:, 2 * W_MIX:3 * W_MIX]
    if t_valid is not None:
        valid = (i * tt + lax.broadcasted_iota(jnp.int32, (tt, 1), 0)) < t_valid
        k = jnp.where(valid, k, 0.0)
        v = jnp.where(valid, v, 0.0)
        log_decay = jnp.where(valid, log_decay, 0.0)
    return xs[:, 0:W_MIX], k, v, log_decay, alpha, _dot(_sigmoid(g_dn), gup_ref[...])


def _bdot(a, b):
    return lax.dot_general(a.astype(BF16), b.astype(BF16), (((2,), (1,)), ((0,), (0,))),
                           preferred_element_type=F32)


def _bdot_nt(a, b):
    return lax.dot_general(a.astype(BF16), b.astype(BF16), (((2,), (2,)), ((0,), (0,))),
                           preferred_element_type=F32)


def _bdot_tn(a, b):
    return lax.dot_general(a.astype(BF16), b.astype(BF16), (((1,), (1,)), ((0,), (0,))),
                           preferred_element_type=F32)


def _heads(x):
    return jnp.stack([x[:, h * RWKV_HEAD_DIM:(h + 1) * RWKV_HEAD_DIM] for h in range(RWKV_HEADS)], axis=0)


def _head_sums(xs):
    r = lax.broadcasted_iota(jnp.int32, (W_MIX, W_MIX), 0) // RWKV_HEAD_DIM
    c = lax.broadcasted_iota(jnp.int32, (W_MIX, W_MIX), 1) // RWKV_HEAD_DIM
    same = (r == c).astype(BF16)
    rows = xs[0].shape[0]
    parts = [p for x in xs for p in _split_bf16(x)]
    out = jnp.dot(jnp.concatenate(parts, axis=0), same, preferred_element_type=F32)
    return [out[(2 * i) * rows:(2 * i + 1) * rows] + out[(2 * i + 1) * rows:(2 * i + 2) * rows]
            for i in range(len(xs))]


def _wkv_chunk(r, k, v, ld, al, g, s, kk_w, ka_w, rk_w, ln_g, ln_b):
    ch = r.shape[0]
    row = lax.broadcasted_iota(jnp.int32, (ch, ch), 0)
    col = lax.broadcasted_iota(jnp.int32, (ch, ch), 1)
    incl = row >= col
    eye = (row == col).astype(F32)
    row2 = lax.broadcasted_iota(jnp.int32, (2 * ch, 2 * ch), 0)
    col2 = lax.broadcasted_iota(jnp.int32, (2 * ch, 2 * ch), 1) & (ch - 1)
    tri = (row2 & (ch - 1)) + (row2 // ch) > col2
    hi, lo = _split_bf16(ld)
    lower = incl.astype(BF16)
    cum = (jnp.dot(lower, hi, preferred_element_type=F32)
           + jnp.dot(lower, lo, preferred_element_type=F32))
    cum_end = cum[ch - 1:ch, :]
    g_in = jnp.exp(cum)
    g_ex = jnp.exp(cum - ld)
    g_inv = jnp.exp(-cum)
    g_end = jnp.exp(cum_end - cum)
    g_tot = jnp.exp(cum_end)
    kk = k * kk_w
    k2 = k * (1.0 + (al - 1.0) * ka_w)
    kk_sq, rk_sum = _head_sums([kk * kk, r * k2 * rk_w])
    kkn = kk / jnp.maximum(jnp.sqrt(kk_sq), 1e-12)
    b = kkn * al
    a2 = _heads(jnp.concatenate([-kkn * g_ex, r * g_in], axis=0))
    bm = _heads(jnp.concatenate([b * g_inv, k2 * g_inv], axis=0))
    bk_end = _heads(jnp.concatenate([b * g_end, k2 * g_end], axis=0))
    v3 = _heads(v)
    prod = jnp.where(tri, _bdot_nt(a2, bm), 0.0)
    x = prod[:, 0:ch, 0:ch]
    inv = eye + x
    n_sq = ch.bit_length() - 2
    for _ in range(n_sq):
        x = _bdot(x, x)
        inv = inv + _bdot(inv, x)
    w_rhs = _bdot_nt(a2[:, 0:ch], s) + _bdot(prod[:, 0:ch], jnp.concatenate([jnp.zeros_like(v3), v3], axis=1))
    u = _bdot(inv, w_rhs)
    uv = jnp.concatenate([u, v3], axis=1)
    y3 = _bdot_nt(a2[:, ch:2 * ch], s) + _bdot(prod[:, ch:2 * ch], uv)
    s_new = s * _heads(g_tot) + _bdot_tn(uv, bk_end)
    mu = jnp.mean(y3, axis=-1, keepdims=True)
    yc = y3 - mu
    var = jnp.mean(yc * yc, axis=-1, keepdims=True)
    yn3 = yc * lax.rsqrt(var + GN_EPS)
    yn = jnp.concatenate([yn3[h] for h in range(RWKV_HEADS)], axis=1)
    return (yn * ln_g + ln_b + rk_sum * v) * g, s_new


def _wkv_kernel(pr_ref, pk_ref, pv_ref, pl_ref, sh0_ref, mu_ref, w0_ref, dup_ref, a0_ref, aup_ref, gup_ref,
                s0_ref, kk_ref, ka_ref, rk_ref, lg_ref, lb_ref, y_ref, so_ref, s_ref, ext_ref,
                *, ch, n_steps, t_valid):
    c = pl.program_id(1)

    @pl.when(c == 0)
    def _():
        s_ref[...] = s0_ref[...]

    tt = pr_ref.shape[0]
    r, k, v, ld, al, g = _wkv_inputs(pr_ref, pk_ref, pv_ref, pl_ref, sh0_ref, mu_ref, w0_ref, dup_ref,
                                     a0_ref, aup_ref, gup_ref, ext_ref, tt=tt, t_valid=t_valid)
    s = s_ref[...]
    for j in range(tt // ch):
        rows = slice(j * ch, (j + 1) * ch)
        y, s = _wkv_chunk(r[rows], k[rows], v[rows], ld[rows], al[rows], g[rows], s,
                          kk_ref[...], ka_ref[...], rk_ref[...], lg_ref[...], lb_ref[...])
        y_ref[rows, :] = y.astype(y_ref.dtype)
    s_ref[...] = s

    @pl.when(c == n_steps - 1)
    def _():
        so_ref[...] = s


def wkv_mixer(l, proj3, shift0, s0, lw, t_valid):
    b, t, _ = proj3.shape
    ch = min(WKV_CHUNK, t)
    rows = ch * WKV_CHUNKS_PER_STEP if t % (ch * WKV_CHUNKS_PER_STEP) == 0 else ch
    cd = OFF_D // W_MIX
    col = lambda j: pl.BlockSpec((None, rows, W_MIX), lambda b, c, l: (b, c, cd + j))
    st = pl.BlockSpec((None, RWKV_HEADS, RWKV_HEAD_DIM, RWKV_HEAD_DIM), lambda b, c, l: (b, 0, 0, 0))
    vec = lambda n: pl.BlockSpec((None, 1, n), lambda b, c, l: (l[0], 0, 0))
    up = lambda n: pl.BlockSpec((None, n, W_MIX), lambda b, c, l: (l[0], 0, 0))
    return _call(
        functools.partial(_wkv_kernel, ch=ch, n_steps=t // rows, t_valid=t_valid), l, (b, t // rows),
        [col(0), col(1), col(2),
         pl.BlockSpec((None, rows, N_LORA), lambda b, c, l: (b, c, (OFF_D + 3 * W_MIX) // N_LORA)),
         pl.BlockSpec((None, SHIFT_HALO, N_D), lambda b, c, l: (b, 0, 0)),
         vec(N_D), vec(W_MIX), up(DECAY_LORA), vec(W_MIX), up(A_LORA), up(GATE_LORA), st]
        + [vec(W_MIX)] * 5,
        [pl.BlockSpec((None, rows, W_MIX), lambda b, c, l: (b, c, 0)), st],
        [jax.ShapeDtypeStruct((b, t, W_MIX), BF16),
         jax.ShapeDtypeStruct((b, RWKV_HEADS, RWKV_HEAD_DIM, RWKV_HEAD_DIM), F32)],
        (proj3, proj3, proj3, proj3, shift0, lw['shift_mu'], lw['decay_w0'], lw['decay_up'], lw['a0'],
         lw['a_up'], lw['g_up'], s0, lw['k_k'], lw['k_a'], lw['r_k'], lw['lnx_g'], lw['lnx_b']),
        scratch=[pltpu.VMEM((RWKV_HEADS, RWKV_HEAD_DIM, RWKV_HEAD_DIM), F32),
                 pltpu.VMEM((SHIFT_HALO + rows, N_D), F32)],
        dims=("parallel", "arbitrary"))


def _split_heads_kernel(k_ref, v_ref, ko_ref, vo_ref):
    ko_ref[...] = k_ref[...].T.reshape(ko_ref.shape)
    vo_ref[...] = v_ref[...].T.reshape(vo_ref.shape)


def split_heads(l, proj3):
    b, t, _ = proj3.shape
    tt = min(512, t)
    col = lambda j: pl.BlockSpec((None, tt, W_MIX), lambda b, i, l: (b, i, OFF_C // W_MIX + j))
    out = pl.BlockSpec((None, SB_HEADS, SB_HEAD_DIM, tt), lambda b, i, l: (b, 0, 0, i))
    shape = jax.ShapeDtypeStruct((b, SB_HEADS, SB_HEAD_DIM, t), F32)
    return _call(_split_heads_kernel, l, (b, t // tt), [col(1), col(2)], [out, out], [shape, shape],
                 (proj3, proj3), dims=("parallel", "parallel"))


def _mixers(l, proj3, pool0, conv0, shift0, wkv0, pos0, t_valid, attend, lw):
    b, t, _ = proj3.shape
    y_a = pool_mixer(l, proj3, pool0, lw['w_pool'], lw['pool_scale'], pos0)
    y_b, glu = conv_mixer(l, proj3, conv0, lw['conv_w'], lw['conv_b'], lw['ln_g'], lw['ln_b'])
    y_c = attend(l, proj3)
    y_d, wkv_new = wkv_mixer(l, proj3, shift0, wkv0, lw, t_valid)
    return [y.reshape(b * t, W_MIX) for y in (y_a, y_b, y_c, y_d)], glu, wkv_new


def _trunk_layer(l, xp, xs, mix_p, mix_s, lw):
    dp, ds = xp.shape, xs.shape
    xp, xs = xp.reshape(-1, dp[2]), xs.reshape(-1, ds[2])

    def ffn(xp, xs, g, w_a, w_b):
        act_p, act_s = ffn_in(l, rmsnorm(l, xp, g, BF16), rmsnorm(l, xs, g, BF16), w_a)
        return matmul(l, act_p, act_s, w_b, tn=256, res=xp, res_s=xs, scale=FFN_RESIDUAL)

    xp, xs = ffn(xp, xs, lw['g_ffn1'], lw['w_ffn1_in'], lw['w_ffn1_out'])
    hp, hs = rmsnorm(l, xp, lw['g_mix'], BF16), rmsnorm(l, xs, lw['g_mix'], BF16)
    proj_p, proj_s = matmul(l, hp, hs, lw['w_in'], tn=256, n=OFF_G, tm=2048)
    proj_p, proj_s = proj_p.reshape(dp[0], dp[1], OFF_G), proj_s.reshape(ds[0], ds[1], OFF_G)
    ys_p, glu_p, wkv_p = mix_p(l, proj_p)
    ys_s, glu_s, wkv_s = mix_s(l, proj_s)
    merged_p, merged_s = merge(l, hp, ys_p, hs, ys_s, lw['w_in'], lw['w_branch'])
    xp, xs = matmul(l, merged_p, merged_s, lw['w_out'], tn=256, res=xp, res_s=xs, scale=1.0, tm=2048)
    xp, xs = ffn(xp, xs, lw['g_ffn2'], lw['w_ffn2_in'], lw['w_ffn2_out'])
    return xp.reshape(dp), xs.reshape(ds), (proj_p, glu_p, wkv_p), (proj_s, glu_s, wkv_s)


def _front_pad(rows, height):
    return jnp.pad(rows, ((0, 0), (height - rows.shape[1], 0), (0, 0)))


def kernel(x_prompt, x_sample, cache_k, cache_v, state_pool, state_conv, state_shift, state_wkv,
           page_table, g_ffn1, w_ffn1_in, w_ffn1_out, g_mix, w_in, w_pool, pool_scale, conv_w,
           conv_b, ln_g, ln_b, sb_bias, shift_mu, decay_w0, decay_up, a0, a_up, g_up, k_k, k_a, r_k,
           lnx_g, lnx_b, w_branch, w_out, g_ffn2, w_ffn2_in, w_ffn2_out, g_final):
    depth = w_in.shape[0]
    bp, tp, d = x_prompt.shape
    bs, ts, _ = x_sample.shape
    assert d == D_MODEL and ts == 1
    past_len = page_table.shape[1] * PAGE_SIZE
    row = lambda a: a.reshape(depth, 1, -1)
    lw = {
        'g_ffn1': row(g_ffn1), 'w_ffn1_in': w_ffn1_in, 'w_ffn1_out': w_ffn1_out, 'g_mix': row(g_mix),
        'w_in': w_in, 'w_pool': w_pool, 'pool_scale': row(pool_scale), 'conv_w': conv_w,
        'conv_b': row(conv_b), 'ln_g': row(ln_g), 'ln_b': row(ln_b), 'sb_bias': row(sb_bias),
        'shift_mu': row(shift_mu), 'decay_w0': row(decay_w0), 'decay_up': decay_up, 'a0': row(a0),
        'a_up': a_up, 'g_up': g_up, 'k_k': row(k_k), 'k_a': row(k_a), 'r_k': row(r_k),
        'lnx_g': row(lnx_g), 'lnx_b': row(lnx_b), 'w_branch': w_branch, 'w_out': w_out,
        'g_ffn2': row(g_ffn2), 'w_ffn2_in': w_ffn2_in, 'w_ffn2_out': w_ffn2_out,
    }
    sb_bias_col = sb_bias.reshape(depth, SB_HEADS, 1)
    cache_kt = jnp.transpose(cache_k, (0, 1, 3, 4, 2))
    cache_vt = jnp.transpose(cache_v, (0, 1, 3, 4, 2))
    own_head = (jnp.arange(W_MIX)[None, :] // SB_HEAD_DIM) == jnp.arange(SB_HEADS)[:, None]
    xs0 = jnp.pad(x_sample, ((0, 0), (0, SAMPLE_ROWS - ts), (0, 0)))

    def layer(carry, li):
        xp, xs = carry
        l = li.reshape(1)
        pool_s = lax.dynamic_index_in_dim(state_pool, li, 0, keepdims=False)
        conv_s = lax.dynamic_index_in_dim(state_conv, li, 0, keepdims=False)
        shift_s = lax.dynamic_index_in_dim(state_shift, li, 0, keepdims=False)
        wkv_s = lax.dynamic_index_in_dim(state_wkv, li, 0, keepdims=False)

        def attend_sample(l, p3):
            q_blocks = jnp.where(own_head, p3[:, 0:1, OFF_C:OFF_C + W_MIX], 0.0)
            y = sb_decode(l, q_blocks, sb_bias_col, cache_kt, cache_vt, page_table)
            return jnp.pad(y, ((0, 0), (0, SAMPLE_ROWS - 1), (0, 0))).astype(BF16)

        mix_p = lambda l, p3: _mixers(
            l, p3, jnp.zeros((bp, POOL_HALO, W_MIX), F32), jnp.zeros((bp, CONV_HALO, W_MIX), F32),
            jnp.zeros((bp, SHIFT_HALO, N_D), F32),
            jnp.zeros((bp, RWKV_HEADS, RWKV_HEAD_DIM, RWKV_HEAD_DIM), F32), 0, None,
            lambda l, p3: sb_prompt(l, p3, lw['sb_bias']), lw)
        mix_s = lambda l, p3: _mixers(
            l, p3, _front_pad(pool_s, POOL_HALO), _front_pad(conv_s, CONV_HALO),
            _front_pad(shift_s, SHIFT_HALO), wkv_s, past_len, ts, attend_sample, lw)
        xp, xs, (proj_p, glu_p, wkv_p), (proj_s, glu_s, wkv_s_new) = _trunk_layer(l, xp, xs, mix_p, mix_s, lw)

        k_p, v_p = split_heads(l, proj_p)
        k_s, v_s = (proj_s[:, :ts, OFF_C + j * W_MIX:OFF_C + (j + 1) * W_MIX].reshape(
            bs, ts, SB_HEADS, SB_HEAD_DIM) for j in (1, 2))
        outs = (
            k_p, v_p, proj_p[:, tp - POOL_BUF:, OFF_A:OFF_A + W_MIX], glu_p[:, tp - CONV_BUF:],
            proj_p[:, tp - 1:, OFF_D:OFF_D + N_D], wkv_p,
            k_s, v_s,
            jnp.concatenate([pool_s, proj_s[:, :ts, OFF_A:OFF_A + W_MIX]], axis=1)[:, -POOL_BUF:],
            jnp.concatenate([conv_s, glu_s[:, :ts]], axis=1)[:, -CONV_BUF:],
            proj_s[:, ts - 1:ts, OFF_D:OFF_D + N_D], wkv_s_new)
        return (xp, xs), outs

    (xp, xs), st = lax.scan(layer, (x_prompt, xs0), jnp.arange(depth, dtype=jnp.int32))
    zero = jnp.zeros((1,), jnp.int32)
    g_fin = g_final.reshape(1, 1, d)
    y_prompt = rmsnorm(zero, xp.reshape(bp * tp, d), g_fin, F32).reshape(bp, tp, d)
    y_sample = rmsnorm(zero, xs.reshape(bs * SAMPLE_ROWS, d), g_fin, F32).reshape(bs, SAMPLE_ROWS, d)[:, :ts]
    to_cache_layout = lambda a: jnp.transpose(a, (0, 1, 4, 2, 3))
    return (y_prompt, y_sample, to_cache_layout(st[0]), to_cache_layout(st[1])) + tuple(st[2:])
```

```python
import functools

import jax
import jax.numpy as jnp
from jax import lax
from jax.experimental import pallas as pl
from jax.experimental.pallas import tpu as pltpu

F32 = jnp.float32
BF16 = jnp.bfloat16
LOG2E = 1.4426950408889634

LANES = 128
SUBLANES = 8
VMEM_LIMIT_BYTES = 56 * 1024 * 1024

D_MODEL = 2048
W_MIX = D_MODEL // 4
POOL_WINDOWS = (2, 4, 8, 16)
POOL_GROUP = W_MIX // len(POOL_WINDOWS)
POOL_BUF = max(POOL_WINDOWS) - 1
CONV_WIDTH = 31
CONV_BUF = CONV_WIDTH - 1
SB_HEADS = 8
SB_HEAD_DIM = W_MIX // SB_HEADS
SB_SCALE = SB_HEAD_DIM ** -0.5
RWKV_HEAD_DIM = 64
RWKV_HEADS = W_MIX // RWKV_HEAD_DIM
DECAY_LORA = 64
A_LORA = 64
GATE_LORA = 128
N_LORA = DECAY_LORA + A_LORA + GATE_LORA
N_D = 3 * W_MIX + N_LORA
N_BRANCH = 4
OFF_A = 0
OFF_B = OFF_A + W_MIX
OFF_C = OFF_B + 2 * W_MIX
OFF_D = OFF_C + 3 * W_MIX
OFF_G = OFF_D + N_D
N_IN = OFF_G + N_BRANCH * D_MODEL
D_FF = ((8 * D_MODEL // 3 + 127) // 128) * 128
FFN_RESIDUAL = 0.5
RMS_EPS = 1e-6
LN_EPS = 1e-5
GN_EPS = 64e-5
PAGE_SIZE = 128

POOL_HALO = 16
CONV_HALO = 32
SHIFT_HALO = 8
SAMPLE_ROWS = 8
ATTN_BLOCK = 256
SB_GROUP = 4
SB_UNROLL = 4
DECODE_PAGES = 16
FFN_ROW_TILE = 4096
WKV_CHUNK = 64
WKV_CHUNKS_PER_STEP = 2


def _call(kernel, l, grid, in_specs, out_specs, out_shape, args, scratch=(), dims=None, prefetch=()):
    n_pre = 1 + len(prefetch)

    def body(*refs):
        kernel(*refs[n_pre:])

    grid_spec = pltpu.PrefetchScalarGridSpec(
        num_scalar_prefetch=n_pre, grid=grid, in_specs=in_specs, out_specs=out_specs,
        scratch_shapes=list(scratch))
    fn = kernel.func if isinstance(kernel, functools.partial) else kernel
    return pl.pallas_call(
        body, grid_spec=grid_spec, out_shape=out_shape, name=fn.__name__.strip("_"),
        compiler_params=pltpu.CompilerParams(
            dimension_semantics=dims or ("arbitrary",) * len(grid),
            vmem_limit_bytes=VMEM_LIMIT_BYTES),
    )(l, *prefetch, *args)


def _sigmoid(x):
    return 1.0 / (1.0 + jnp.exp(-x))


def _softplus(x):
    return jnp.maximum(x, 0.0) + jnp.log1p(jnp.exp(-jnp.abs(x)))


def _split_bf16(x):
    hi = x.astype(BF16)
    lo = (x - hi.astype(F32)).astype(BF16)
    return hi, lo


def _dot(a, b):
    return jnp.dot(a.astype(BF16), b.astype(BF16), preferred_element_type=F32)


def _dot_nt(a, b):
    return lax.dot_general(a.astype(BF16), b.astype(BF16), (((1,), (1,)), ((), ())),
                           preferred_element_type=F32)


def _dot_tn(a, b):
    return lax.dot_general(a.astype(BF16), b.astype(BF16), (((0,), (0,)), ((), ())),
                           preferred_element_type=F32)


def _rmsnorm_kernel(x_ref, g_ref, o_ref):
    x = x_ref[...]
    y = x * lax.rsqrt(jnp.mean(x * x, axis=-1, keepdims=True) + RMS_EPS)
    o_ref[...] = (y * g_ref[...]).astype(o_ref.dtype)


def rmsnorm(l, x, g, out_dtype):
    m, d = x.shape
    tm = min(512, m)
    return _call(
        _rmsnorm_kernel, l, (m // tm,),
        [pl.BlockSpec((tm, d), lambda i, l: (i, 0)),
         pl.BlockSpec((None, 1, d), lambda i, l: (l[0], 0, 0))],
        pl.BlockSpec((tm, d), lambda i, l: (i, 0)),
        jax.ShapeDtypeStruct((m, d), out_dtype), (x, g), dims=("parallel",))


def _side_spec(rows, cols, col_block):
    if col_block:
        return pl.BlockSpec((rows, cols), lambda i, j, l: (0, j))
    return pl.BlockSpec((rows, cols), lambda i, j, l: (0, 0))


def _side_out(rows, cols):
    return pl.BlockSpec((None, rows, cols), lambda i, j, l: (i, 0, j))


def _side_slabs(row_tiles):
    return row_tiles


def _on_first_row_tile(out_ref, compute):
    @pl.when(pl.program_id(0) == 0)
    def _():
        out_ref[...] = compute().astype(out_ref.dtype)

    @pl.when(pl.program_id(0) != 0)
    def _():
        out_ref[...] = jnp.zeros_like(out_ref)


def _ffn_in_kernel(x_ref, wg_ref, wu_ref, xs_ref, o_ref, os_ref):
    w = jnp.concatenate([wg_ref[...], wu_ref[...]], axis=1).astype(BF16)

    def act(x):
        r = jnp.dot(x, w, preferred_element_type=F32)
        g = r[:, :LANES]
        return g * _sigmoid(g) * r[:, LANES:]

    o_ref[...] = act(x_ref[...]).astype(o_ref.dtype)
    _on_first_row_tile(os_ref, lambda: act(xs_ref[...]))


def ffn_in(l, x, xs, w):
    m, d = x.shape
    ms = xs.shape[0]
    d_ff = w.shape[2] // 2
    nb = d_ff // LANES
    tm = min(FFN_ROW_TILE, m)
    out, out_s = _call(
        _ffn_in_kernel, l, (m // tm, nb),
        [pl.BlockSpec((tm, d), lambda i, j, l: (i, 0), pipeline_mode=pl.Buffered(1)),
         pl.BlockSpec((None, d, LANES), lambda i, j, l: (l[0], 0, j)),
         pl.BlockSpec((None, d, LANES), lambda i, j, l: (l[0], 0, j + nb)),
         _side_spec(ms, d, False)],
        [pl.BlockSpec((tm, LANES), lambda i, j, l: (i, j)), _side_out(ms, LANES)],
        [jax.ShapeDtypeStruct((m, d_ff), BF16), jax.ShapeDtypeStruct((_side_slabs(m // tm), ms, d_ff), BF16)],
        (x, w, w, xs), dims=("arbitrary", "arbitrary"))
    return out, out_s[0]


def _mm_kernel(a_ref, w_ref, as_ref, o_ref, os_ref):
    w = w_ref[...].astype(BF16)
    o_ref[...] = jnp.dot(a_ref[...], w, preferred_element_type=F32)
    _on_first_row_tile(os_ref, lambda: jnp.dot(as_ref[...], w, preferred_element_type=F32))


def _mm_res_kernel(a_ref, w_ref, r_ref, as_ref, rs_ref, o_ref, os_ref, *, scale):
    w = w_ref[...].astype(BF16)
    o_ref[...] = r_ref[...] + scale * jnp.dot(a_ref[...], w, preferred_element_type=F32)
    _on_first_row_tile(
        os_ref, lambda: rs_ref[...] + scale * jnp.dot(as_ref[...], w, preferred_element_type=F32))


def matmul(l, a, a_s, w, *, tn, res=None, res_s=None, scale=1.0, n=None, tm=1024):
    m, k = a.shape
    ms = a_s.shape[0]
    n = w.shape[2] if n is None else n
    tm = min(tm, m)
    a_spec = pl.BlockSpec((tm, k), lambda i, j, l: (i, 0))
    if w.shape[0] == 1:
        w_spec = pl.BlockSpec((None, k, tn), lambda i, j, l: (0, 0, j))
    else:
        w_spec = pl.BlockSpec((None, k, tn), lambda i, j, l: (l[0], 0, j))
    o_spec = pl.BlockSpec((tm, tn), lambda i, j, l: (i, j))
    out_specs = [o_spec, _side_out(ms, tn)]
    out_shape = [jax.ShapeDtypeStruct((m, n), F32), jax.ShapeDtypeStruct((_side_slabs(m // tm), ms, n), F32)]
    dims = ("arbitrary", "arbitrary")
    if res is None:
        out, out_s = _call(_mm_kernel, l, (m // tm, n // tn), [a_spec, w_spec, _side_spec(ms, k, False)],
                           out_specs, out_shape, (a, w, a_s), dims=dims)
    else:
        out, out_s = _call(
            functools.partial(_mm_res_kernel, scale=scale), l, (m // tm, n // tn),
            [a_spec, w_spec, o_spec, _side_spec(ms, k, False), _side_spec(ms, tn, True)],
            out_specs, out_shape, (a, w, res, a_s, res_s), dims=dims)
    return out, out_s[0]


def _merge_kernel(h_ref, ya_ref, yb_ref, yc_ref, yd_ref, g0_ref, g1_ref, g2_ref, g3_ref, wb_ref,
                  hs_ref, sa_ref, sb_ref, sc_ref, sd_ref, o_ref, os_ref):
    wg = [g_ref[...].astype(BF16) for g_ref in (g0_ref, g1_ref, g2_ref, g3_ref)]
    wb = [wb_ref[n].astype(BF16) for n in range(N_BRANCH)]

    def gated_sum(h, ys):
        acc = None
        for n in range(N_BRANCH):
            gate = jnp.dot(h, wg[n], preferred_element_type=F32)
            t = _sigmoid(gate) * jnp.dot(ys[n], wb[n], preferred_element_type=F32)
            acc = t if acc is None else acc + t
        return acc

    o_ref[...] = gated_sum(h_ref[...], [r[...] for r in (ya_ref, yb_ref, yc_ref, yd_ref)]).astype(o_ref.dtype)
    _on_first_row_tile(os_ref, lambda: gated_sum(hs_ref[...], [r[...] for r in (sa_ref, sb_ref, sc_ref, sd_ref)]))


def merge(l, h, ys, h_s, ys_s, w_in, w_branch):
    m = h.shape[0]
    ms = h_s.shape[0]
    tn = 256
    tm = min(1024, m)
    nj = D_MODEL // tn
    y_spec = pl.BlockSpec((tm, W_MIX), lambda i, j, l: (i, 0))
    g_specs = [pl.BlockSpec((None, D_MODEL, tn), functools.partial(
        lambda i, j, l, n: (l[0], 0, OFF_G // tn + n * nj + j), n=n)) for n in range(N_BRANCH)]
    out, out_s = _call(
        _merge_kernel, l, (m // tm, nj),
        [pl.BlockSpec((tm, D_MODEL), lambda i, j, l: (i, 0))] + [y_spec] * N_BRANCH + g_specs
        + [pl.BlockSpec((None, N_BRANCH, W_MIX, tn), lambda i, j, l: (l[0], 0, 0, j))]
        + [_side_spec(ms, D_MODEL, False)] + [_side_spec(ms, W_MIX, False)] * N_BRANCH,
        [pl.BlockSpec((tm, tn), lambda i, j, l: (i, j)), _side_out(ms, tn)],
        [jax.ShapeDtypeStruct((m, D_MODEL), BF16),
         jax.ShapeDtypeStruct((_side_slabs(m // tm), ms, D_MODEL), BF16)],
        (h, *ys, w_in, w_in, w_in, w_in, w_branch, h_s, *ys_s), dims=("arbitrary", "arbitrary"))
    return out, out_s[0]


def _pool_kernel(u_ref, t0_ref, wp_ref, ps_ref, y_ref, ext_ref, *, tt, pos0):
    i = pl.program_id(1)

    @pl.when(i == 0)
    def _():
        ext_ref[0:POOL_HALO, :] = t0_ref[...]

    @pl.when(i > 0)
    def _():
        ext_ref[0:POOL_HALO, :] = ext_ref[tt:tt + POOL_HALO, :]

    u = u_ref[...]
    ext_ref[POOL_HALO:POOL_HALO + tt, :] = u
    pos = pos0 + i * tt + lax.broadcasted_iota(jnp.int32, (tt, 1), 0)
    ys = []
    for gi, win in enumerate(POOL_WINDOWS):
        cs = slice(gi * POOL_GROUP, (gi + 1) * POOL_GROUP)
        wsum = u[:, cs]
        for k in range(1, win):
            wsum = wsum + ext_ref[POOL_HALO - k:POOL_HALO - k + tt, cs]
        count = jnp.minimum(win, pos + 1).astype(F32)
        d = wsum / count - u[:, cs]
        ys.append(_dot(d, wp_ref[gi]))
    y_ref[...] = (jnp.concatenate(ys, axis=1) * ps_ref[...]).astype(y_ref.dtype)


def pool_mixer(l, proj3, tail0, w_pool, pool_scale, pos0):
    b, t, _ = proj3.shape
    tt = min(512, t)
    return _call(
        functools.partial(_pool_kernel, tt=tt, pos0=pos0), l, (b, t // tt),
        [pl.BlockSpec((None, tt, W_MIX), lambda b, i, l: (b, i, OFF_A // W_MIX)),
         pl.BlockSpec((None, POOL_HALO, W_MIX), lambda b, i, l: (b, 0, 0)),
         pl.BlockSpec((None, len(POOL_WINDOWS), POOL_GROUP, POOL_GROUP), lambda b, i, l: (l[0], 0, 0, 0)),
         pl.BlockSpec((None, 1, W_MIX), lambda b, i, l: (l[0], 0, 0))],
        pl.BlockSpec((None, tt, W_MIX), lambda b, i, l: (b, i, 0)),
        jax.ShapeDtypeStruct((b, t, W_MIX), BF16), (proj3, tail0, w_pool, pool_scale),
        scratch=[pltpu.VMEM((POOL_HALO + tt, W_MIX), F32)], dims=("parallel", "arbitrary"))


def _conv_kernel(val_ref, gate_ref, t0_ref, w_ref, cb_ref, lg_ref, lb_ref, y_ref, u_ref, ext_ref, sh_ref,
                 *, tt):
    i = pl.program_id(1)

    @pl.when(i == 0)
    def _():
        ext_ref[0:CONV_HALO, :] = t0_ref[...]

    @pl.when(i > 0)
    def _():
        ext_ref[0:CONV_HALO, :] = ext_ref[tt:tt + CONV_HALO, :]

    u = val_ref[...] * _sigmoid(gate_ref[...])
    u_ref[...] = u
    ext_ref[CONV_HALO:CONV_HALO + tt, :] = u
    first = CONV_HALO - CONV_BUF
    last = first + CONV_WIDTH - 1
    for b in range(SUBLANES):
        top = max(s for s in range(first, last + 1) if s % SUBLANES == b)
        sh_ref[b, 0:top - b + tt, :] = ext_ref[b:top + tt, :]
    acc = cb_ref[...]
    for j in range(CONV_WIDTH):
        s = first + j
        base = s - s % SUBLANES
        acc = acc + sh_ref[s % SUBLANES, base:base + tt, :] * w_ref[j:j + 1, :]
    y = acc
    mu = jnp.mean(y, axis=-1, keepdims=True)
    yc = y - mu
    var = jnp.mean(yc * yc, axis=-1, keepdims=True)
    yn = yc * lax.rsqrt(var + LN_EPS) * lg_ref[...] + lb_ref[...]
    y_ref[...] = (yn * _sigmoid(yn)).astype(y_ref.dtype)


def conv_mixer(l, proj3, tail0, conv_w, conv_b, ln_g, ln_b):
    b, t, _ = proj3.shape
    tt = min(256, t)
    vec = pl.BlockSpec((None, 1, W_MIX), lambda b, i, l: (l[0], 0, 0))
    blk = pl.BlockSpec((None, tt, W_MIX), lambda b, i, l: (b, i, 0))
    return _call(
        functools.partial(_conv_kernel, tt=tt), l, (b, t // tt),
        [pl.BlockSpec((None, tt, W_MIX), lambda b, i, l: (b, i, OFF_B // W_MIX)),
         pl.BlockSpec((None, tt, W_MIX), lambda b, i, l: (b, i, OFF_B // W_MIX + 1)),
         pl.BlockSpec((None, CONV_HALO, W_MIX), lambda b, i, l: (b, 0, 0)),
         pl.BlockSpec((None, CONV_WIDTH, W_MIX), lambda b, i, l: (l[0], 0, 0)),
         vec, vec, vec],
        [blk, blk],
        [jax.ShapeDtypeStruct((b, t, W_MIX), BF16), jax.ShapeDtypeStruct((b, t, W_MIX), F32)],
        (proj3, proj3, tail0, conv_w, conv_b, ln_g, ln_b),
        scratch=[pltpu.VMEM((CONV_HALO + tt, W_MIX), F32),
                 pltpu.VMEM((SUBLANES, CONV_HALO + tt, W_MIX), F32)], dims=("parallel", "arbitrary"))


def _sb_heads(x):
    n = x.shape[1] // SB_HEAD_DIM
    return jnp.stack([x[:, h * SB_HEAD_DIM:(h + 1) * SB_HEAD_DIM] for h in range(n)], axis=0)


def _sb_block(q3, k3, v3, bias, carry, acc, neg_upper, mask):
    nh, tq, _ = q3.shape
    tk = k3.shape[1]
    z = _bdot_nt(q3, k3)
    z = jnp.stack([z[h] + bias[h] for h in range(nh)], axis=0)
    fail = jnp.maximum(z, 0.0) + jnp.log(1.0 + jnp.exp2(jnp.abs(z) * (-LOG2E)))
    if mask is not None:
        fail = jnp.where(mask, fail, 0.0)
    log_hit = z - fail
    fail_first = fail[:, :, 0:1]
    later = jnp.dot(fail.astype(BF16).reshape(nh * tq, tk), neg_upper,
                    preferred_element_type=F32).reshape(nh, tq, tk)
    wts = jnp.exp(log_hit + later + carry)
    if mask is not None:
        wts = jnp.where(mask, wts, 0.0)
    acc = acc + _bdot(wts, v3)
    carry = carry + later[:, :, 0:1] - fail_first
    return carry, acc


def _sb_prompt_kernel(bias_ref, q_ref, k_ref, v_ref, o_ref, kb_ref, vb_ref, *, blk, n_blk):
    qi = pl.program_id(2)
    grp = pl.program_id(1)

    @pl.when(qi == 0)
    def _():
        def fill(j, _):
            rows = pl.ds(pl.multiple_of(j * blk, blk), blk)
            kb_ref[:, rows, :] = _sb_heads(k_ref[rows, :].astype(BF16))
            vb_ref[:, rows, :] = _sb_heads(v_ref[rows, :].astype(BF16))
            return 0
        lax.fori_loop(0, n_blk, fill, 0)

    row = lax.broadcasted_iota(jnp.int32, (blk, blk), 0)
    col = lax.broadcasted_iota(jnp.int32, (blk, blk), 1)
    causal = row > col
    neg_upper = jnp.where(causal, -1.0, 0.0).astype(BF16)
    q3 = _sb_heads((q_ref[...] * SB_SCALE).astype(BF16))
    nh = q3.shape[0]
    bias = [bias_ref[0, grp * nh + h] for h in range(nh)]

    def kv(j):
        rows = pl.ds(pl.multiple_of(j * blk, blk), blk)
        return kb_ref[:, rows, :], vb_ref[:, rows, :]

    carry = jnp.zeros((nh, blk, 1), F32)
    acc = jnp.zeros((nh, blk, SB_HEAD_DIM), F32)
    carry, acc = _sb_block(q3, *kv(qi), bias, carry, acc, neg_upper, causal)

    def step(j, state):
        return _sb_block(q3, *kv(j), bias, state[0], state[1], neg_upper, None)

    rem = qi % SB_UNROLL
    carry, acc = lax.fori_loop(0, rem, lambda it, state: step(qi - 1 - it, state), (carry, acc))

    def body(it, state):
        j = qi - 1 - rem - SB_UNROLL * it
        for u in range(SB_UNROLL):
            state = step(j - u, state)
        return state

    carry, acc = lax.fori_loop(0, qi // SB_UNROLL, body, (carry, acc))
    o_ref[...] = jnp.concatenate([acc[h] for h in range(nh)], axis=1).astype(o_ref.dtype)


def sb_prompt(l, proj3, sb_bias):
    b, t, _ = proj3.shape
    blk = min(ATTN_BLOCK, t)
    wid = SB_GROUP * SB_HEAD_DIM
    ngrp = W_MIX // wid
    c0 = OFF_C // wid
    return _call(
        functools.partial(_sb_prompt_kernel, blk=blk, n_blk=t // blk), l, (b, ngrp, t // blk),
        [pl.BlockSpec((None, 1, SB_HEADS), lambda b, h, i, l: (l[0], 0, 0), memory_space=pltpu.SMEM),
         pl.BlockSpec((None, blk, wid), lambda b, h, i, l: (b, i, c0 + h)),
         pl.BlockSpec((None, t, wid), lambda b, h, i, l: (b, 0, c0 + ngrp + h)),
         pl.BlockSpec((None, t, wid), lambda b, h, i, l: (b, 0, c0 + 2 * ngrp + h))],
        pl.BlockSpec((None, blk, wid), lambda b, h, i, l: (b, i, h)),
        jax.ShapeDtypeStruct((b, t, W_MIX), BF16), (sb_bias, proj3, proj3, proj3),
        scratch=[pltpu.VMEM((SB_GROUP, t, SB_HEAD_DIM), BF16)] * 2,
        dims=("parallel", "parallel", "arbitrary"))


def _sb_decode_kernel(q_ref, bias_ref, *refs, n_steps, pages):
    k_refs, v_refs = refs[:pages], refs[pages:2 * pages]
    o_ref, carry_ref, acc_ref = refs[2 * pages:]
    p = pl.program_id(1)

    @pl.when(p == 0)
    def _():
        carry_ref[...] = jnp.zeros_like(carry_ref)
        acc_ref[...] = jnp.zeros_like(acc_ref)

    q = (q_ref[...] * SB_SCALE).astype(BF16)
    bias = bias_ref[...]
    z = jnp.concatenate([_dot(q, k_ref[...].reshape(W_MIX, PAGE_SIZE)) + bias for k_ref in k_refs], axis=0)
    log_fail = -_softplus(z)
    row = lax.broadcasted_iota(jnp.int32, (PAGE_SIZE, PAGE_SIZE), 0)
    col = lax.broadcasted_iota(jnp.int32, (PAGE_SIZE, PAGE_SIZE), 1)
    upper = (row > col).astype(BF16)
    later = _dot(log_fail, upper)
    total = later[:, 0:1] + log_fail[:, 0:1]
    arg = z + log_fail + later
    carry = carry_ref[...]
    acc = acc_ref[...]
    for i in reversed(range(pages)):
        rows = slice(i * SB_HEADS, (i + 1) * SB_HEADS)
        wts = jnp.exp(arg[rows] + carry)
        acc = acc + _dot_nt(wts, v_refs[i][...].reshape(W_MIX, PAGE_SIZE))
        carry = carry + total[rows]
    carry_ref[...] = carry
    acc_ref[...] = acc

    @pl.when(p == n_steps - 1)
    def _():
        head = lax.broadcasted_iota(jnp.int32, acc.shape, 0)
        col_head = lax.broadcasted_iota(jnp.int32, acc.shape, 1) // SB_HEAD_DIM
        o_ref[...] = jnp.sum(jnp.where(head == col_head, acc, 0.0), axis=0, keepdims=True)


def sb_decode(l, q_blocks, sb_bias_col, cache_kt, cache_vt, page_table):
    b = q_blocks.shape[0]
    n_pages = page_table.shape[1]
    pages = DECODE_PAGES if n_pages % DECODE_PAGES == 0 else 1
    n_steps = n_pages // pages

    def page(i):
        return pl.BlockSpec((None, None, SB_HEADS, SB_HEAD_DIM, PAGE_SIZE),
                            lambda b, p, l, pt: (l[0], pt[b, n_pages - pages * (p + 1) + i], 0, 0, 0))

    return _call(
        functools.partial(_sb_decode_kernel, n_steps=n_steps, pages=pages), l, (b, n_steps),
        [pl.BlockSpec((None, SB_HEADS, W_MIX), lambda b, p, l, pt: (b, 0, 0)),
         pl.BlockSpec((None, SB_HEADS, 1), lambda b, p, l, pt: (l[0], 0, 0))]
        + [page(i) for i in range(pages)] * 2,
        pl.BlockSpec((None, 1, W_MIX), lambda b, p, l, pt: (b, 0, 0)),
        jax.ShapeDtypeStruct((b, 1, W_MIX), F32),
        (q_blocks, sb_bias_col) + (cache_kt,) * pages + (cache_vt,) * pages,
        scratch=[pltpu.VMEM((SB_HEADS, 1), F32), pltpu.VMEM((SB_HEADS, W_MIX), F32)],
        dims=("parallel", "arbitrary"), prefetch=(page_table,))


def _wkv_inputs(pr_ref, pk_ref, pv_ref, pl_ref, sh0_ref, mu_ref, w0_ref, dup_ref, a0_ref, aup_ref,
                gup_ref, ext_ref, *, tt, t_valid):
    i = pl.program_id(1)

    @pl.when(i == 0)
    def _():
        ext_ref[0:SHIFT_HALO, :] = sh0_ref[...]

    @pl.when(i > 0)
    def _():
        ext_ref[0:SHIFT_HALO, :] = ext_ref[tt:tt + SHIFT_HALO, :]

    rows = slice(SHIFT_HALO, SHIFT_HALO + tt)
    ext_ref[rows, 0:W_MIX] = pr_ref[...]
    ext_ref[rows, W_MIX:2 * W_MIX] = pk_ref[...]
    ext_ref[rows, 2 * W_MIX:3 * W_MIX] = pv_ref[...]
    ext_ref[rows, 3 * W_MIX:N_D] = pl_ref[...]
    p = ext_ref[rows, :]
    prev = ext_ref[SHIFT_HALO - 1:SHIFT_HALO - 1 + tt, :]
    xs = p + (prev - p) * mu_ref[...]
    o = 3 * W_MIX
    w_dn = xs[:, o:o + DECAY_LORA]
    a_dn = xs[:, o + DECAY_LORA:o + DECAY_LORA + A_LORA]
    g_dn = xs[:, o + DECAY_LORA + A_LORA:N_D]
    w_log = -_softplus(-(w0_ref[...] + _dot(jnp.tanh(w_dn), dup_ref[...]))) - 0.5
    log_decay = -jnp.exp(w_log)
    alpha = _sigmoid(a0_ref[...] + _dot(a_dn, aup_ref[...]))
    k = xs[:, W_MIX:2 * W_MIX]
    v = xs[:, 2 * W_MIX:3 * W_MIX]
    if t_valid is not None:
        valid = (i * tt + lax.broadcasted_iota(jnp.int32, (tt, 1), 0)) < t_valid
        k = jnp.where(valid, k, 0.0)
        v = jnp.where(valid, v, 0.0)
        log_decay = jnp.where(valid, log_decay, 0.0)
    return xs[:, 0:W_MIX], k, v, log_decay, alpha, _dot(_sigmoid(g_dn), gup_ref[...])


def _bdot(a, b):
    return lax.dot_general(a.astype(BF16), b.astype(BF16), (((2,), (1,)), ((0,), (0,))),
                           preferred_element_type=F32)


def _bdot_nt(a, b):
    return lax.dot_general(a.astype(BF16), b.astype(BF16), (((2,), (2,)), ((0,), (0,))),
                           preferred_element_type=F32)


def _bdot_tn(a, b):
    return lax.dot_general(a.astype(BF16), b.astype(BF16), (((1,), (1,)), ((0,), (0,))),
                           preferred_element_type=F32)


def _heads(x):
    return jnp.stack([x[:, h * RWKV_HEAD_DIM:(h + 1) * RWKV_HEAD_DIM] for h in range(RWKV_HEADS)], axis=0)


def _head_sums(xs):
    r = lax.broadcasted_iota(jnp.int32, (W_MIX, W_MIX), 0) // RWKV_HEAD_DIM
    c = lax.broadcasted_iota(jnp.int32, (W_MIX, W_MIX), 1) // RWKV_HEAD_DIM
    same = (r == c).astype(BF16)
    rows = xs[0].shape[0]
    parts = [p for x in xs for p in _split_bf16(x)]
    out = jnp.dot(jnp.concatenate(parts, axis=0), same, preferred_element_type=F32)
    return [out[(2 * i) * rows:(2 * i + 1) * rows] + out[(2 * i + 1) * rows:(2 * i + 2) * rows]
            for i in range(len(xs))]


def _wkv_chunk(r, k, v, ld, al, g, s, kk_w, ka_w, rk_w, ln_g, ln_b):
    ch = r.shape[0]
    row = lax.broadcasted_iota(jnp.int32, (ch, ch), 0)
    col = lax.broadcasted_iota(jnp.int32, (ch, ch), 1)
    incl = row >= col
    eye = (row == col).astype(F32)
    row2 = lax.broadcasted_iota(jnp.int32, (2 * ch, 2 * ch), 0)
    col2 = lax.broadcasted_iota(jnp.int32, (2 * ch, 2 * ch), 1) & (ch - 1)
    tri = (row2 & (ch - 1)) + (row2 // ch) > col2
    hi, lo = _split_bf16(ld)
    lower = incl.astype(BF16)
    cum = (jnp.dot(lower, hi, preferred_element_type=F32)
           + jnp.dot(lower, lo, preferred_element_type=F32))
    cum_end = cum[ch - 1:ch, :]
    g_in = jnp.exp(cum)
    g_ex = jnp.exp(cum - ld)
    g_inv = jnp.exp(-cum)
    g_end = jnp.exp(cum_end - cum)
    g_tot = jnp.exp(cum_end)
    kk = k * kk_w
    k2 = k * (1.0 + (al - 1.0) * ka_w)
    kk_sq, rk_sum = _head_sums([kk * kk, r * k2 * rk_w])
    kkn = kk / jnp.maximum(jnp.sqrt(kk_sq), 1e-12)
    b = kkn * al
    a2 = _heads(jnp.concatenate([-kkn * g_ex, r * g_in], axis=0))
    bm = _heads(jnp.concatenate([b * g_inv, k2 * g_inv], axis=0))
    bk_end = _heads(jnp.concatenate([b * g_end, k2 * g_end], axis=0))
    v3 = _heads(v)
    prod = jnp.where(tri, _bdot_nt(a2, bm), 0.0)
    x = prod[:, 0:ch, 0:ch]
    inv = eye + x
    n_sq = ch.bit_length() - 2
    for _ in range(n_sq):
        x = _bdot(x, x)
        inv = inv + _bdot(inv, x)
    w_rhs = _bdot_nt(a2[:, 0:ch], s) + _bdot(prod[:, 0:ch], jnp.concatenate([jnp.zeros_like(v3), v3], axis=1))
    u = _bdot(inv, w_rhs)
    uv = jnp.concatenate([u, v3], axis=1)
    y3 = _bdot_nt(a2[:, ch:2 * ch], s) + _bdot(prod[:, ch:2 * ch], uv)
    s_new = s * _heads(g_tot) + _bdot_tn(uv, bk_end)
    mu = jnp.mean(y3, axis=-1, keepdims=True)
    yc = y3 - mu
    var = jnp.mean(yc * yc, axis=-1, keepdims=True)
    yn3 = yc * lax.rsqrt(var + GN_EPS)
    yn = jnp.concatenate([yn3[h] for h in range(RWKV_HEADS)], axis=1)
    return (yn * ln_g + ln_b + rk_sum * v) * g, s_new


def _wkv_kernel(pr_ref, pk_ref, pv_ref, pl_ref, sh0_ref, mu_ref, w0_ref, dup_ref, a0_ref, aup_ref, gup_ref,
                s0_ref, kk_ref, ka_ref, rk_ref, lg_ref, lb_ref, y_ref, so_ref, s_ref, ext_ref,
                *, ch, n_steps, t_valid):
    c = pl.program_id(1)

    @pl.when(c == 0)
    def _():
        s_ref[...] = s0_ref[...]

    tt = pr_ref.shape[0]
    r, k, v, ld, al, g = _wkv_inputs(pr_ref, pk_ref, pv_ref, pl_ref, sh0_ref, mu_ref, w0_ref, dup_ref,
                                     a0_ref, aup_ref, gup_ref, ext_ref, tt=tt, t_valid=t_valid)
    s = s_ref[...]
    for j in range(tt // ch):
        rows = slice(j * ch, (j + 1) * ch)
        y, s = _wkv_chunk(r[rows], k[rows], v[rows], ld[rows], al[rows], g[rows], s,
                          kk_ref[...], ka_ref[...], rk_ref[...], lg_ref[...], lb_ref[...])
        y_ref[rows, :] = y.astype(y_ref.dtype)
    s_ref[...] = s

    @pl.when(c == n_steps - 1)
    def _():
        so_ref[...] = s


def wkv_mixer(l, proj3, shift0, s0, lw, t_valid):
    b, t, _ = proj3.shape
    ch = min(WKV_CHUNK, t)
    rows = ch * WKV_CHUNKS_PER_STEP if t % (ch * WKV_CHUNKS_PER_STEP) == 0 else ch
    cd = OFF_D // W_MIX
    col = lambda j: pl.BlockSpec((None, rows, W_MIX), lambda b, c, l: (b, c, cd + j))
    st = pl.BlockSpec((None, RWKV_HEADS, RWKV_HEAD_DIM, RWKV_HEAD_DIM), lambda b, c, l: (b, 0, 0, 0))
    vec = lambda n: pl.BlockSpec((None, 1, n), lambda b, c, l: (l[0], 0, 0))
    up = lambda n: pl.BlockSpec((None, n, W_MIX), lambda b, c, l: (l[0], 0, 0))
    return _call(
        functools.partial(_wkv_kernel, ch=ch, n_steps=t // rows, t_valid=t_valid), l, (b, t // rows),
        [col(0), col(1), col(2),
         pl.BlockSpec((None, rows, N_LORA), lambda b, c, l: (b, c, (OFF_D + 3 * W_MIX) // N_LORA)),
         pl.BlockSpec((None, SHIFT_HALO, N_D), lambda b, c, l: (b, 0, 0)),
         vec(N_D), vec(W_MIX), up(DECAY_LORA), vec(W_MIX), up(A_LORA), up(GATE_LORA), st]
        + [vec(W_MIX)] * 5,
        [pl.BlockSpec((None, rows, W_MIX), lambda b, c, l: (b, c, 0)), st],
        [jax.ShapeDtypeStruct((b, t, W_MIX), BF16),
         jax.ShapeDtypeStruct((b, RWKV_HEADS, RWKV_HEAD_DIM, RWKV_HEAD_DIM), F32)],
        (proj3, proj3, proj3, proj3, shift0, lw['shift_mu'], lw['decay_w0'], lw['decay_up'], lw['a0'],
         lw['a_up'], lw['g_up'], s0, lw['k_k'], lw['k_a'], lw['r_k'], lw['lnx_g'], lw['lnx_b']),
        scratch=[pltpu.VMEM((RWKV_HEADS, RWKV_HEAD_DIM, RWKV_HEAD_DIM), F32),
                 pltpu.VMEM((SHIFT_HALO + rows, N_D), F32)],
        dims=("parallel", "arbitrary"))


def _split_heads_kernel(k_ref, v_ref, ko_ref, vo_ref):
    ko_ref[...] = k_ref[...].T.reshape(ko_ref.shape)
    vo_ref[...] = v_ref[...].T.reshape(vo_ref.shape)


def split_heads(l, proj3):
    b, t, _ = proj3.shape
    tt = min(512, t)
    col = lambda j: pl.BlockSpec((None, tt, W_MIX), lambda b, i, l: (b, i, OFF_C // W_MIX + j))
    out = pl.BlockSpec((None, SB_HEADS, SB_HEAD_DIM, tt), lambda b, i, l: (b, 0, 0, i))
    shape = jax.ShapeDtypeStruct((b, SB_HEADS, SB_HEAD_DIM, t), F32)
    return _call(_split_heads_kernel, l, (b, t // tt), [col(1), col(2)], [out, out], [shape, shape],
                 (proj3, proj3), dims=("parallel", "parallel"))


def _mixers(l, proj3, pool0, conv0, shift0, wkv0, pos0, t_valid, attend, lw):
    b, t, _ = proj3.shape
    y_a = pool_mixer(l, proj3, pool0, lw['w_pool'], lw['pool_scale'], pos0)
    y_b, glu = conv_mixer(l, proj3, conv0, lw['conv_w'], lw['conv_b'], lw['ln_g'], lw['ln_b'])
    y_c = attend(l, proj3)
    y_d, wkv_new = wkv_mixer(l, proj3, shift0, wkv0, lw, t_valid)
    return [y.reshape(b * t, W_MIX) for y in (y_a, y_b, y_c, y_d)], glu, wkv_new


def _trunk_layer(l, xp, xs, mix_p, mix_s, lw):
    dp, ds = xp.shape, xs.shape
    xp, xs = xp.reshape(-1, dp[2]), xs.reshape(-1, ds[2])

    def ffn(xp, xs, g, w_a, w_b):
        act_p, act_s = ffn_in(l, rmsnorm(l, xp, g, BF16), rmsnorm(l, xs, g, BF16), w_a)
        w_b16 = lax.dynamic_index_in_dim(w_b, l[0], 0, keepdims=True).astype(BF16)
        return matmul(l, act_p, act_s, w_b16, tn=512, res=xp, res_s=xs, scale=FFN_RESIDUAL)

    xp, xs = ffn(xp, xs, lw['g_ffn1'], lw['w_ffn1_in'], lw['w_ffn1_out'])
    hp, hs = rmsnorm(l, xp, lw['g_mix'], BF16), rmsnorm(l, xs, lw['g_mix'], BF16)
    proj_p, proj_s = matmul(l, hp, hs, lw['w_in'], tn=256, n=OFF_G, tm=2048)
    proj_p, proj_s = proj_p.reshape(dp[0], dp[1], OFF_G), proj_s.reshape(ds[0], ds[1], OFF_G)
    ys_p, glu_p, wkv_p = mix_p(l, proj_p)
    ys_s, glu_s, wkv_s = mix_s(l, proj_s)
    merged_p, merged_s = merge(l, hp, ys_p, hs, ys_s, lw['w_in'], lw['w_branch'])
    xp, xs = matmul(l, merged_p, merged_s, lw['w_out'], tn=256, res=xp, res_s=xs, scale=1.0, tm=2048)
    xp, xs = ffn(xp, xs, lw['g_ffn2'], lw['w_ffn2_in'], lw['w_ffn2_out'])
    return xp.reshape(dp), xs.reshape(ds), (proj_p, glu_p, wkv_p), (proj_s, glu_s, wkv_s)


def _front_pad(rows, height):
    return jnp.pad(rows, ((0, 0), (height - rows.shape[1], 0), (0, 0)))


def kernel(x_prompt, x_sample, cache_k, cache_v, state_pool, state_conv, state_shift, state_wkv,
           page_table, g_ffn1, w_ffn1_in, w_ffn1_out, g_mix, w_in, w_pool, pool_scale, conv_w,
           conv_b, ln_g, ln_b, sb_bias, shift_mu, decay_w0, decay_up, a0, a_up, g_up, k_k, k_a, r_k,
           lnx_g, lnx_b, w_branch, w_out, g_ffn2, w_ffn2_in, w_ffn2_out, g_final):
    depth = w_in.shape[0]
    bp, tp, d = x_prompt.shape
    bs, ts, _ = x_sample.shape
    assert d == D_MODEL and ts == 1
    past_len = page_table.shape[1] * PAGE_SIZE
    row = lambda a: a.reshape(depth, 1, -1)
    lw = {
        'g_ffn1': row(g_ffn1), 'w_ffn1_in': w_ffn1_in, 'w_ffn1_out': w_ffn1_out, 'g_mix': row(g_mix),
        'w_in': w_in, 'w_pool': w_pool, 'pool_scale': row(pool_scale), 'conv_w': conv_w,
        'conv_b': row(conv_b), 'ln_g': row(ln_g), 'ln_b': row(ln_b), 'sb_bias': row(sb_bias),
        'shift_mu': row(shift_mu), 'decay_w0': row(decay_w0), 'decay_up': decay_up, 'a0': row(a0),
        'a_up': a_up, 'g_up': g_up, 'k_k': row(k_k), 'k_a': row(k_a), 'r_k': row(r_k),
        'lnx_g': row(lnx_g), 'lnx_b': row(lnx_b), 'w_branch': w_branch, 'w_out': w_out,
        'g_ffn2': row(g_ffn2), 'w_ffn2_in': w_ffn2_in, 'w_ffn2_out': w_ffn2_out,
    }
    sb_bias_col = sb_bias.reshape(depth, SB_HEADS, 1)
    cache_kt = jnp.transpose(cache_k, (0, 1, 3, 4, 2))
    cache_vt = jnp.transpose(cache_v, (0, 1, 3, 4, 2))
    own_head = (jnp.arange(W_MIX)[None, :] // SB_HEAD_DIM) == jnp.arange(SB_HEADS)[:, None]
    xs0 = jnp.pad(x_sample, ((0, 0), (0, SAMPLE_ROWS - ts), (0, 0)))

    def layer(carry, li):
        xp, xs = carry
        l = li.reshape(1)
        pool_s = lax.dynamic_index_in_dim(state_pool, li, 0, keepdims=False)
        conv_s = lax.dynamic_index_in_dim(state_conv, li, 0, keepdims=False)
        shift_s = lax.dynamic_index_in_dim(state_shift, li, 0, keepdims=False)
        wkv_s = lax.dynamic_index_in_dim(state_wkv, li, 0, keepdims=False)

        def attend_sample(l, p3):
            q_blocks = jnp.where(own_head, p3[:, 0:1, OFF_C:OFF_C + W_MIX], 0.0)
            y = sb_decode(l, q_blocks, sb_bias_col, cache_kt, cache_vt, page_table)
            return jnp.pad(y, ((0, 0), (0, SAMPLE_ROWS - 1), (0, 0))).astype(BF16)

        mix_p = lambda l, p3: _mixers(
            l, p3, jnp.zeros((bp, POOL_HALO, W_MIX), F32), jnp.zeros((bp, CONV_HALO, W_MIX), F32),
            jnp.zeros((bp, SHIFT_HALO, N_D), F32),
            jnp.zeros((bp, RWKV_HEADS, RWKV_HEAD_DIM, RWKV_HEAD_DIM), F32), 0, None,
            lambda l, p3: sb_prompt(l, p3, lw['sb_bias']), lw)
        mix_s = lambda l, p3: _mixers(
            l, p3, _front_pad(pool_s, POOL_HALO), _front_pad(conv_s, CONV_HALO),
            _front_pad(shift_s, SHIFT_HALO), wkv_s, past_len, ts, attend_sample, lw)
        xp, xs, (proj_p, glu_p, wkv_p), (proj_s, glu_s, wkv_s_new) = _trunk_layer(l, xp, xs, mix_p, mix_s, lw)

        k_p, v_p = split_heads(l, proj_p)
        k_s, v_s = (proj_s[:, :ts, OFF_C + j * W_MIX:OFF_C + (j + 1) * W_MIX].reshape(
            bs, ts, SB_HEADS, SB_HEAD_DIM) for j in (1, 2))
        outs = (
            k_p, v_p, proj_p[:, tp - POOL_BUF:, OFF_A:OFF_A + W_MIX], glu_p[:, tp - CONV_BUF:],
            proj_p[:, tp - 1:, OFF_D:OFF_D + N_D], wkv_p,
            k_s, v_s,
            jnp.concatenate([pool_s, proj_s[:, :ts, OFF_A:OFF_A + W_MIX]], axis=1)[:, -POOL_BUF:],
            jnp.concatenate([conv_s, glu_s[:, :ts]], axis=1)[:, -CONV_BUF:],
            proj_s[:, ts - 1:ts, OFF_D:OFF_D + N_D], wkv_s_new)
        return (xp, xs), outs

    (xp, xs), st = lax.scan(layer, (x_prompt, xs0), jnp.arange(depth, dtype=jnp.int32))
    zero = jnp.zeros((1,), jnp.int32)
    g_fin = g_final.reshape(1, 1, d)
    y_prompt = rmsnorm(zero, xp.reshape(bp * tp, d), g_fin, F32).reshape(bp, tp, d)
    y_sample = rmsnorm(zero, xs.reshape(bs * SAMPLE_ROWS, d), g_fin, F32).reshape(bs, SAMPLE_ROWS, d)[:, :ts]
    to_cache_layout = lambda a: jnp.transpose(a, (0, 1, 4, 2, 3))
    return (y_prompt, y_sample, to_cache_layout(st[0]), to_cache_layout(st[1])) + tuple(st[2:])
```

```python
import functools

import jax
import jax.numpy as jnp
from jax import lax
from jax.experimental import pallas as pl
from jax.experimental.pallas import tpu as pltpu

F32 = jnp.float32
BF16 = jnp.bfloat16
LOG2E = 1.4426950408889634

LANES = 128
SUBLANES = 8
VMEM_LIMIT_BYTES = 56 * 1024 * 1024

D_MODEL = 2048
W_MIX = D_MODEL // 4
POOL_WINDOWS = (2, 4, 8, 16)
POOL_GROUP = W_MIX // len(POOL_WINDOWS)
POOL_BUF = max(POOL_WINDOWS) - 1
CONV_WIDTH = 31
CONV_BUF = CONV_WIDTH - 1
SB_HEADS = 8
SB_HEAD_DIM = W_MIX // SB_HEADS
SB_SCALE = SB_HEAD_DIM ** -0.5
RWKV_HEAD_DIM = 64
RWKV_HEADS = W_MIX // RWKV_HEAD_DIM
DECAY_LORA = 64
A_LORA = 64
GATE_LORA = 128
N_LORA = DECAY_LORA + A_LORA + GATE_LORA
N_D = 3 * W_MIX + N_LORA
N_BRANCH = 4
OFF_A = 0
OFF_B = OFF_A + W_MIX
OFF_C = OFF_B + 2 * W_MIX
OFF_D = OFF_C + 3 * W_MIX
OFF_G = OFF_D + N_D
N_IN = OFF_G + N_BRANCH * D_MODEL
D_FF = ((8 * D_MODEL // 3 + 127) // 128) * 128
FFN_RESIDUAL = 0.5
RMS_EPS = 1e-6
LN_EPS = 1e-5
GN_EPS = 64e-5
PAGE_SIZE = 128

POOL_HALO = 16
CONV_HALO = 32
SHIFT_HALO = 8
SAMPLE_ROWS = 8
ATTN_BLOCK = 256
SB_GROUP = 4
SB_UNROLL = 4
DECODE_PAGES = 16
FFN_ROW_TILE = 4096
WKV_CHUNK = 64
WKV_CHUNKS_PER_STEP = 4


def _call(kernel, l, grid, in_specs, out_specs, out_shape, args, scratch=(), dims=None, prefetch=()):
    n_pre = 1 + len(prefetch)

    def body(*refs):
        kernel(*refs[n_pre:])

    grid_spec = pltpu.PrefetchScalarGridSpec(
        num_scalar_prefetch=n_pre, grid=grid, in_specs=in_specs, out_specs=out_specs,
        scratch_shapes=list(scratch))
    fn = kernel.func if isinstance(kernel, functools.partial) else kernel
    return pl.pallas_call(
        body, grid_spec=grid_spec, out_shape=out_shape, name=fn.__name__.strip("_"),
        compiler_params=pltpu.CompilerParams(
            dimension_semantics=dims or ("arbitrary",) * len(grid),
            vmem_limit_bytes=VMEM_LIMIT_BYTES),
    )(l, *prefetch, *args)


def _sigmoid(x):
    return 1.0 / (1.0 + jnp.exp(-x))


def _softplus(x):
    return jnp.maximum(x, 0.0) + jnp.log1p(jnp.exp(-jnp.abs(x)))


def _split_bf16(x):
    hi = x.astype(BF16)
    lo = (x - hi.astype(F32)).astype(BF16)
    return hi, lo


def _dot(a, b):
    return jnp.dot(a.astype(BF16), b.astype(BF16), preferred_element_type=F32)


def _dot_nt(a, b):
    return lax.dot_general(a.astype(BF16), b.astype(BF16), (((1,), (1,)), ((), ())),
                           preferred_element_type=F32)


def _rmsnorm_kernel(x_ref, g_ref, o_ref):
    x = x_ref[...]
    y = x * lax.rsqrt(jnp.mean(x * x, axis=-1, keepdims=True) + RMS_EPS)
    o_ref[...] = (y * g_ref[...]).astype(o_ref.dtype)


def rmsnorm(l, x, g, out_dtype):
    m, d = x.shape
    tm = min(512, m)
    return _call(
        _rmsnorm_kernel, l, (m // tm,),
        [pl.BlockSpec((tm, d), lambda i, l: (i, 0)),
         pl.BlockSpec((None, 1, d), lambda i, l: (l[0], 0, 0))],
        pl.BlockSpec((tm, d), lambda i, l: (i, 0)),
        jax.ShapeDtypeStruct((m, d), out_dtype), (x, g), dims=("parallel",))


def _side_spec(rows, cols, col_block):
    if col_block:
        return pl.BlockSpec((rows, cols), lambda i, j, l: (0, j))
    return pl.BlockSpec((rows, cols), lambda i, j, l: (0, 0))


def _side_out(rows, cols):
    return pl.BlockSpec((None, rows, cols), lambda i, j, l: (i, 0, j))


def _on_first_row_tile(out_ref, compute):
    @pl.when(pl.program_id(0) == 0)
    def _():
        out_ref[...] = compute().astype(out_ref.dtype)

    @pl.when(pl.program_id(0) != 0)
    def _():
        out_ref[...] = jnp.zeros_like(out_ref)


def _ffn_in_kernel(x_ref, wg_ref, wu_ref, xs_ref, o_ref, os_ref):
    w = jnp.concatenate([wg_ref[...], wu_ref[...]], axis=1).astype(BF16)

    def act(x):
        r = jnp.dot(x, w, preferred_element_type=F32)
        g = r[:, :LANES]
        return g * _sigmoid(g) * r[:, LANES:]

    o_ref[...] = act(x_ref[...]).astype(o_ref.dtype)
    _on_first_row_tile(os_ref, lambda: act(xs_ref[...]))


def ffn_in(l, x, xs, w):
    m, d = x.shape
    ms = xs.shape[0]
    d_ff = w.shape[2] // 2
    nb = d_ff // LANES
    tm = min(FFN_ROW_TILE, m)
    out, out_s = _call(
        _ffn_in_kernel, l, (m // tm, nb),
        [pl.BlockSpec((tm, d), lambda i, j, l: (i, 0), pipeline_mode=pl.Buffered(1)),
         pl.BlockSpec((None, d, LANES), lambda i, j, l: (l[0], 0, j)),
         pl.BlockSpec((None, d, LANES), lambda i, j, l: (l[0], 0, j + nb)),
         _side_spec(ms, d, False)],
        [pl.BlockSpec((tm, LANES), lambda i, j, l: (i, j)), _side_out(ms, LANES)],
        [jax.ShapeDtypeStruct((m, d_ff), BF16), jax.ShapeDtypeStruct((m // tm, ms, d_ff), BF16)],
        (x, w, w, xs), dims=("arbitrary", "arbitrary"))
    return out, out_s[0]


def _mm_kernel(a_ref, w_ref, as_ref, o_ref, os_ref):
    w = w_ref[...].astype(BF16)
    o_ref[...] = jnp.dot(a_ref[...], w, preferred_element_type=F32)
    _on_first_row_tile(os_ref, lambda: jnp.dot(as_ref[...], w, preferred_element_type=F32))


def _mm_res_kernel(a_ref, w_ref, r_ref, as_ref, rs_ref, o_ref, os_ref, *, scale):
    w = w_ref[...].astype(BF16)
    o_ref[...] = r_ref[...] + scale * jnp.dot(a_ref[...], w, preferred_element_type=F32)
    _on_first_row_tile(
        os_ref, lambda: rs_ref[...] + scale * jnp.dot(as_ref[...], w, preferred_element_type=F32))


def matmul(l, a, a_s, w, *, tn, res=None, res_s=None, scale=1.0, n=None, tm=1024):
    m, k = a.shape
    ms = a_s.shape[0]
    n = w.shape[2] if n is None else n
    tm = min(tm, m)
    a_spec = pl.BlockSpec((tm, k), lambda i, j, l: (i, 0))
    if w.shape[0] == 1:
        w_spec = pl.BlockSpec((None, k, tn), lambda i, j, l: (0, 0, j))
    else:
        w_spec = pl.BlockSpec((None, k, tn), lambda i, j, l: (l[0], 0, j))
    o_spec = pl.BlockSpec((tm, tn), lambda i, j, l: (i, j))
    out_specs = [o_spec, _side_out(ms, tn)]
    out_shape = [jax.ShapeDtypeStruct((m, n), F32), jax.ShapeDtypeStruct((m // tm, ms, n), F32)]
    dims = ("arbitrary", "arbitrary")
    if res is None:
        out, out_s = _call(_mm_kernel, l, (m // tm, n // tn), [a_spec, w_spec, _side_spec(ms, k, False)],
                           out_specs, out_shape, (a, w, a_s), dims=dims)
    else:
        out, out_s = _call(
            functools.partial(_mm_res_kernel, scale=scale), l, (m // tm, n // tn),
            [a_spec, w_spec, o_spec, _side_spec(ms, k, False), _side_spec(ms, tn, True)],
            out_specs, out_shape, (a, w, res, a_s, res_s), dims=dims)
    return out, out_s[0]


def _merge_kernel(h_ref, ya_ref, yb_ref, yc_ref, yd_ref, g0_ref, g1_ref, g2_ref, g3_ref, wb_ref,
                  hs_ref, sa_ref, sb_ref, sc_ref, sd_ref, o_ref, os_ref):
    wg = [g_ref[...].astype(BF16) for g_ref in (g0_ref, g1_ref, g2_ref, g3_ref)]
    wb = [wb_ref[n].astype(BF16) for n in range(N_BRANCH)]

    def gated_sum(h, ys):
        acc = None
        for n in range(N_BRANCH):
            gate = jnp.dot(h, wg[n], preferred_element_type=F32)
            t = _sigmoid(gate) * jnp.dot(ys[n], wb[n], preferred_element_type=F32)
            acc = t if acc is None else acc + t
        return acc

    o_ref[...] = gated_sum(h_ref[...], [r[...] for r in (ya_ref, yb_ref, yc_ref, yd_ref)]).astype(o_ref.dtype)
    _on_first_row_tile(os_ref, lambda: gated_sum(hs_ref[...], [r[...] for r in (sa_ref, sb_ref, sc_ref, sd_ref)]))


def merge(l, h, ys, h_s, ys_s, w_in, w_branch):
    m = h.shape[0]
    ms = h_s.shape[0]
    tn = 256
    tm = min(1024, m)
    nj = D_MODEL // tn
    y_spec = pl.BlockSpec((tm, W_MIX), lambda i, j, l: (i, 0))
    g_specs = [pl.BlockSpec((None, D_MODEL, tn), functools.partial(
        lambda i, j, l, n: (l[0], 0, OFF_G // tn + n * nj + j), n=n)) for n in range(N_BRANCH)]
    out, out_s = _call(
        _merge_kernel, l, (m // tm, nj),
        [pl.BlockSpec((tm, D_MODEL), lambda i, j, l: (i, 0))] + [y_spec] * N_BRANCH + g_specs
        + [pl.BlockSpec((None, N_BRANCH, W_MIX, tn), lambda i, j, l: (l[0], 0, 0, j))]
        + [_side_spec(ms, D_MODEL, False)] + [_side_spec(ms, W_MIX, False)] * N_BRANCH,
        [pl.BlockSpec((tm, tn), lambda i, j, l: (i, j)), _side_out(ms, tn)],
        [jax.ShapeDtypeStruct((m, D_MODEL), BF16),
         jax.ShapeDtypeStruct((m // tm, ms, D_MODEL), BF16)],
        (h, *ys, w_in, w_in, w_in, w_in, w_branch, h_s, *ys_s), dims=("arbitrary", "arbitrary"))
    return out, out_s[0]


def _pool_kernel(u_ref, t0_ref, wp_ref, ps_ref, y_ref, ext_ref, *, tt, pos0):
    i = pl.program_id(1)

    @pl.when(i == 0)
    def _():
        ext_ref[0:POOL_HALO, :] = t0_ref[...]

    @pl.when(i > 0)
    def _():
        ext_ref[0:POOL_HALO, :] = ext_ref[tt:tt + POOL_HALO, :]

    u = u_ref[...]
    ext_ref[POOL_HALO:POOL_HALO + tt, :] = u
    pos = pos0 + i * tt + lax.broadcasted_iota(jnp.int32, (tt, 1), 0)
    ys = []
    for gi, win in enumerate(POOL_WINDOWS):
        cs = slice(gi * POOL_GROUP, (gi + 1) * POOL_GROUP)
        wsum = u[:, cs]
        for k in range(1, win):
            wsum = wsum + ext_ref[POOL_HALO - k:POOL_HALO - k + tt, cs]
        count = jnp.minimum(win, pos + 1).astype(F32)
        d = wsum / count - u[:, cs]
        ys.append(_dot(d, wp_ref[gi]))
    y_ref[...] = (jnp.concatenate(ys, axis=1) * ps_ref[...]).astype(y_ref.dtype)


def pool_mixer(l, proj3, tail0, w_pool, pool_scale, pos0):
    b, t, _ = proj3.shape
    tt = min(512, t)
    return _call(
        functools.partial(_pool_kernel, tt=tt, pos0=pos0), l, (b, t // tt),
        [pl.BlockSpec((None, tt, W_MIX), lambda b, i, l: (b, i, OFF_A // W_MIX)),
         pl.BlockSpec((None, POOL_HALO, W_MIX), lambda b, i, l: (b, 0, 0)),
         pl.BlockSpec((None, len(POOL_WINDOWS), POOL_GROUP, POOL_GROUP), lambda b, i, l: (l[0], 0, 0, 0)),
         pl.BlockSpec((None, 1, W_MIX), lambda b, i, l: (l[0], 0, 0))],
        pl.BlockSpec((None, tt, W_MIX), lambda b, i, l: (b, i, 0)),
        jax.ShapeDtypeStruct((b, t, W_MIX), BF16), (proj3, tail0, w_pool, pool_scale),
        scratch=[pltpu.VMEM((POOL_HALO + tt, W_MIX), F32)], dims=("parallel", "arbitrary"))


def _conv_kernel(val_ref, gate_ref, t0_ref, w_ref, cb_ref, lg_ref, lb_ref, y_ref, u_ref, ext_ref, sh_ref,
                 *, tt):
    i = pl.program_id(1)

    @pl.when(i == 0)
    def _():
        ext_ref[0:CONV_HALO, :] = t0_ref[...]

    @pl.when(i > 0)
    def _():
        ext_ref[0:CONV_HALO, :] = ext_ref[tt:tt + CONV_HALO, :]

    u = val_ref[...] * _sigmoid(gate_ref[...])
    u_ref[...] = u
    ext_ref[CONV_HALO:CONV_HALO + tt, :] = u
    first = CONV_HALO - CONV_BUF
    last = first + CONV_WIDTH - 1
    for b in range(SUBLANES):
        top = max(s for s in range(first, last + 1) if s % SUBLANES == b)
        sh_ref[b, 0:top - b + tt, :] = ext_ref[b:top + tt, :]
    acc = cb_ref[...]
    for j in range(CONV_WIDTH):
        s = first + j
        base = s - s % SUBLANES
        acc = acc + sh_ref[s % SUBLANES, base:base + tt, :] * w_ref[j:j + 1, :]
    y = acc
    mu = jnp.mean(y, axis=-1, keepdims=True)
    yc = y - mu
    var = jnp.mean(yc * yc, axis=-1, keepdims=True)
    yn = yc * lax.rsqrt(var + LN_EPS) * lg_ref[...] + lb_ref[...]
    y_ref[...] = (yn * _sigmoid(yn)).astype(y_ref.dtype)


def conv_mixer(l, proj3, tail0, conv_w, conv_b, ln_g, ln_b):
    b, t, _ = proj3.shape
    tt = min(256, t)
    vec = pl.BlockSpec((None, 1, W_MIX), lambda b, i, l: (l[0], 0, 0))
    blk = pl.BlockSpec((None, tt, W_MIX), lambda b, i, l: (b, i, 0))
    return _call(
        functools.partial(_conv_kernel, tt=tt), l, (b, t // tt),
        [pl.BlockSpec((None, tt, W_MIX), lambda b, i, l: (b, i, OFF_B // W_MIX)),
         pl.BlockSpec((None, tt, W_MIX), lambda b, i, l: (b, i, OFF_B // W_MIX + 1)),
         pl.BlockSpec((None, CONV_HALO, W_MIX), lambda b, i, l: (b, 0, 0)),
         pl.BlockSpec((None, CONV_WIDTH, W_MIX), lambda b, i, l: (l[0], 0, 0)),
         vec, vec, vec],
        [blk, blk],
        [jax.ShapeDtypeStruct((b, t, W_MIX), BF16), jax.ShapeDtypeStruct((b, t, W_MIX), F32)],
        (proj3, proj3, tail0, conv_w, conv_b, ln_g, ln_b),
        scratch=[pltpu.VMEM((CONV_HALO + tt, W_MIX), F32),
                 pltpu.VMEM((SUBLANES, CONV_HALO + tt, W_MIX), F32)], dims=("parallel", "arbitrary"))


def _sb_heads(x):
    n = x.shape[1] // SB_HEAD_DIM
    return jnp.stack([x[:, h * SB_HEAD_DIM:(h + 1) * SB_HEAD_DIM] for h in range(n)], axis=0)


def _sb_block(q3, k3, v3, bias, carry, acc, neg_upper, mask):
    nh, tq, _ = q3.shape
    tk = k3.shape[1]
    z = _bdot_nt(q3, k3)
    z = jnp.stack([z[h] + bias[h] for h in range(nh)], axis=0)
    fail = jnp.maximum(z, 0.0) + jnp.log(1.0 + jnp.exp2(jnp.abs(z) * (-LOG2E)))
    if mask is not None:
        fail = jnp.where(mask, fail, 0.0)
    log_hit = z - fail
    fail_first = fail[:, :, 0:1]
    later = jnp.dot(fail.astype(BF16).reshape(nh * tq, tk), neg_upper,
                    preferred_element_type=F32).reshape(nh, tq, tk)
    wts = jnp.exp(log_hit + later + carry)
    if mask is not None:
        wts = jnp.where(mask, wts, 0.0)
    acc = acc + _bdot(wts, v3)
    carry = carry + later[:, :, 0:1] - fail_first
    return carry, acc


def _sb_prompt_kernel(bias_ref, q_ref, k_ref, v_ref, o_ref, kb_ref, vb_ref, *, blk, n_blk):
    qi = pl.program_id(2)
    grp = pl.program_id(1)

    @pl.when(qi == 0)
    def _():
        def fill(j, _):
            rows = pl.ds(pl.multiple_of(j * blk, blk), blk)
            kb_ref[:, rows, :] = _sb_heads(k_ref[rows, :].astype(BF16))
            vb_ref[:, rows, :] = _sb_heads(v_ref[rows, :].astype(BF16))
            return 0
        lax.fori_loop(0, n_blk, fill, 0)

    row = lax.broadcasted_iota(jnp.int32, (blk, blk), 0)
    col = lax.broadcasted_iota(jnp.int32, (blk, blk), 1)
    causal = row > col
    neg_upper = jnp.where(causal, -1.0, 0.0).astype(BF16)
    q3 = _sb_heads((q_ref[...] * SB_SCALE).astype(BF16))
    nh = q3.shape[0]
    bias = [bias_ref[0, grp * nh + h] for h in range(nh)]

    def kv(j):
        rows = pl.ds(pl.multiple_of(j * blk, blk), blk)
        return kb_ref[:, rows, :], vb_ref[:, rows, :]

    carry = jnp.zeros((nh, blk, 1), F32)
    acc = jnp.zeros((nh, blk, SB_HEAD_DIM), F32)
    carry, acc = _sb_block(q3, *kv(qi), bias, carry, acc, neg_upper, causal)

    def step(j, state):
        return _sb_block(q3, *kv(j), bias, state[0], state[1], neg_upper, None)

    rem = qi % SB_UNROLL
    carry, acc = lax.fori_loop(0, rem, lambda it, state: step(qi - 1 - it, state), (carry, acc))

    def body(it, state):
        j = qi - 1 - rem - SB_UNROLL * it
        for u in range(SB_UNROLL):
            state = step(j - u, state)
        return state

    carry, acc = lax.fori_loop(0, qi // SB_UNROLL, body, (carry, acc))
    o_ref[...] = jnp.concatenate([acc[h] for h in range(nh)], axis=1).astype(o_ref.dtype)


def sb_prompt(l, proj3, sb_bias):
    b, t, _ = proj3.shape
    blk = min(ATTN_BLOCK, t)
    wid = SB_GROUP * SB_HEAD_DIM
    ngrp = W_MIX // wid
    c0 = OFF_C // wid
    return _call(
        functools.partial(_sb_prompt_kernel, blk=blk, n_blk=t // blk), l, (b, ngrp, t // blk),
        [pl.BlockSpec((None, 1, SB_HEADS), lambda b, h, i, l: (l[0], 0, 0), memory_space=pltpu.SMEM),
         pl.BlockSpec((None, blk, wid), lambda b, h, i, l: (b, i, c0 + h)),
         pl.BlockSpec((None, t, wid), lambda b, h, i, l: (b, 0, c0 + ngrp + h)),
         pl.BlockSpec((None, t, wid), lambda b, h, i, l: (b, 0, c0 + 2 * ngrp + h))],
        pl.BlockSpec((None, blk, wid), lambda b, h, i, l: (b, i, h)),
        jax.ShapeDtypeStruct((b, t, W_MIX), BF16), (sb_bias, proj3, proj3, proj3),
        scratch=[pltpu.VMEM((SB_GROUP, t, SB_HEAD_DIM), BF16)] * 2,
        dims=("parallel", "parallel", "arbitrary"))


def _sb_decode_kernel(q_ref, bias_ref, *refs, n_steps, pages):
    k_refs, v_refs = refs[:pages], refs[pages:2 * pages]
    o_ref, carry_ref, acc_ref = refs[2 * pages:]
    p = pl.program_id(1)

    @pl.when(p == 0)
    def _():
        carry_ref[...] = jnp.zeros_like(carry_ref)
        acc_ref[...] = jnp.zeros_like(acc_ref)

    q = (q_ref[...] * SB_SCALE).astype(BF16)
    bias = bias_ref[...]
    z = jnp.concatenate([_dot(q, k_ref[...].reshape(W_MIX, PAGE_SIZE)) + bias for k_ref in k_refs], axis=0)
    log_fail = -_softplus(z)
    row = lax.broadcasted_iota(jnp.int32, (PAGE_SIZE, PAGE_SIZE), 0)
    col = lax.broadcasted_iota(jnp.int32, (PAGE_SIZE, PAGE_SIZE), 1)
    upper = (row > col).astype(BF16)
    later = _dot(log_fail, upper)
    total = later[:, 0:1] + log_fail[:, 0:1]
    arg = z + log_fail + later
    carry = carry_ref[...]
    acc = acc_ref[...]
    for i in reversed(range(pages)):
        rows = slice(i * SB_HEADS, (i + 1) * SB_HEADS)
        wts = jnp.exp(arg[rows] + carry)
        acc = acc + _dot_nt(wts, v_refs[i][...].reshape(W_MIX, PAGE_SIZE))
        carry = carry + total[rows]
    carry_ref[...] = carry
    acc_ref[...] = acc

    @pl.when(p == n_steps - 1)
    def _():
        head = lax.broadcasted_iota(jnp.int32, acc.shape, 0)
        col_head = lax.broadcasted_iota(jnp.int32, acc.shape, 1) // SB_HEAD_DIM
        o_ref[...] = jnp.sum(jnp.where(head == col_head, acc, 0.0), axis=0, keepdims=True)


def sb_decode(l, q_blocks, sb_bias_col, cache_kt, cache_vt, page_table):
    b = q_blocks.shape[0]
    n_pages = page_table.shape[1]
    pages = DECODE_PAGES if n_pages % DECODE_PAGES == 0 else 1
    n_steps = n_pages // pages

    def page(i):
        return pl.BlockSpec((None, None, SB_HEADS, SB_HEAD_DIM, PAGE_SIZE),
                            lambda b, p, l, pt: (l[0], pt[b, n_pages - pages * (p + 1) + i], 0, 0, 0))

    return _call(
        functools.partial(_sb_decode_kernel, n_steps=n_steps, pages=pages), l, (b, n_steps),
        [pl.BlockSpec((None, SB_HEADS, W_MIX), lambda b, p, l, pt: (b, 0, 0)),
         pl.BlockSpec((None, SB_HEADS, 1), lambda b, p, l, pt: (l[0], 0, 0))]
        + [page(i) for i in range(pages)] * 2,
        pl.BlockSpec((None, 1, W_MIX), lambda b, p, l, pt: (b, 0, 0)),
        jax.ShapeDtypeStruct((b, 1, W_MIX), F32),
        (q_blocks, sb_bias_col) + (cache_kt,) * pages + (cache_vt,) * pages,
        scratch=[pltpu.VMEM((SB_HEADS, 1), F32), pltpu.VMEM((SB_HEADS, W_MIX), F32)],
        dims=("parallel", "arbitrary"), prefetch=(page_table,))


def _wkv_inputs(pr_ref, pk_ref, pv_ref, pl_ref, sh0_ref, mu_ref, w0_ref, dup_ref, a0_ref, aup_ref,
                gup_ref, ext_ref, *, tt, t_valid):
    i = pl.program_id(1)

    @pl.when(i == 0)
    def _():
        ext_ref[0:SHIFT_HALO, :] = sh0_ref[...]

    @pl.when(i > 0)
    def _():
        ext_ref[0:SHIFT_HALO, :] = ext_ref[tt:tt + SHIFT_HALO, :]

    rows = slice(SHIFT_HALO, SHIFT_HALO + tt)
    ext_ref[rows, 0:W_MIX] = pr_ref[...]
    ext_ref[rows, W_MIX:2 * W_MIX] = pk_ref[...]
    ext_ref[rows, 2 * W_MIX:3 * W_MIX] = pv_ref[...]
    ext_ref[rows, 3 * W_MIX:N_D] = pl_ref[...]
    p = ext_ref[rows, :]
    prev = ext_ref[SHIFT_HALO - 1:SHIFT_HALO - 1 + tt, :]
    xs = p + (prev - p) * mu_ref[...]
    o = 3 * W_MIX
    w_dn = xs[:, o:o + DECAY_LORA]
    a_dn = xs[:, o + DECAY_LORA:o + DECAY_LORA + A_LORA]
    g_dn = xs[:, o + DECAY_LORA + A_LORA:N_D]
    w_log = -_softplus(-(w0_ref[...] + _dot(jnp.tanh(w_dn), dup_ref[...]))) - 0.5
    log_decay = -jnp.exp(w_log)
    alpha = _sigmoid(a0_ref[...] + _dot(a_dn, aup_ref[...]))
    k = xs[:, W_MIX:2 * W_MIX]
    v = xs[:, 2 * W_MIX:3 * W_MIX]
    if t_valid is not None:
        valid = (i * tt + lax.broadcasted_iota(jnp.int32, (tt, 1), 0)) < t_valid
        k = jnp.where(valid, k, 0.0)
        v = jnp.where(valid, v, 0.0)
        log_decay = jnp.where(valid, log_decay, 0.0)
    return xs[:, 0:W_MIX], k, v, log_decay, alpha, _dot(_sigmoid(g_dn), gup_ref[...])


def _bdot(a, b):
    return lax.dot_general(a.astype(BF16), b.astype(BF16), (((2,), (1,)), ((0,), (0,))),
                           preferred_element_type=F32)


def _bdot_nt(a, b):
    return lax.dot_general(a.astype(BF16), b.astype(BF16), (((2,), (2,)), ((0,), (0,))),
                           preferred_element_type=F32)


def _bdot_tn(a, b):
    return lax.dot_general(a.astype(BF16), b.astype(BF16), (((1,), (1,)), ((0,), (0,))),
                           preferred_element_type=F32)


def _heads(x):
    return jnp.stack([x[:, h * RWKV_HEAD_DIM:(h + 1) * RWKV_HEAD_DIM] for h in range(RWKV_HEADS)], axis=0)


def _wkv_chunk_prepare(r, k, v, ld, al, kk_w, ka_w, rk_w):
    ch = r.shape[0]
    row = lax.broadcasted_iota(jnp.int32, (ch, ch), 0)
    col = lax.broadcasted_iota(jnp.int32, (ch, ch), 1)
    incl = row >= col
    eye = (row == col).astype(F32)
    row2 = lax.broadcasted_iota(jnp.int32, (2 * ch, 2 * ch), 0)
    col2 = lax.broadcasted_iota(jnp.int32, (2 * ch, 2 * ch), 1) & (ch - 1)
    tri = (row2 & (ch - 1)) + (row2 // ch) > col2
    hi, lo = _split_bf16(ld)
    lower = incl.astype(BF16)
    cum = (jnp.dot(lower, hi, preferred_element_type=F32)
           + jnp.dot(lower, lo, preferred_element_type=F32))
    cum_end = cum[ch - 1:ch, :]
    g_in = jnp.exp(cum)
    g_ex = jnp.exp(cum - ld)
    g_inv = jnp.exp(-cum)
    g_end = jnp.exp(cum_end - cum)
    g_tot = jnp.exp(cum_end)
    kk = k * kk_w
    k2 = k * (1.0 + (al - 1.0) * ka_w)
    sums = _heads(jnp.concatenate([kk, r * k2 * rk_w], axis=0))
    kk3 = sums[:, 0:ch]
    unit = 1.0 / jnp.maximum(jnp.sqrt(jnp.sum(kk3 * kk3, axis=-1, keepdims=True)), 1e-12)
    rk_sum = jnp.sum(sums[:, ch:2 * ch], axis=-1, keepdims=True)

    def scaled(top, bottom):
        x = _heads(jnp.concatenate([top, bottom], axis=0))
        return jnp.concatenate([x[:, 0:ch] * unit, x[:, ch:2 * ch]], axis=1)

    b = kk * al
    a2 = scaled(-kk * g_ex, r * g_in)
    bm = scaled(b * g_inv, k2 * g_inv)
    bk_end = scaled(b * g_end, k2 * g_end)
    v3 = _heads(v)
    prod = jnp.where(tri, _bdot_nt(a2, bm), 0.0)
    x = prod[:, 0:ch, 0:ch]
    inv = eye + x
    n_sq = ch.bit_length() - 2
    for _ in range(n_sq):
        x = _bdot(x, x)
        inv = inv + _bdot(inv, x)
    ak_v = _bdot(prod[:, 0:ch], jnp.concatenate([jnp.zeros_like(v3), v3], axis=1))
    return dict(a2=a2, prod_r=prod[:, ch:2 * ch], inv=inv, ak_v=ak_v, v3=v3, bk_end=bk_end,
                g_tot=_heads(g_tot), bonus=rk_sum * v3)


def _wkv_chunk_advance(p, s, g, ln_g, ln_b):
    ch = p['inv'].shape[1]
    u = _bdot(p['inv'], _bdot_nt(p['a2'][:, 0:ch], s) + p['ak_v'])
    uv = jnp.concatenate([u, p['v3']], axis=1)
    y3 = _bdot_nt(p['a2'][:, ch:2 * ch], s) + _bdot(p['prod_r'], uv)
    s_new = s * p['g_tot'] + _bdot_tn(uv, p['bk_end'])
    mu = jnp.mean(y3, axis=-1, keepdims=True)
    yc = y3 - mu
    var = jnp.mean(yc * yc, axis=-1, keepdims=True)
    out3 = yc * lax.rsqrt(var + GN_EPS) * _heads(ln_g) + _heads(ln_b) + p['bonus']
    return jnp.concatenate([out3[h] for h in range(RWKV_HEADS)], axis=1) * g, s_new


def _wkv_kernel(pr_ref, pk_ref, pv_ref, pl_ref, sh0_ref, mu_ref, w0_ref, dup_ref, a0_ref, aup_ref, gup_ref,
                s0_ref, kk_ref, ka_ref, rk_ref, lg_ref, lb_ref, y_ref, so_ref, s_ref, ext_ref,
                *, ch, n_steps, t_valid):
    c = pl.program_id(1)

    @pl.when(c == 0)
    def _():
        s_ref[...] = s0_ref[...]

    tt = pr_ref.shape[0]
    r, k, v, ld, al, g = _wkv_inputs(pr_ref, pk_ref, pv_ref, pl_ref, sh0_ref, mu_ref, w0_ref, dup_ref,
                                     a0_ref, aup_ref, gup_ref, ext_ref, tt=tt, t_valid=t_valid)
    chunks = [slice(j * ch, (j + 1) * ch) for j in range(tt // ch)]
    prepared = [_wkv_chunk_prepare(r[rows], k[rows], v[rows], ld[rows], al[rows],
                                   kk_ref[...], ka_ref[...], rk_ref[...]) for rows in chunks]
    s = s_ref[...]
    for rows, p in zip(chunks, prepared):
        y, s = _wkv_chunk_advance(p, s, g[rows], lg_ref[...], lb_ref[...])
        y_ref[rows, :] = y.astype(y_ref.dtype)
    s_ref[...] = s

    @pl.when(c == n_steps - 1)
    def _():
        so_ref[...] = s


def wkv_mixer(l, proj3, shift0, s0, lw, t_valid):
    b, t, _ = proj3.shape
    ch = min(WKV_CHUNK, t)
    rows = ch * WKV_CHUNKS_PER_STEP if t % (ch * WKV_CHUNKS_PER_STEP) == 0 else ch
    cd = OFF_D // W_MIX
    col = lambda j: pl.BlockSpec((None, rows, W_MIX), lambda b, c, l: (b, c, cd + j))
    st = pl.BlockSpec((None, RWKV_HEADS, RWKV_HEAD_DIM, RWKV_HEAD_DIM), lambda b, c, l: (b, 0, 0, 0))
    vec = lambda n: pl.BlockSpec((None, 1, n), lambda b, c, l: (l[0], 0, 0))
    up = lambda n: pl.BlockSpec((None, n, W_MIX), lambda b, c, l: (l[0], 0, 0))
    return _call(
        functools.partial(_wkv_kernel, ch=ch, n_steps=t // rows, t_valid=t_valid), l, (b, t // rows),
        [col(0), col(1), col(2),
         pl.BlockSpec((None, rows, N_LORA), lambda b, c, l: (b, c, (OFF_D + 3 * W_MIX) // N_LORA)),
         pl.BlockSpec((None, SHIFT_HALO, N_D), lambda b, c, l: (b, 0, 0)),
         vec(N_D), vec(W_MIX), up(DECAY_LORA), vec(W_MIX), up(A_LORA), up(GATE_LORA), st]
        + [vec(W_MIX)] * 5,
        [pl.BlockSpec((None, rows, W_MIX), lambda b, c, l: (b, c, 0)), st],
        [jax.ShapeDtypeStruct((b, t, W_MIX), BF16),
         jax.ShapeDtypeStruct((b, RWKV_HEADS, RWKV_HEAD_DIM, RWKV_HEAD_DIM), F32)],
        (proj3, proj3, proj3, proj3, shift0, lw['shift_mu'], lw['decay_w0'], lw['decay_up'], lw['a0'],
         lw['a_up'], lw['g_up'], s0, lw['k_k'], lw['k_a'], lw['r_k'], lw['lnx_g'], lw['lnx_b']),
        scratch=[pltpu.VMEM((RWKV_HEADS, RWKV_HEAD_DIM, RWKV_HEAD_DIM), F32),
                 pltpu.VMEM((SHIFT_HALO + rows, N_D), F32)],
        dims=("parallel", "arbitrary"))


def _split_heads_kernel(k_ref, v_ref, ko_ref, vo_ref):
    ko_ref[...] = k_ref[...].T.reshape(ko_ref.shape)
    vo_ref[...] = v_ref[...].T.reshape(vo_ref.shape)


def split_heads(l, proj3):
    b, t, _ = proj3.shape
    tt = min(512, t)
    col = lambda j: pl.BlockSpec((None, tt, W_MIX), lambda b, i, l: (b, i, OFF_C // W_MIX + j))
    out = pl.BlockSpec((None, SB_HEADS, SB_HEAD_DIM, tt), lambda b, i, l: (b, 0, 0, i))
    shape = jax.ShapeDtypeStruct((b, SB_HEADS, SB_HEAD_DIM, t), F32)
    return _call(_split_heads_kernel, l, (b, t // tt), [col(1), col(2)], [out, out], [shape, shape],
                 (proj3, proj3), dims=("parallel", "parallel"))


def _mixers(l, proj3, pool0, conv0, shift0, wkv0, pos0, t_valid, attend, lw):
    b, t, _ = proj3.shape
    y_a = pool_mixer(l, proj3, pool0, lw['w_pool'], lw['pool_scale'], pos0)
    y_b, glu = conv_mixer(l, proj3, conv0, lw['conv_w'], lw['conv_b'], lw['ln_g'], lw['ln_b'])
    y_c = attend(l, proj3)
    y_d, wkv_new = wkv_mixer(l, proj3, shift0, wkv0, lw, t_valid)
    return [y.reshape(b * t, W_MIX) for y in (y_a, y_b, y_c, y_d)], glu, wkv_new


def _trunk_layer(l, xp, xs, mix_p, mix_s, lw):
    dp, ds = xp.shape, xs.shape
    xp, xs = xp.reshape(-1, dp[2]), xs.reshape(-1, ds[2])

    def ffn(xp, xs, g, w_a, w_b):
        act_p, act_s = ffn_in(l, rmsnorm(l, xp, g, BF16), rmsnorm(l, xs, g, BF16), w_a)
        w_b16 = lax.dynamic_index_in_dim(w_b, l[0], 0, keepdims=True).astype(BF16)
        return matmul(l, act_p, act_s, w_b16, tn=512, res=xp, res_s=xs, scale=FFN_RESIDUAL)

    xp, xs = ffn(xp, xs, lw['g_ffn1'], lw['w_ffn1_in'], lw['w_ffn1_out'])
    hp, hs = rmsnorm(l, xp, lw['g_mix'], BF16), rmsnorm(l, xs, lw['g_mix'], BF16)
    proj_p, proj_s = matmul(l, hp, hs, lw['w_in'], tn=256, n=OFF_G, tm=2048)
    proj_p, proj_s = proj_p.reshape(dp[0], dp[1], OFF_G), proj_s.reshape(ds[0], ds[1], OFF_G)
    ys_p, glu_p, wkv_p = mix_p(l, proj_p)
    ys_s, glu_s, wkv_s = mix_s(l, proj_s)
    merged_p, merged_s = merge(l, hp, ys_p, hs, ys_s, lw['w_in'], lw['w_branch'])
    xp, xs = matmul(l, merged_p, merged_s, lw['w_out'], tn=256, res=xp, res_s=xs, scale=1.0, tm=2048)
    xp, xs = ffn(xp, xs, lw['g_ffn2'], lw['w_ffn2_in'], lw['w_ffn2_out'])
    return xp.reshape(dp), xs.reshape(ds), (proj_p, glu_p, wkv_p), (proj_s, glu_s, wkv_s)


def _front_pad(rows, height):
    return jnp.pad(rows, ((0, 0), (height - rows.shape[1], 0), (0, 0)))


def kernel(x_prompt, x_sample, cache_k, cache_v, state_pool, state_conv, state_shift, state_wkv,
           page_table, g_ffn1, w_ffn1_in, w_ffn1_out, g_mix, w_in, w_pool, pool_scale, conv_w,
           conv_b, ln_g, ln_b, sb_bias, shift_mu, decay_w0, decay_up, a0, a_up, g_up, k_k, k_a, r_k,
           lnx_g, lnx_b, w_branch, w_out, g_ffn2, w_ffn2_in, w_ffn2_out, g_final):
    depth = w_in.shape[0]
    bp, tp, d = x_prompt.shape
    bs, ts, _ = x_sample.shape
    assert d == D_MODEL and ts == 1
    past_len = page_table.shape[1] * PAGE_SIZE
    row = lambda a: a.reshape(depth, 1, -1)
    lw = {
        'g_ffn1': row(g_ffn1), 'w_ffn1_in': w_ffn1_in, 'w_ffn1_out': w_ffn1_out, 'g_mix': row(g_mix),
        'w_in': w_in, 'w_pool': w_pool, 'pool_scale': row(pool_scale), 'conv_w': conv_w,
        'conv_b': row(conv_b), 'ln_g': row(ln_g), 'ln_b': row(ln_b), 'sb_bias': row(sb_bias),
        'shift_mu': row(shift_mu), 'decay_w0': row(decay_w0), 'decay_up': decay_up, 'a0': row(a0),
        'a_up': a_up, 'g_up': g_up, 'k_k': row(k_k), 'k_a': row(k_a), 'r_k': row(r_k),
        'lnx_g': row(lnx_g), 'lnx_b': row(lnx_b), 'w_branch': w_branch, 'w_out': w_out,
        'g_ffn2': row(g_ffn2), 'w_ffn2_in': w_ffn2_in, 'w_ffn2_out': w_ffn2_out,
    }
    sb_bias_col = sb_bias.reshape(depth, SB_HEADS, 1)
    cache_kt = jnp.transpose(cache_k, (0, 1, 3, 4, 2))
    cache_vt = jnp.transpose(cache_v, (0, 1, 3, 4, 2))
    own_head = (jnp.arange(W_MIX)[None, :] // SB_HEAD_DIM) == jnp.arange(SB_HEADS)[:, None]
    xs0 = jnp.pad(x_sample, ((0, 0), (0, SAMPLE_ROWS - ts), (0, 0)))

    def layer(carry, li):
        xp, xs = carry
        l = li.reshape(1)
        pool_s = lax.dynamic_index_in_dim(state_pool, li, 0, keepdims=False)
        conv_s = lax.dynamic_index_in_dim(state_conv, li, 0, keepdims=False)
        shift_s = lax.dynamic_index_in_dim(state_shift, li, 0, keepdims=False)
        wkv_s = lax.dynamic_index_in_dim(state_wkv, li, 0, keepdims=False)

        def attend_sample(l, p3):
            q_blocks = jnp.where(own_head, p3[:, 0:1, OFF_C:OFF_C + W_MIX], 0.0)
            y = sb_decode(l, q_blocks, sb_bias_col, cache_kt, cache_vt, page_table)
            return jnp.pad(y, ((0, 0), (0, SAMPLE_ROWS - 1), (0, 0))).astype(BF16)

        mix_p = lambda l, p3: _mixers(
            l, p3, jnp.zeros((bp, POOL_HALO, W_MIX), F32), jnp.zeros((bp, CONV_HALO, W_MIX), F32),
            jnp.zeros((bp, SHIFT_HALO, N_D), F32),
            jnp.zeros((bp, RWKV_HEADS, RWKV_HEAD_DIM, RWKV_HEAD_DIM), F32), 0, None,
            lambda l, p3: sb_prompt(l, p3, lw['sb_bias']), lw)
        mix_s = lambda l, p3: _mixers(
            l, p3, _front_pad(pool_s, POOL_HALO), _front_pad(conv_s, CONV_HALO),
            _front_pad(shift_s, SHIFT_HALO), wkv_s, past_len, ts, attend_sample, lw)
        xp, xs, (proj_p, glu_p, wkv_p), (proj_s, glu_s, wkv_s_new) = _trunk_layer(l, xp, xs, mix_p, mix_s, lw)

        k_p, v_p = split_heads(l, proj_p)
        k_s, v_s = (proj_s[:, :ts, OFF_C + j * W_MIX:OFF_C + (j + 1) * W_MIX].reshape(
            bs, ts, SB_HEADS, SB_HEAD_DIM) for j in (1, 2))
        outs = (
            k_p, v_p, proj_p[:, tp - POOL_BUF:, OFF_A:OFF_A + W_MIX], glu_p[:, tp - CONV_BUF:],
            proj_p[:, tp - 1:, OFF_D:OFF_D + N_D], wkv_p,
            k_s, v_s,
            jnp.concatenate([pool_s, proj_s[:, :ts, OFF_A:OFF_A + W_MIX]], axis=1)[:, -POOL_BUF:],
            jnp.concatenate([conv_s, glu_s[:, :ts]], axis=1)[:, -CONV_BUF:],
            proj_s[:, ts - 1:ts, OFF_D:OFF_D + N_D], wkv_s_new)
        return (xp, xs), outs

    (xp, xs), st = lax.scan(layer, (x_prompt, xs0), jnp.arange(depth, dtype=jnp.int32))
    zero = jnp.zeros((1,), jnp.int32)
    g_fin = g_final.reshape(1, 1, d)
    y_prompt = rmsnorm(zero, xp.reshape(bp * tp, d), g_fin, F32).reshape(bp, tp, d)
    y_sample = rmsnorm(zero, xs.reshape(bs * SAMPLE_ROWS, d), g_fin, F32).reshape(bs, SAMPLE_ROWS, d)[:, :ts]
    to_cache_layout = lambda a: jnp.transpose(a, (0, 1, 4, 2, 3))
    return (y_prompt, y_sample, to_cache_layout(st[0]), to_cache_layout(st[1])) + tuple(st[2:])
```

```python
import functools

import jax
import jax.numpy as jnp
from jax import lax
from jax.experimental import pallas as pl
from jax.experimental.pallas import tpu as pltpu

F32 = jnp.float32
BF16 = jnp.bfloat16
LOG2E = 1.4426950408889634

LANES = 128
SUBLANES = 8
VMEM_LIMIT_BYTES = 56 * 1024 * 1024

D_MODEL = 2048
W_MIX = D_MODEL // 4
POOL_WINDOWS = (2, 4, 8, 16)
POOL_GROUP = W_MIX // len(POOL_WINDOWS)
POOL_BUF = max(POOL_WINDOWS) - 1
CONV_WIDTH = 31
CONV_BUF = CONV_WIDTH - 1
SB_HEADS = 8
SB_HEAD_DIM = W_MIX // SB_HEADS
SB_SCALE = SB_HEAD_DIM ** -0.5
RWKV_HEAD_DIM = 64
RWKV_HEADS = W_MIX // RWKV_HEAD_DIM
DECAY_LORA = 64
A_LORA = 64
GATE_LORA = 128
N_LORA = DECAY_LORA + A_LORA + GATE_LORA
N_D = 3 * W_MIX + N_LORA
N_BRANCH = 4
OFF_A = 0
OFF_B = OFF_A + W_MIX
OFF_C = OFF_B + 2 * W_MIX
OFF_D = OFF_C + 3 * W_MIX
OFF_G = OFF_D + N_D
N_IN = OFF_G + N_BRANCH * D_MODEL
D_FF = ((8 * D_MODEL // 3 + 127) // 128) * 128
FFN_RESIDUAL = 0.5
RMS_EPS = 1e-6
LN_EPS = 1e-5
GN_EPS = 64e-5
PAGE_SIZE = 128

POOL_HALO = 16
CONV_HALO = 32
SHIFT_HALO = 8
SAMPLE_ROWS = 8
ATTN_BLOCK = 256
SB_GROUP = 4
SB_UNROLL = 4
DECODE_PAGES = 16
FFN_ROW_TILE = 4096
WKV_CHUNK = 64
WKV_CHUNKS_PER_STEP = 4


def _call(kernel, l, grid, in_specs, out_specs, out_shape, args, scratch=(), dims=None, prefetch=()):
    n_pre = 1 + len(prefetch)

    def body(*refs):
        kernel(*refs[n_pre:])

    grid_spec = pltpu.PrefetchScalarGridSpec(
        num_scalar_prefetch=n_pre, grid=grid, in_specs=in_specs, out_specs=out_specs,
        scratch_shapes=list(scratch))
    fn = kernel.func if isinstance(kernel, functools.partial) else kernel
    return pl.pallas_call(
        body, grid_spec=grid_spec, out_shape=out_shape, name=fn.__name__.strip("_"),
        compiler_params=pltpu.CompilerParams(
            dimension_semantics=dims or ("arbitrary",) * len(grid),
            vmem_limit_bytes=VMEM_LIMIT_BYTES),
    )(l, *prefetch, *args)


def _sigmoid(x):
    return 1.0 / (1.0 + jnp.exp(-x))


def _softplus(x):
    return jnp.maximum(x, 0.0) + jnp.log1p(jnp.exp(-jnp.abs(x)))


def _split_bf16(x):
    hi = x.astype(BF16)
    lo = (x - hi.astype(F32)).astype(BF16)
    return hi, lo


def _dot(a, b):
    return jnp.dot(a.astype(BF16), b.astype(BF16), preferred_element_type=F32)


def _dot_nt(a, b):
    return lax.dot_general(a.astype(BF16), b.astype(BF16), (((1,), (1,)), ((), ())),
                           preferred_element_type=F32)


def _rmsnorm_kernel(x_ref, g_ref, o_ref):
    x = x_ref[...]
    y = x * lax.rsqrt(jnp.mean(x * x, axis=-1, keepdims=True) + RMS_EPS)
    o_ref[...] = (y * g_ref[...]).astype(o_ref.dtype)


def rmsnorm(l, x, g, out_dtype):
    m, d = x.shape
    tm = min(512, m)
    return _call(
        _rmsnorm_kernel, l, (m // tm,),
        [pl.BlockSpec((tm, d), lambda i, l: (i, 0)),
         pl.BlockSpec((None, 1, d), lambda i, l: (l[0], 0, 0))],
        pl.BlockSpec((tm, d), lambda i, l: (i, 0)),
        jax.ShapeDtypeStruct((m, d), out_dtype), (x, g), dims=("parallel",))


def _side_spec(rows, cols, col_block):
    if col_block:
        return pl.BlockSpec((rows, cols), lambda i, j, l: (0, j))
    return pl.BlockSpec((rows, cols), lambda i, j, l: (0, 0))


def _side_out(rows, cols):
    return pl.BlockSpec((None, rows, cols), lambda i, j, l: (i, 0, j))


def _on_first_row_tile(out_ref, compute):
    @pl.when(pl.program_id(0) == 0)
    def _():
        out_ref[...] = compute().astype(out_ref.dtype)

    @pl.when(pl.program_id(0) != 0)
    def _():
        out_ref[...] = jnp.zeros_like(out_ref)


def _ffn_in_kernel(x_ref, wg_ref, wu_ref, xs_ref, wn_ref, o_ref, os_ref, wn16_ref):
    w = jnp.concatenate([wg_ref[...], wu_ref[...]], axis=1).astype(BF16)

    def act(x):
        r = jnp.dot(x, w, preferred_element_type=F32)
        g = r[:, :LANES]
        return g * _sigmoid(g) * r[:, LANES:]

    o_ref[...] = act(x_ref[...]).astype(o_ref.dtype)
    _on_first_row_tile(os_ref, lambda: act(xs_ref[...]))
    wn16_ref[...] = wn_ref[...].astype(BF16)


def ffn_in(l, x, xs, w, w_next):
    m, d = x.shape
    ms = xs.shape[0]
    d_ff = w.shape[2] // 2
    nb = d_ff // LANES
    n_next = w_next.shape[2]
    tm = min(FFN_ROW_TILE, m)
    out, out_s, w_next16 = _call(
        _ffn_in_kernel, l, (m // tm, nb),
        [pl.BlockSpec((tm, d), lambda i, j, l: (i, 0), pipeline_mode=pl.Buffered(1)),
         pl.BlockSpec((None, d, LANES), lambda i, j, l: (l[0], 0, j)),
         pl.BlockSpec((None, d, LANES), lambda i, j, l: (l[0], 0, j + nb)),
         _side_spec(ms, d, False),
         pl.BlockSpec((None, LANES, n_next), lambda i, j, l: (l[0], j, 0))],
        [pl.BlockSpec((tm, LANES), lambda i, j, l: (i, j)), _side_out(ms, LANES),
         pl.BlockSpec((None, LANES, n_next), lambda i, j, l: (i, j, 0))],
        [jax.ShapeDtypeStruct((m, d_ff), BF16), jax.ShapeDtypeStruct((m // tm, ms, d_ff), BF16),
         jax.ShapeDtypeStruct((m // tm, d_ff, n_next), BF16)],
        (x, w, w, xs, w_next), dims=("arbitrary", "arbitrary"))
    return out, out_s[0], w_next16


def _mm_kernel(a_ref, w_ref, as_ref, o_ref, os_ref):
    w = w_ref[...].astype(BF16)
    o_ref[...] = jnp.dot(a_ref[...], w, preferred_element_type=F32)
    _on_first_row_tile(os_ref, lambda: jnp.dot(as_ref[...], w, preferred_element_type=F32))


def _mm_res_kernel(a_ref, w_ref, r_ref, as_ref, rs_ref, o_ref, os_ref, *, scale):
    w = w_ref[...].astype(BF16)
    o_ref[...] = r_ref[...] + scale * jnp.dot(a_ref[...], w, preferred_element_type=F32)
    _on_first_row_tile(
        os_ref, lambda: rs_ref[...] + scale * jnp.dot(as_ref[...], w, preferred_element_type=F32))


def matmul(l, a, a_s, w, *, tn, res=None, res_s=None, scale=1.0, n=None, tm=1024, w_is_layer=False):
    m, k = a.shape
    ms = a_s.shape[0]
    n = w.shape[2] if n is None else n
    tm = min(tm, m)
    a_spec = pl.BlockSpec((tm, k), lambda i, j, l: (i, 0))
    if w_is_layer:
        w_spec = pl.BlockSpec((None, k, tn), lambda i, j, l: (0, 0, j))
    else:
        w_spec = pl.BlockSpec((None, k, tn), lambda i, j, l: (l[0], 0, j))
    o_spec = pl.BlockSpec((tm, tn), lambda i, j, l: (i, j))
    out_specs = [o_spec, _side_out(ms, tn)]
    out_shape = [jax.ShapeDtypeStruct((m, n), F32), jax.ShapeDtypeStruct((m // tm, ms, n), F32)]
    dims = ("arbitrary", "arbitrary")
    if res is None:
        out, out_s = _call(_mm_kernel, l, (m // tm, n // tn), [a_spec, w_spec, _side_spec(ms, k, False)],
                           out_specs, out_shape, (a, w, a_s), dims=dims)
    else:
        out, out_s = _call(
            functools.partial(_mm_res_kernel, scale=scale), l, (m // tm, n // tn),
            [a_spec, w_spec, o_spec, _side_spec(ms, k, False), _side_spec(ms, tn, True)],
            out_specs, out_shape, (a, w, res, a_s, res_s), dims=dims)
    return out, out_s[0]


def _merge_kernel(h_ref, ya_ref, yb_ref, yc_ref, yd_ref, g0_ref, g1_ref, g2_ref, g3_ref, wb_ref,
                  hs_ref, sa_ref, sb_ref, sc_ref, sd_ref, o_ref, os_ref):
    wg = [g_ref[...].astype(BF16) for g_ref in (g0_ref, g1_ref, g2_ref, g3_ref)]
    wb = [wb_ref[n].astype(BF16) for n in range(N_BRANCH)]

    def gated_sum(h, ys):
        acc = None
        for n in range(N_BRANCH):
            gate = jnp.dot(h, wg[n], preferred_element_type=F32)
            t = _sigmoid(gate) * jnp.dot(ys[n], wb[n], preferred_element_type=F32)
            acc = t if acc is None else acc + t
        return acc

    o_ref[...] = gated_sum(h_ref[...], [r[...] for r in (ya_ref, yb_ref, yc_ref, yd_ref)]).astype(o_ref.dtype)
    _on_first_row_tile(os_ref, lambda: gated_sum(hs_ref[...], [r[...] for r in (sa_ref, sb_ref, sc_ref, sd_ref)]))


def merge(l, h, ys, h_s, ys_s, w_in, w_branch):
    m = h.shape[0]
    ms = h_s.shape[0]
    tn = 256
    tm = min(1024, m)
    nj = D_MODEL // tn
    y_spec = pl.BlockSpec((tm, W_MIX), lambda i, j, l: (i, 0))
    g_specs = [pl.BlockSpec((None, D_MODEL, tn), functools.partial(
        lambda i, j, l, n: (l[0], 0, OFF_G // tn + n * nj + j), n=n)) for n in range(N_BRANCH)]
    out, out_s = _call(
        _merge_kernel, l, (m // tm, nj),
        [pl.BlockSpec((tm, D_MODEL), lambda i, j, l: (i, 0))] + [y_spec] * N_BRANCH + g_specs
        + [pl.BlockSpec((None, N_BRANCH, W_MIX, tn), lambda i, j, l: (l[0], 0, 0, j))]
        + [_side_spec(ms, D_MODEL, False)] + [_side_spec(ms, W_MIX, False)] * N_BRANCH,
        [pl.BlockSpec((tm, tn), lambda i, j, l: (i, j)), _side_out(ms, tn)],
        [jax.ShapeDtypeStruct((m, D_MODEL), BF16),
         jax.ShapeDtypeStruct((m // tm, ms, D_MODEL), BF16)],
        (h, *ys, w_in, w_in, w_in, w_in, w_branch, h_s, *ys_s), dims=("arbitrary", "arbitrary"))
    return out, out_s[0]


def _pool_kernel(u_ref, t0_ref, wp_ref, ps_ref, y_ref, ext_ref, *, tt, pos0):
    i = pl.program_id(1)

    @pl.when(i == 0)
    def _():
        ext_ref[0:POOL_HALO, :] = t0_ref[...]

    @pl.when(i > 0)
    def _():
        ext_ref[0:POOL_HALO, :] = ext_ref[tt:tt + POOL_HALO, :]

    u = u_ref[...]
    ext_ref[POOL_HALO:POOL_HALO + tt, :] = u
    pos = pos0 + i * tt + lax.broadcasted_iota(jnp.int32, (tt, 1), 0)
    ys = []
    for gi, win in enumerate(POOL_WINDOWS):
        cs = slice(gi * POOL_GROUP, (gi + 1) * POOL_GROUP)
        wsum = u[:, cs]
        for k in range(1, win):
            wsum = wsum + ext_ref[POOL_HALO - k:POOL_HALO - k + tt, cs]
        count = jnp.minimum(win, pos + 1).astype(F32)
        d = wsum / count - u[:, cs]
        ys.append(_dot(d, wp_ref[gi]))
    y_ref[...] = (jnp.concatenate(ys, axis=1) * ps_ref[...]).astype(y_ref.dtype)


def pool_mixer(l, proj3, tail0, w_pool, pool_scale, pos0):
    b, t, _ = proj3.shape
    tt = min(512, t)
    return _call(
        functools.partial(_pool_kernel, tt=tt, pos0=pos0), l, (b, t // tt),
        [pl.BlockSpec((None, tt, W_MIX), lambda b, i, l: (b, i, OFF_A // W_MIX)),
         pl.BlockSpec((None, POOL_HALO, W_MIX), lambda b, i, l: (b, 0, 0)),
         pl.BlockSpec((None, len(POOL_WINDOWS), POOL_GROUP, POOL_GROUP), lambda b, i, l: (l[0], 0, 0, 0)),
         pl.BlockSpec((None, 1, W_MIX), lambda b, i, l: (l[0], 0, 0))],
        pl.BlockSpec((None, tt, W_MIX), lambda b, i, l: (b, i, 0)),
        jax.ShapeDtypeStruct((b, t, W_MIX), BF16), (proj3, tail0, w_pool, pool_scale),
        scratch=[pltpu.VMEM((POOL_HALO + tt, W_MIX), F32)], dims=("parallel", "arbitrary"))


def _conv_kernel(val_ref, gate_ref, t0_ref, w_ref, cb_ref, lg_ref, lb_ref, y_ref, u_ref, ext_ref, sh_ref,
                 *, tt):
    i = pl.program_id(1)

    @pl.when(i == 0)
    def _():
        ext_ref[0:CONV_HALO, :] = t0_ref[...]

    @pl.when(i > 0)
    def _():
        ext_ref[0:CONV_HALO, :] = ext_ref[tt:tt + CONV_HALO, :]

    u = val_ref[...] * _sigmoid(gate_ref[...])
    u_ref[...] = u
    ext_ref[CONV_HALO:CONV_HALO + tt, :] = u
    first = CONV_HALO - CONV_BUF
    last = first + CONV_WIDTH - 1
    for b in range(SUBLANES):
        top = max(s for s in range(first, last + 1) if s % SUBLANES == b)
        sh_ref[b, 0:top - b + tt, :] = ext_ref[b:top + tt, :]
    acc = cb_ref[...]
    for j in range(CONV_WIDTH):
        s = first + j
        base = s - s % SUBLANES
        acc = acc + sh_ref[s % SUBLANES, base:base + tt, :] * w_ref[j:j + 1, :]
    y = acc
    mu = jnp.mean(y, axis=-1, keepdims=True)
    yc = y - mu
    var = jnp.mean(yc * yc, axis=-1, keepdims=True)
    yn = yc * lax.rsqrt(var + LN_EPS) * lg_ref[...] + lb_ref[...]
    y_ref[...] = (yn * _sigmoid(yn)).astype(y_ref.dtype)


def conv_mixer(l, proj3, tail0, conv_w, conv_b, ln_g, ln_b):
    b, t, _ = proj3.shape
    tt = min(256, t)
    vec = pl.BlockSpec((None, 1, W_MIX), lambda b, i, l: (l[0], 0, 0))
    blk = pl.BlockSpec((None, tt, W_MIX), lambda b, i, l: (b, i, 0))
    return _call(
        functools.partial(_conv_kernel, tt=tt), l, (b, t // tt),
        [pl.BlockSpec((None, tt, W_MIX), lambda b, i, l: (b, i, OFF_B // W_MIX)),
         pl.BlockSpec((None, tt, W_MIX), lambda b, i, l: (b, i, OFF_B // W_MIX + 1)),
         pl.BlockSpec((None, CONV_HALO, W_MIX), lambda b, i, l: (b, 0, 0)),
         pl.BlockSpec((None, CONV_WIDTH, W_MIX), lambda b, i, l: (l[0], 0, 0)),
         vec, vec, vec],
        [blk, blk],
        [jax.ShapeDtypeStruct((b, t, W_MIX), BF16), jax.ShapeDtypeStruct((b, t, W_MIX), F32)],
        (proj3, proj3, tail0, conv_w, conv_b, ln_g, ln_b),
        scratch=[pltpu.VMEM((CONV_HALO + tt, W_MIX), F32),
                 pltpu.VMEM((SUBLANES, CONV_HALO + tt, W_MIX), F32)], dims=("parallel", "arbitrary"))


def _sb_heads(x):
    n = x.shape[1] // SB_HEAD_DIM
    return jnp.stack([x[:, h * SB_HEAD_DIM:(h + 1) * SB_HEAD_DIM] for h in range(n)], axis=0)


def _sb_block(q3, k3, v3, bias, carry, acc, neg_upper, mask):
    nh, tq, _ = q3.shape
    tk = k3.shape[1]
    z = _bdot_nt(q3, k3)
    z = jnp.stack([z[h] + bias[h] for h in range(nh)], axis=0)
    fail = jnp.maximum(z, 0.0) + jnp.log(1.0 + jnp.exp2(jnp.abs(z) * (-LOG2E)))
    if mask is not None:
        fail = jnp.where(mask, fail, 0.0)
    log_hit = z - fail
    fail_first = fail[:, :, 0:1]
    later = jnp.dot(fail.astype(BF16).reshape(nh * tq, tk), neg_upper,
                    preferred_element_type=F32).reshape(nh, tq, tk)
    wts = jnp.exp(log_hit + later + carry)
    if mask is not None:
        wts = jnp.where(mask, wts, 0.0)
    acc = acc + _bdot(wts, v3)
    carry = carry + later[:, :, 0:1] - fail_first
    return carry, acc


def _sb_prompt_kernel(bias_ref, q_ref, k_ref, v_ref, o_ref, kb_ref, vb_ref, *, blk, n_blk):
    qi = pl.program_id(2)
    grp = pl.program_id(1)

    @pl.when(qi == 0)
    def _():
        def fill(j, _):
            rows = pl.ds(pl.multiple_of(j * blk, blk), blk)
            kb_ref[:, rows, :] = _sb_heads(k_ref[rows, :].astype(BF16))
            vb_ref[:, rows, :] = _sb_heads(v_ref[rows, :].astype(BF16))
            return 0
        lax.fori_loop(0, n_blk, fill, 0)

    row = lax.broadcasted_iota(jnp.int32, (blk, blk), 0)
    col = lax.broadcasted_iota(jnp.int32, (blk, blk), 1)
    causal = row > col
    neg_upper = jnp.where(causal, -1.0, 0.0).astype(BF16)
    q3 = _sb_heads((q_ref[...] * SB_SCALE).astype(BF16))
    nh = q3.shape[0]
    bias = [bias_ref[0, grp * nh + h] for h in range(nh)]

    def kv(j):
        rows = pl.ds(pl.multiple_of(j * blk, blk), blk)
        return kb_ref[:, rows, :], vb_ref[:, rows, :]

    carry = jnp.zeros((nh, blk, 1), F32)
    acc = jnp.zeros((nh, blk, SB_HEAD_DIM), F32)
    carry, acc = _sb_block(q3, *kv(qi), bias, carry, acc, neg_upper, causal)

    def step(j, state):
        return _sb_block(q3, *kv(j), bias, state[0], state[1], neg_upper, None)

    rem = qi % SB_UNROLL
    carry, acc = lax.fori_loop(0, rem, lambda it, state: step(qi - 1 - it, state), (carry, acc))

    def body(it, state):
        j = qi - 1 - rem - SB_UNROLL * it
        for u in range(SB_UNROLL):
            state = step(j - u, state)
        return state

    carry, acc = lax.fori_loop(0, qi // SB_UNROLL, body, (carry, acc))
    o_ref[...] = jnp.concatenate([acc[h] for h in range(nh)], axis=1).astype(o_ref.dtype)


def sb_prompt(l, proj3, sb_bias):
    b, t, _ = proj3.shape
    blk = min(ATTN_BLOCK, t)
    wid = SB_GROUP * SB_HEAD_DIM
    ngrp = W_MIX // wid
    c0 = OFF_C // wid
    return _call(
        functools.partial(_sb_prompt_kernel, blk=blk, n_blk=t // blk), l, (b, ngrp, t // blk),
        [pl.BlockSpec((None, 1, SB_HEADS), lambda b, h, i, l: (l[0], 0, 0), memory_space=pltpu.SMEM),
         pl.BlockSpec((None, blk, wid), lambda b, h, i, l: (b, i, c0 + h)),
         pl.BlockSpec((None, t, wid), lambda b, h, i, l: (b, 0, c0 + ngrp + h)),
         pl.BlockSpec((None, t, wid), lambda b, h, i, l: (b, 0, c0 + 2 * ngrp + h))],
        pl.BlockSpec((None, blk, wid), lambda b, h, i, l: (b, i, h)),
        jax.ShapeDtypeStruct((b, t, W_MIX), BF16), (sb_bias, proj3, proj3, proj3),
        scratch=[pltpu.VMEM((SB_GROUP, t, SB_HEAD_DIM), BF16)] * 2,
        dims=("parallel", "parallel", "arbitrary"))


def _sb_decode_kernel(q_ref, bias_ref, *refs, n_steps, pages):
    k_refs, v_refs = refs[:pages], refs[pages:2 * pages]
    o_ref, carry_ref, acc_ref = refs[2 * pages:]
    p = pl.program_id(1)

    @pl.when(p == 0)
    def _():
        carry_ref[...] = jnp.zeros_like(carry_ref)
        acc_ref[...] = jnp.zeros_like(acc_ref)

    q = (q_ref[...] * SB_SCALE).astype(BF16)
    bias = bias_ref[...]
    z = jnp.concatenate([_dot(q, k_ref[...].reshape(W_MIX, PAGE_SIZE)) + bias for k_ref in k_refs], axis=0)
    log_fail = -_softplus(z)
    row = lax.broadcasted_iota(jnp.int32, (PAGE_SIZE, PAGE_SIZE), 0)
    col = lax.broadcasted_iota(jnp.int32, (PAGE_SIZE, PAGE_SIZE), 1)
    upper = (row > col).astype(BF16)
    later = _dot(log_fail, upper)
    total = later[:, 0:1] + log_fail[:, 0:1]
    arg = z + log_fail + later
    carry = carry_ref[...]
    acc = acc_ref[...]
    for i in reversed(range(pages)):
        rows = slice(i * SB_HEADS, (i + 1) * SB_HEADS)
        wts = jnp.exp(arg[rows] + carry)
        acc = acc + _dot_nt(wts, v_refs[i][...].reshape(W_MIX, PAGE_SIZE))
        carry = carry + total[rows]
    carry_ref[...] = carry
    acc_ref[...] = acc

    @pl.when(p == n_steps - 1)
    def _():
        head = lax.broadcasted_iota(jnp.int32, acc.shape, 0)
        col_head = lax.broadcasted_iota(jnp.int32, acc.shape, 1) // SB_HEAD_DIM
        o_ref[...] = jnp.sum(jnp.where(head == col_head, acc, 0.0), axis=0, keepdims=True)


def sb_decode(l, q_blocks, sb_bias_col, cache_kt, cache_vt, page_table):
    b = q_blocks.shape[0]
    n_pages = page_table.shape[1]
    pages = DECODE_PAGES if n_pages % DECODE_PAGES == 0 else 1
    n_steps = n_pages // pages

    def page(i):
        return pl.BlockSpec((None, None, SB_HEADS, SB_HEAD_DIM, PAGE_SIZE),
                            lambda b, p, l, pt: (l[0], pt[b, n_pages - pages * (p + 1) + i], 0, 0, 0))

    return _call(
        functools.partial(_sb_decode_kernel, n_steps=n_steps, pages=pages), l, (b, n_steps),
        [pl.BlockSpec((None, SB_HEADS, W_MIX), lambda b, p, l, pt: (b, 0, 0)),
         pl.BlockSpec((None, SB_HEADS, 1), lambda b, p, l, pt: (l[0], 0, 0))]
        + [page(i) for i in range(pages)] * 2,
        pl.BlockSpec((None, 1, W_MIX), lambda b, p, l, pt: (b, 0, 0)),
        jax.ShapeDtypeStruct((b, 1, W_MIX), F32),
        (q_blocks, sb_bias_col) + (cache_kt,) * pages + (cache_vt,) * pages,
        scratch=[pltpu.VMEM((SB_HEADS, 1), F32), pltpu.VMEM((SB_HEADS, W_MIX), F32)],
        dims=("parallel", "arbitrary"), prefetch=(page_table,))


def _wkv_inputs(pr_ref, pk_ref, pv_ref, pl_ref, sh0_ref, mu_ref, w0_ref, dup_ref, a0_ref, aup_ref,
                gup_ref, ext_ref, *, tt, t_valid):
    i = pl.program_id(1)

    @pl.when(i == 0)
    def _():
        ext_ref[0:SHIFT_HALO, :] = sh0_ref[...]

    @pl.when(i > 0)
    def _():
        ext_ref[0:SHIFT_HALO, :] = ext_ref[tt:tt + SHIFT_HALO, :]

    rows = slice(SHIFT_HALO, SHIFT_HALO + tt)
    ext_ref[rows, 0:W_MIX] = pr_ref[...]
    ext_ref[rows, W_MIX:2 * W_MIX] = pk_ref[...]
    ext_ref[rows, 2 * W_MIX:3 * W_MIX] = pv_ref[...]
    ext_ref[rows, 3 * W_MIX:N_D] = pl_ref[...]
    p = ext_ref[rows, :]
    prev = ext_ref[SHIFT_HALO - 1:SHIFT_HALO - 1 + tt, :]
    xs = p + (prev - p) * mu_ref[...]
    o = 3 * W_MIX
    w_dn = xs[:, o:o + DECAY_LORA]
    a_dn = xs[:, o + DECAY_LORA:o + DECAY_LORA + A_LORA]
    g_dn = xs[:, o + DECAY_LORA + A_LORA:N_D]
    w_log = -_softplus(-(w0_ref[...] + _dot(jnp.tanh(w_dn), dup_ref[...]))) - 0.5
    log_decay = -jnp.exp(w_log)
    alpha = _sigmoid(a0_ref[...] + _dot(a_dn, aup_ref[...]))
    k = xs[:, W_MIX:2 * W_MIX]
    v = xs[:, 2 * W_MIX:3 * W_MIX]
    if t_valid is not None:
        valid = (i * tt + lax.broadcasted_iota(jnp.int32, (tt, 1), 0)) < t_valid
        k = jnp.where(valid, k, 0.0)
        v = jnp.where(valid, v, 0.0)
        log_decay = jnp.where(valid, log_decay, 0.0)
    return xs[:, 0:W_MIX], k, v, log_decay, alpha, _dot(_sigmoid(g_dn), gup_ref[...])


def _bdot(a, b):
    return lax.dot_general(a.astype(BF16), b.astype(BF16), (((2,), (1,)), ((0,), (0,))),
                           preferred_element_type=F32)


def _bdot_nt(a, b):
    return lax.dot_general(a.astype(BF16), b.astype(BF16), (((2,), (2,)), ((0,), (0,))),
                           preferred_element_type=F32)


def _bdot_tn(a, b):
    return lax.dot_general(a.astype(BF16), b.astype(BF16), (((1,), (1,)), ((0,), (0,))),
                           preferred_element_type=F32)


def _heads(x):
    return jnp.stack([x[:, h * RWKV_HEAD_DIM:(h + 1) * RWKV_HEAD_DIM] for h in range(RWKV_HEADS)], axis=0)


def _wkv_chunk_prepare(r, k, v, ld, al, kk_w, ka_w, rk_w):
    ch = r.shape[0]
    row = lax.broadcasted_iota(jnp.int32, (ch, ch), 0)
    col = lax.broadcasted_iota(jnp.int32, (ch, ch), 1)
    incl = row >= col
    eye = (row == col).astype(F32)
    row2 = lax.broadcasted_iota(jnp.int32, (2 * ch, 2 * ch), 0)
    col2 = lax.broadcasted_iota(jnp.int32, (2 * ch, 2 * ch), 1) & (ch - 1)
    tri = (row2 & (ch - 1)) + (row2 // ch) > col2
    hi, lo = _split_bf16(ld)
    lower = incl.astype(BF16)
    cum = (jnp.dot(lower, hi, preferred_element_type=F32)
           + jnp.dot(lower, lo, preferred_element_type=F32))
    cum_end = cum[ch - 1:ch, :]
    g_in = jnp.exp(cum)
    g_ex = jnp.exp(cum - ld)
    g_inv = jnp.exp(-cum)
    g_end = jnp.exp(cum_end - cum)
    g_tot = jnp.exp(cum_end)
    kk = k * kk_w
    k2 = k * (1.0 + (al - 1.0) * ka_w)
    sums = _heads(jnp.concatenate([kk, r * k2 * rk_w], axis=0))
    kk3 = sums[:, 0:ch]
    unit = 1.0 / jnp.maximum(jnp.sqrt(jnp.sum(kk3 * kk3, axis=-1, keepdims=True)), 1e-12)
    rk_sum = jnp.sum(sums[:, ch:2 * ch], axis=-1, keepdims=True)

    def scaled(top, bottom):
        x = _heads(jnp.concatenate([top, bottom], axis=0))
        return jnp.concatenate([x[:, 0:ch] * unit, x[:, ch:2 * ch]], axis=1)

    b = kk * al
    a2 = scaled(-kk * g_ex, r * g_in)
    bm = scaled(b * g_inv, k2 * g_inv)
    bk_end = scaled(b * g_end, k2 * g_end)
    v3 = _heads(v)
    prod = jnp.where(tri, _bdot_nt(a2, bm), 0.0)
    x = prod[:, 0:ch, 0:ch]
    inv = eye + x
    n_sq = ch.bit_length() - 2
    for _ in range(n_sq):
        x = _bdot(x, x)
        inv = inv + _bdot(inv, x)
    ak_v = _bdot(prod[:, 0:ch], jnp.concatenate([jnp.zeros_like(v3), v3], axis=1))
    return dict(a2=a2, prod_r=prod[:, ch:2 * ch], inv=inv, ak_v=ak_v, v3=v3, bk_end=bk_end,
                g_tot=_heads(g_tot), bonus=rk_sum * v3)


def _wkv_chunk_advance(p, s, g, ln_g, ln_b):
    ch = p['inv'].shape[1]
    u = _bdot(p['inv'], _bdot_nt(p['a2'][:, 0:ch], s) + p['ak_v'])
    uv = jnp.concatenate([u, p['v3']], axis=1)
    y3 = _bdot_nt(p['a2'][:, ch:2 * ch], s) + _bdot(p['prod_r'], uv)
    s_new = s * p['g_tot'] + _bdot_tn(uv, p['bk_end'])
    mu = jnp.mean(y3, axis=-1, keepdims=True)
    yc = y3 - mu
    var = jnp.mean(yc * yc, axis=-1, keepdims=True)
    out3 = yc * lax.rsqrt(var + GN_EPS) * _heads(ln_g) + _heads(ln_b) + p['bonus']
    return jnp.concatenate([out3[h] for h in range(RWKV_HEADS)], axis=1) * g, s_new


def _wkv_kernel(pr_ref, pk_ref, pv_ref, pl_ref, sh0_ref, mu_ref, w0_ref, dup_ref, a0_ref, aup_ref, gup_ref,
                s0_ref, kk_ref, ka_ref, rk_ref, lg_ref, lb_ref, y_ref, so_ref, s_ref, ext_ref,
                *, ch, n_steps, t_valid):
    c = pl.program_id(1)

    @pl.when(c == 0)
    def _():
        s_ref[...] = s0_ref[...]

    tt = pr_ref.shape[0]
    r, k, v, ld, al, g = _wkv_inputs(pr_ref, pk_ref, pv_ref, pl_ref, sh0_ref, mu_ref, w0_ref, dup_ref,
                                     a0_ref, aup_ref, gup_ref, ext_ref, tt=tt, t_valid=t_valid)
    chunks = [slice(j * ch, (j + 1) * ch) for j in range(tt // ch)]
    prepared = [_wkv_chunk_prepare(r[rows], k[rows], v[rows], ld[rows], al[rows],
                                   kk_ref[...], ka_ref[...], rk_ref[...]) for rows in chunks]
    s = s_ref[...]
    for rows, p in zip(chunks, prepared):
        y, s = _wkv_chunk_advance(p, s, g[rows], lg_ref[...], lb_ref[...])
        y_ref[rows, :] = y.astype(y_ref.dtype)
    s_ref[...] = s

    @pl.when(c == n_steps - 1)
    def _():
        so_ref[...] = s


def wkv_mixer(l, proj3, shift0, s0, lw, t_valid):
    b, t, _ = proj3.shape
    ch = min(WKV_CHUNK, t)
    rows = ch * WKV_CHUNKS_PER_STEP if t % (ch * WKV_CHUNKS_PER_STEP) == 0 else ch
    cd = OFF_D // W_MIX
    col = lambda j: pl.BlockSpec((None, rows, W_MIX), lambda b, c, l: (b, c, cd + j))
    st = pl.BlockSpec((None, RWKV_HEADS, RWKV_HEAD_DIM, RWKV_HEAD_DIM), lambda b, c, l: (b, 0, 0, 0))
    vec = lambda n: pl.BlockSpec((None, 1, n), lambda b, c, l: (l[0], 0, 0))
    up = lambda n: pl.BlockSpec((None, n, W_MIX), lambda b, c, l: (l[0], 0, 0))
    return _call(
        functools.partial(_wkv_kernel, ch=ch, n_steps=t // rows, t_valid=t_valid), l, (b, t // rows),
        [col(0), col(1), col(2),
         pl.BlockSpec((None, rows, N_LORA), lambda b, c, l: (b, c, (OFF_D + 3 * W_MIX) // N_LORA)),
         pl.BlockSpec((None, SHIFT_HALO, N_D), lambda b, c, l: (b, 0, 0)),
         vec(N_D), vec(W_MIX), up(DECAY_LORA), vec(W_MIX), up(A_LORA), up(GATE_LORA), st]
        + [vec(W_MIX)] * 5,
        [pl.BlockSpec((None, rows, W_MIX), lambda b, c, l: (b, c, 0)), st],
        [jax.ShapeDtypeStruct((b, t, W_MIX), BF16),
         jax.ShapeDtypeStruct((b, RWKV_HEADS, RWKV_HEAD_DIM, RWKV_HEAD_DIM), F32)],
        (proj3, proj3, proj3, proj3, shift0, lw['shift_mu'], lw['decay_w0'], lw['decay_up'], lw['a0'],
         lw['a_up'], lw['g_up'], s0, lw['k_k'], lw['k_a'], lw['r_k'], lw['lnx_g'], lw['lnx_b']),
        scratch=[pltpu.VMEM((RWKV_HEADS, RWKV_HEAD_DIM, RWKV_HEAD_DIM), F32),
                 pltpu.VMEM((SHIFT_HALO + rows, N_D), F32)],
        dims=("parallel", "arbitrary"))


def _split_heads_kernel(k_ref, v_ref, ko_ref, vo_ref):
    ko_ref[...] = k_ref[...].T.reshape(ko_ref.shape)
    vo_ref[...] = v_ref[...].T.reshape(vo_ref.shape)


def split_heads(l, proj3):
    b, t, _ = proj3.shape
    tt = min(512, t)
    col = lambda j: pl.BlockSpec((None, tt, W_MIX), lambda b, i, l: (b, i, OFF_C // W_MIX + j))
    out = pl.BlockSpec((None, SB_HEADS, SB_HEAD_DIM, tt), lambda b, i, l: (b, 0, 0, i))
    shape = jax.ShapeDtypeStruct((b, SB_HEADS, SB_HEAD_DIM, t), F32)
    return _call(_split_heads_kernel, l, (b, t // tt), [col(1), col(2)], [out, out], [shape, shape],
                 (proj3, proj3), dims=("parallel", "parallel"))


def _mixers(l, proj3, pool0, conv0, shift0, wkv0, pos0, t_valid, attend, lw):
    b, t, _ = proj3.shape
    y_a = pool_mixer(l, proj3, pool0, lw['w_pool'], lw['pool_scale'], pos0)
    y_b, glu = conv_mixer(l, proj3, conv0, lw['conv_w'], lw['conv_b'], lw['ln_g'], lw['ln_b'])
    y_c = attend(l, proj3)
    y_d, wkv_new = wkv_mixer(l, proj3, shift0, wkv0, lw, t_valid)
    return [y.reshape(b * t, W_MIX) for y in (y_a, y_b, y_c, y_d)], glu, wkv_new


def _trunk_layer(l, xp, xs, mix_p, mix_s, lw):
    dp, ds = xp.shape, xs.shape
    xp, xs = xp.reshape(-1, dp[2]), xs.reshape(-1, ds[2])

    def ffn(xp, xs, g, w_a, w_b):
        act_p, act_s, w_b16 = ffn_in(l, rmsnorm(l, xp, g, BF16), rmsnorm(l, xs, g, BF16), w_a, w_b)
        return matmul(l, act_p, act_s, w_b16, tn=512, res=xp, res_s=xs, scale=FFN_RESIDUAL, w_is_layer=True)

    xp, xs = ffn(xp, xs, lw['g_ffn1'], lw['w_ffn1_in'], lw['w_ffn1_out'])
    hp, hs = rmsnorm(l, xp, lw['g_mix'], BF16), rmsnorm(l, xs, lw['g_mix'], BF16)
    proj_p, proj_s = matmul(l, hp, hs, lw['w_in'], tn=256, n=OFF_G, tm=2048)
    proj_p, proj_s = proj_p.reshape(dp[0], dp[1], OFF_G), proj_s.reshape(ds[0], ds[1], OFF_G)
    ys_p, glu_p, wkv_p = mix_p(l, proj_p)
    ys_s, glu_s, wkv_s = mix_s(l, proj_s)
    merged_p, merged_s = merge(l, hp, ys_p, hs, ys_s, lw['w_in'], lw['w_branch'])
    xp, xs = matmul(l, merged_p, merged_s, lw['w_out'], tn=512, res=xp, res_s=xs, scale=1.0, tm=2048)
    xp, xs = ffn(xp, xs, lw['g_ffn2'], lw['w_ffn2_in'], lw['w_ffn2_out'])
    return xp.reshape(dp), xs.reshape(ds), (proj_p, glu_p, wkv_p), (proj_s, glu_s, wkv_s)


def _front_pad(rows, height):
    return jnp.pad(rows, ((0, 0), (height - rows.shape[1], 0), (0, 0)))


def kernel(x_prompt, x_sample, cache_k, cache_v, state_pool, state_conv, state_shift, state_wkv,
           page_table, g_ffn1, w_ffn1_in, w_ffn1_out, g_mix, w_in, w_pool, pool_scale, conv_w,
           conv_b, ln_g, ln_b, sb_bias, shift_mu, decay_w0, decay_up, a0, a_up, g_up, k_k, k_a, r_k,
           lnx_g, lnx_b, w_branch, w_out, g_ffn2, w_ffn2_in, w_ffn2_out, g_final):
    depth = w_in.shape[0]
    bp, tp, d = x_prompt.shape
    bs, ts, _ = x_sample.shape
    assert d == D_MODEL and ts == 1
    past_len = page_table.shape[1] * PAGE_SIZE
    row = lambda a: a.reshape(depth, 1, -1)
    lw = {
        'g_ffn1': row(g_ffn1), 'w_ffn1_in': w_ffn1_in, 'w_ffn1_out': w_ffn1_out, 'g_mix': row(g_mix),
        'w_in': w_in, 'w_pool': w_pool, 'pool_scale': row(pool_scale), 'conv_w': conv_w,
        'conv_b': row(conv_b), 'ln_g': row(ln_g), 'ln_b': row(ln_b), 'sb_bias': row(sb_bias),
        'shift_mu': row(shift_mu), 'decay_w0': row(decay_w0), 'decay_up': decay_up, 'a0': row(a0),
        'a_up': a_up, 'g_up': g_up, 'k_k': row(k_k), 'k_a': row(k_a), 'r_k': row(r_k),
        'lnx_g': row(lnx_g), 'lnx_b': row(lnx_b), 'w_branch': w_branch, 'w_out': w_out,
        'g_ffn2': row(g_ffn2), 'w_ffn2_in': w_ffn2_in, 'w_ffn2_out': w_ffn2_out,
    }
    sb_bias_col = sb_bias.reshape(depth, SB_HEADS, 1)
    cache_kt = jnp.transpose(cache_k, (0, 1, 3, 4, 2))
    cache_vt = jnp.transpose(cache_v, (0, 1, 3, 4, 2))
    own_head = (jnp.arange(W_MIX)[None, :] // SB_HEAD_DIM) == jnp.arange(SB_HEADS)[:, None]
    xs0 = jnp.pad(x_sample, ((0, 0), (0, SAMPLE_ROWS - ts), (0, 0)))

    def layer(carry, li):
        xp, xs = carry
        l = li.reshape(1)
        pool_s = lax.dynamic_index_in_dim(state_pool, li, 0, keepdims=False)
        conv_s = lax.dynamic_index_in_dim(state_conv, li, 0, keepdims=False)
        shift_s = lax.dynamic_index_in_dim(state_shift, li, 0, keepdims=False)
        wkv_s = lax.dynamic_index_in_dim(state_wkv, li, 0, keepdims=False)

        def attend_sample(l, p3):
            q_blocks = jnp.where(own_head, p3[:, 0:1, OFF_C:OFF_C + W_MIX], 0.0)
            y = sb_decode(l, q_blocks, sb_bias_col, cache_kt, cache_vt, page_table)
            return jnp.pad(y, ((0, 0), (0, SAMPLE_ROWS - 1), (0, 0))).astype(BF16)

        mix_p = lambda l, p3: _mixers(
            l, p3, jnp.zeros((bp, POOL_HALO, W_MIX), F32), jnp.zeros((bp, CONV_HALO, W_MIX), F32),
            jnp.zeros((bp, SHIFT_HALO, N_D), F32),
            jnp.zeros((bp, RWKV_HEADS, RWKV_HEAD_DIM, RWKV_HEAD_DIM), F32), 0, None,
            lambda l, p3: sb_prompt(l, p3, lw['sb_bias']), lw)
        mix_s = lambda l, p3: _mixers(
            l, p3, _front_pad(pool_s, POOL_HALO), _front_pad(conv_s, CONV_HALO),
            _front_pad(shift_s, SHIFT_HALO), wkv_s, past_len, ts, attend_sample, lw)
        xp, xs, (proj_p, glu_p, wkv_p), (proj_s, glu_s, wkv_s_new) = _trunk_layer(l, xp, xs, mix_p, mix_s, lw)

        k_p, v_p = split_heads(l, proj_p)
        k_s, v_s = (proj_s[:, :ts, OFF_C + j * W_MIX:OFF_C + (j + 1) * W_MIX].reshape(
            bs, ts, SB_HEADS, SB_HEAD_DIM) for j in (1, 2))
        outs = (
            k_p, v_p, proj_p[:, tp - POOL_BUF:, OFF_A:OFF_A + W_MIX], glu_p[:, tp - CONV_BUF:],
            proj_p[:, tp - 1:, OFF_D:OFF_D + N_D], wkv_p,
            k_s, v_s,
            jnp.concatenate([pool_s, proj_s[:, :ts, OFF_A:OFF_A + W_MIX]], axis=1)[:, -POOL_BUF:],
            jnp.concatenate([conv_s, glu_s[:, :ts]], axis=1)[:, -CONV_BUF:],
            proj_s[:, ts - 1:ts, OFF_D:OFF_D + N_D], wkv_s_new)
        return (xp, xs), outs

    (xp, xs), st = lax.scan(layer, (x_prompt, xs0), jnp.arange(depth, dtype=jnp.int32))
    zero = jnp.zeros((1,), jnp.int32)
    g_fin = g_final.reshape(1, 1, d)
    y_prompt = rmsnorm(zero, xp.reshape(bp * tp, d), g_fin, F32).reshape(bp, tp, d)
    y_sample = rmsnorm(zero, xs.reshape(bs * SAMPLE_ROWS, d), g_fin, F32).reshape(bs, SAMPLE_ROWS, d)[:, :ts]
    to_cache_layout = lambda a: jnp.transpose(a, (0, 1, 4, 2, 3))
    return (y_prompt, y_sample, to_cache_layout(st[0]), to_cache_layout(st[1])) + tuple(st[2:])
```

```python
import functools

import jax
import jax.numpy as jnp
from jax import lax
from jax.experimental import pallas as pl
from jax.experimental.pallas import tpu as pltpu

F32 = jnp.float32
BF16 = jnp.bfloat16
LOG2E = 1.4426950408889634

LANES = 128
SUBLANES = 8
VMEM_LIMIT_BYTES = 56 * 1024 * 1024

D_MODEL = 2048
W_MIX = D_MODEL // 4
POOL_WINDOWS = (2, 4, 8, 16)
POOL_GROUP = W_MIX // len(POOL_WINDOWS)
POOL_BUF = max(POOL_WINDOWS) - 1
CONV_WIDTH = 31
CONV_BUF = CONV_WIDTH - 1
SB_HEADS = 8
SB_HEAD_DIM = W_MIX // SB_HEADS
SB_SCALE = SB_HEAD_DIM ** -0.5
RWKV_HEAD_DIM = 64
RWKV_HEADS = W_MIX // RWKV_HEAD_DIM
DECAY_LORA = 64
A_LORA = 64
GATE_LORA = 128
N_LORA = DECAY_LORA + A_LORA + GATE_LORA
N_D = 3 * W_MIX + N_LORA
N_BRANCH = 4
OFF_A = 0
OFF_B = OFF_A + W_MIX
OFF_C = OFF_B + 2 * W_MIX
OFF_D = OFF_C + 3 * W_MIX
OFF_G = OFF_D + N_D
N_IN = OFF_G + N_BRANCH * D_MODEL
D_FF = ((8 * D_MODEL // 3 + 127) // 128) * 128
FFN_RESIDUAL = 0.5
RMS_EPS = 1e-6
LN_EPS = 1e-5
GN_EPS = 64e-5
PAGE_SIZE = 128

POOL_HALO = 16
CONV_HALO = 32
SHIFT_HALO = 8
SAMPLE_ROWS = 8
ATTN_BLOCK = 256
SB_GROUP = 4
SB_UNROLL = 4
DECODE_PAGES = 16
FFN_ROW_TILE = 4096
WKV_CHUNK = 128
WKV_CHUNKS_PER_STEP = 2


def _call(kernel, l, grid, in_specs, out_specs, out_shape, args, scratch=(), dims=None, prefetch=()):
    n_pre = 1 + len(prefetch)

    def body(*refs):
        kernel(*refs[n_pre:])

    grid_spec = pltpu.PrefetchScalarGridSpec(
        num_scalar_prefetch=n_pre, grid=grid, in_specs=in_specs, out_specs=out_specs,
        scratch_shapes=list(scratch))
    fn = kernel.func if isinstance(kernel, functools.partial) else kernel
    return pl.pallas_call(
        body, grid_spec=grid_spec, out_shape=out_shape, name=fn.__name__.strip("_"),
        compiler_params=pltpu.CompilerParams(
            dimension_semantics=dims or ("arbitrary",) * len(grid),
            vmem_limit_bytes=VMEM_LIMIT_BYTES),
    )(l, *prefetch, *args)


def _sigmoid(x):
    return 1.0 / (1.0 + jnp.exp(-x))


def _softplus(x):
    return jnp.maximum(x, 0.0) + jnp.log1p(jnp.exp(-jnp.abs(x)))


def _split_bf16(x):
    hi = x.astype(BF16)
    lo = (x - hi.astype(F32)).astype(BF16)
    return hi, lo


def _dot(a, b):
    return jnp.dot(a.astype(BF16), b.astype(BF16), preferred_element_type=F32)


def _dot_nt(a, b):
    return lax.dot_general(a.astype(BF16), b.astype(BF16), (((1,), (1,)), ((), ())),
                           preferred_element_type=F32)


def _rmsnorm_kernel(x_ref, g_ref, o_ref):
    x = x_ref[...]
    y = x * lax.rsqrt(jnp.mean(x * x, axis=-1, keepdims=True) + RMS_EPS)
    o_ref[...] = (y * g_ref[...]).astype(o_ref.dtype)


def rmsnorm(l, x, g, out_dtype):
    m, d = x.shape
    tm = min(512, m)
    return _call(
        _rmsnorm_kernel, l, (m // tm,),
        [pl.BlockSpec((tm, d), lambda i, l: (i, 0)),
         pl.BlockSpec((None, 1, d), lambda i, l: (l[0], 0, 0))],
        pl.BlockSpec((tm, d), lambda i, l: (i, 0)),
        jax.ShapeDtypeStruct((m, d), out_dtype), (x, g), dims=("parallel",))


def _side_spec(rows, cols, col_block):
    if col_block:
        return pl.BlockSpec((rows, cols), lambda i, j, l: (0, j))
    return pl.BlockSpec((rows, cols), lambda i, j, l: (0, 0))


def _side_out(rows, cols):
    return pl.BlockSpec((None, rows, cols), lambda i, j, l: (i, 0, j))


def _on_first_row_tile(out_ref, compute):
    @pl.when(pl.program_id(0) == 0)
    def _():
        out_ref[...] = compute().astype(out_ref.dtype)

    @pl.when(pl.program_id(0) != 0)
    def _():
        out_ref[...] = jnp.zeros_like(out_ref)


def _ffn_in_kernel(x_ref, wg_ref, wu_ref, xs_ref, wn_ref, o_ref, os_ref, wn16_ref):
    w = jnp.concatenate([wg_ref[...], wu_ref[...]], axis=1).astype(BF16)

    def act(x):
        r = jnp.dot(x, w, preferred_element_type=F32)
        g = r[:, :LANES]
        return g * _sigmoid(g) * r[:, LANES:]

    o_ref[...] = act(x_ref[...]).astype(o_ref.dtype)
    _on_first_row_tile(os_ref, lambda: act(xs_ref[...]))
    wn16_ref[...] = wn_ref[...].astype(BF16)


def ffn_in(l, x, xs, w, w_next):
    m, d = x.shape
    ms = xs.shape[0]
    d_ff = w.shape[2] // 2
    nb = d_ff // LANES
    n_next = w_next.shape[2]
    tm = min(FFN_ROW_TILE, m)
    out, out_s, w_next16 = _call(
        _ffn_in_kernel, l, (m // tm, nb),
        [pl.BlockSpec((tm, d), lambda i, j, l: (i, 0), pipeline_mode=pl.Buffered(1)),
         pl.BlockSpec((None, d, LANES), lambda i, j, l: (l[0], 0, j)),
         pl.BlockSpec((None, d, LANES), lambda i, j, l: (l[0], 0, j + nb)),
         _side_spec(ms, d, False),
         pl.BlockSpec((None, LANES, n_next), lambda i, j, l: (l[0], j, 0))],
        [pl.BlockSpec((tm, LANES), lambda i, j, l: (i, j)), _side_out(ms, LANES),
         pl.BlockSpec((None, LANES, n_next), lambda i, j, l: (i, j, 0))],
        [jax.ShapeDtypeStruct((m, d_ff), BF16), jax.ShapeDtypeStruct((m // tm, ms, d_ff), BF16),
         jax.ShapeDtypeStruct((m // tm, d_ff, n_next), BF16)],
        (x, w, w, xs, w_next), dims=("arbitrary", "arbitrary"))
    return out, out_s[0], w_next16


def _mm_kernel(a_ref, w_ref, as_ref, o_ref, os_ref):
    w = w_ref[...].astype(BF16)
    o_ref[...] = jnp.dot(a_ref[...], w, preferred_element_type=F32)
    _on_first_row_tile(os_ref, lambda: jnp.dot(as_ref[...], w, preferred_element_type=F32))


def _mm_res_kernel(a_ref, w_ref, r_ref, as_ref, rs_ref, o_ref, os_ref, *, scale):
    w = w_ref[...].astype(BF16)
    o_ref[...] = r_ref[...] + scale * jnp.dot(a_ref[...], w, preferred_element_type=F32)
    _on_first_row_tile(
        os_ref, lambda: rs_ref[...] + scale * jnp.dot(as_ref[...], w, preferred_element_type=F32))


def matmul(l, a, a_s, w, *, tn, res=None, res_s=None, scale=1.0, n=None, tm=1024, w_is_layer=False):
    m, k = a.shape
    ms = a_s.shape[0]
    n = w.shape[2] if n is None else n
    tm = min(tm, m)
    a_spec = pl.BlockSpec((tm, k), lambda i, j, l: (i, 0))
    if w_is_layer:
        w_spec = pl.BlockSpec((None, k, tn), lambda i, j, l: (0, 0, j))
    else:
        w_spec = pl.BlockSpec((None, k, tn), lambda i, j, l: (l[0], 0, j))
    o_spec = pl.BlockSpec((tm, tn), lambda i, j, l: (i, j))
    out_specs = [o_spec, _side_out(ms, tn)]
    out_shape = [jax.ShapeDtypeStruct((m, n), F32), jax.ShapeDtypeStruct((m // tm, ms, n), F32)]
    dims = ("arbitrary", "arbitrary")
    if res is None:
        out, out_s = _call(_mm_kernel, l, (m // tm, n // tn), [a_spec, w_spec, _side_spec(ms, k, False)],
                           out_specs, out_shape, (a, w, a_s), dims=dims)
    else:
        out, out_s = _call(
            functools.partial(_mm_res_kernel, scale=scale), l, (m // tm, n // tn),
            [a_spec, w_spec, o_spec, _side_spec(ms, k, False), _side_spec(ms, tn, True)],
            out_specs, out_shape, (a, w, res, a_s, res_s), dims=dims)
    return out, out_s[0]


def _merge_kernel(h_ref, ya_ref, yb_ref, yc_ref, yd_ref, g0_ref, g1_ref, g2_ref, g3_ref, wb_ref,
                  hs_ref, sa_ref, sb_ref, sc_ref, sd_ref, o_ref, os_ref):
    wg = [g_ref[...].astype(BF16) for g_ref in (g0_ref, g1_ref, g2_ref, g3_ref)]
    wb = [wb_ref[n].astype(BF16) for n in range(N_BRANCH)]

    def gated_sum(h, ys):
        acc = None
        for n in range(N_BRANCH):
            gate = jnp.dot(h, wg[n], preferred_element_type=F32)
            t = _sigmoid(gate) * jnp.dot(ys[n], wb[n], preferred_element_type=F32)
            acc = t if acc is None else acc + t
        return acc

    o_ref[...] = gated_sum(h_ref[...], [r[...] for r in (ya_ref, yb_ref, yc_ref, yd_ref)]).astype(o_ref.dtype)
    _on_first_row_tile(os_ref, lambda: gated_sum(hs_ref[...], [r[...] for r in (sa_ref, sb_ref, sc_ref, sd_ref)]))


def merge(l, h, ys, h_s, ys_s, w_in, w_branch):
    m = h.shape[0]
    ms = h_s.shape[0]
    tn = 256
    tm = min(1024, m)
    nj = D_MODEL // tn
    y_spec = pl.BlockSpec((tm, W_MIX), lambda i, j, l: (i, 0))
    g_specs = [pl.BlockSpec((None, D_MODEL, tn), functools.partial(
        lambda i, j, l, n: (l[0], 0, OFF_G // tn + n * nj + j), n=n)) for n in range(N_BRANCH)]
    out, out_s = _call(
        _merge_kernel, l, (m // tm, nj),
        [pl.BlockSpec((tm, D_MODEL), lambda i, j, l: (i, 0))] + [y_spec] * N_BRANCH + g_specs
        + [pl.BlockSpec((None, N_BRANCH, W_MIX, tn), lambda i, j, l: (l[0], 0, 0, j))]
        + [_side_spec(ms, D_MODEL, False)] + [_side_spec(ms, W_MIX, False)] * N_BRANCH,
        [pl.BlockSpec((tm, tn), lambda i, j, l: (i, j)), _side_out(ms, tn)],
        [jax.ShapeDtypeStruct((m, D_MODEL), BF16),
         jax.ShapeDtypeStruct((m // tm, ms, D_MODEL), BF16)],
        (h, *ys, w_in, w_in, w_in, w_in, w_branch, h_s, *ys_s), dims=("arbitrary", "arbitrary"))
    return out, out_s[0]


def _pool_kernel(u_ref, t0_ref, wp_ref, ps_ref, y_ref, ext_ref, *, tt, pos0):
    i = pl.program_id(1)

    @pl.when(i == 0)
    def _():
        ext_ref[0:POOL_HALO, :] = t0_ref[...]

    @pl.when(i > 0)
    def _():
        ext_ref[0:POOL_HALO, :] = ext_ref[tt:tt + POOL_HALO, :]

    u = u_ref[...]
    ext_ref[POOL_HALO:POOL_HALO + tt, :] = u
    pos = pos0 + i * tt + lax.broadcasted_iota(jnp.int32, (tt, 1), 0)
    ys = []
    for gi, win in enumerate(POOL_WINDOWS):
        cs = slice(gi * POOL_GROUP, (gi + 1) * POOL_GROUP)
        wsum = u[:, cs]
        for k in range(1, win):
            wsum = wsum + ext_ref[POOL_HALO - k:POOL_HALO - k + tt, cs]
        count = jnp.minimum(win, pos + 1).astype(F32)
        d = wsum / count - u[:, cs]
        ys.append(_dot(d, wp_ref[gi]))
    y_ref[...] = (jnp.concatenate(ys, axis=1) * ps_ref[...]).astype(y_ref.dtype)


def pool_mixer(l, proj3, tail0, w_pool, pool_scale, pos0):
    b, t, _ = proj3.shape
    tt = min(512, t)
    return _call(
        functools.partial(_pool_kernel, tt=tt, pos0=pos0), l, (b, t // tt),
        [pl.BlockSpec((None, tt, W_MIX), lambda b, i, l: (b, i, OFF_A // W_MIX)),
         pl.BlockSpec((None, POOL_HALO, W_MIX), lambda b, i, l: (b, 0, 0)),
         pl.BlockSpec((None, len(POOL_WINDOWS), POOL_GROUP, POOL_GROUP), lambda b, i, l: (l[0], 0, 0, 0)),
         pl.BlockSpec((None, 1, W_MIX), lambda b, i, l: (l[0], 0, 0))],
        pl.BlockSpec((None, tt, W_MIX), lambda b, i, l: (b, i, 0)),
        jax.ShapeDtypeStruct((b, t, W_MIX), BF16), (proj3, tail0, w_pool, pool_scale),
        scratch=[pltpu.VMEM((POOL_HALO + tt, W_MIX), F32)], dims=("parallel", "arbitrary"))


def _conv_kernel(val_ref, gate_ref, t0_ref, w_ref, cb_ref, lg_ref, lb_ref, y_ref, u_ref, ext_ref, sh_ref,
                 *, tt):
    i = pl.program_id(1)

    @pl.when(i == 0)
    def _():
        ext_ref[0:CONV_HALO, :] = t0_ref[...]

    @pl.when(i > 0)
    def _():
        ext_ref[0:CONV_HALO, :] = ext_ref[tt:tt + CONV_HALO, :]

    u = val_ref[...] * _sigmoid(gate_ref[...])
    u_ref[...] = u
    ext_ref[CONV_HALO:CONV_HALO + tt, :] = u
    first = CONV_HALO - CONV_BUF
    last = first + CONV_WIDTH - 1
    for b in range(SUBLANES):
        top = max(s for s in range(first, last + 1) if s % SUBLANES == b)
        sh_ref[b, 0:top - b + tt, :] = ext_ref[b:top + tt, :]
    acc = cb_ref[...]
    for j in range(CONV_WIDTH):
        s = first + j
        base = s - s % SUBLANES
        acc = acc + sh_ref[s % SUBLANES, base:base + tt, :] * w_ref[j:j + 1, :]
    y = acc
    mu = jnp.mean(y, axis=-1, keepdims=True)
    yc = y - mu
    var = jnp.mean(yc * yc, axis=-1, keepdims=True)
    yn = yc * lax.rsqrt(var + LN_EPS) * lg_ref[...] + lb_ref[...]
    y_ref[...] = (yn * _sigmoid(yn)).astype(y_ref.dtype)


def conv_mixer(l, proj3, tail0, conv_w, conv_b, ln_g, ln_b):
    b, t, _ = proj3.shape
    tt = min(256, t)
    vec = pl.BlockSpec((None, 1, W_MIX), lambda b, i, l: (l[0], 0, 0))
    blk = pl.BlockSpec((None, tt, W_MIX), lambda b, i, l: (b, i, 0))
    return _call(
        functools.partial(_conv_kernel, tt=tt), l, (b, t // tt),
        [pl.BlockSpec((None, tt, W_MIX), lambda b, i, l: (b, i, OFF_B // W_MIX)),
         pl.BlockSpec((None, tt, W_MIX), lambda b, i, l: (b, i, OFF_B // W_MIX + 1)),
         pl.BlockSpec((None, CONV_HALO, W_MIX), lambda b, i, l: (b, 0, 0)),
         pl.BlockSpec((None, CONV_WIDTH, W_MIX), lambda b, i, l: (l[0], 0, 0)),
         vec, vec, vec],
        [blk, blk],
        [jax.ShapeDtypeStruct((b, t, W_MIX), BF16), jax.ShapeDtypeStruct((b, t, W_MIX), F32)],
        (proj3, proj3, tail0, conv_w, conv_b, ln_g, ln_b),
        scratch=[pltpu.VMEM((CONV_HALO + tt, W_MIX), F32),
                 pltpu.VMEM((SUBLANES, CONV_HALO + tt, W_MIX), F32)], dims=("parallel", "arbitrary"))


def _sb_heads(x):
    n = x.shape[1] // SB_HEAD_DIM
    return jnp.stack([x[:, h * SB_HEAD_DIM:(h + 1) * SB_HEAD_DIM] for h in range(n)], axis=0)


def _sb_block(q3, k3, v3, bias, carry, acc, neg_upper, mask):
    nh, tq, _ = q3.shape
    tk = k3.shape[1]
    z = _bdot_nt(q3, k3)
    z = jnp.stack([z[h] + bias[h] for h in range(nh)], axis=0)
    fail = jnp.maximum(z, 0.0) + jnp.log(1.0 + jnp.exp2(jnp.abs(z) * (-LOG2E)))
    if mask is not None:
        fail = jnp.where(mask, fail, 0.0)
    log_hit = z - fail
    fail_first = fail[:, :, 0:1]
    later = jnp.dot(fail.astype(BF16).reshape(nh * tq, tk), neg_upper,
                    preferred_element_type=F32).reshape(nh, tq, tk)
    wts = jnp.exp(log_hit + later + carry)
    if mask is not None:
        wts = jnp.where(mask, wts, 0.0)
    acc = acc + _bdot(wts, v3)
    carry = carry + later[:, :, 0:1] - fail_first
    return carry, acc


def _sb_prompt_kernel(bias_ref, q_ref, k_ref, v_ref, o_ref, kb_ref, vb_ref, *, blk, n_blk):
    qi = pl.program_id(2)
    grp = pl.program_id(1)

    @pl.when(qi == 0)
    def _():
        def fill(j, _):
            rows = pl.ds(pl.multiple_of(j * blk, blk), blk)
            kb_ref[:, rows, :] = _sb_heads(k_ref[rows, :].astype(BF16))
            vb_ref[:, rows, :] = _sb_heads(v_ref[rows, :].astype(BF16))
            return 0
        lax.fori_loop(0, n_blk, fill, 0)

    row = lax.broadcasted_iota(jnp.int32, (blk, blk), 0)
    col = lax.broadcasted_iota(jnp.int32, (blk, blk), 1)
    causal = row > col
    neg_upper = jnp.where(causal, -1.0, 0.0).astype(BF16)
    q3 = _sb_heads((q_ref[...] * SB_SCALE).astype(BF16))
    nh = q3.shape[0]
    bias = [bias_ref[0, grp * nh + h] for h in range(nh)]

    def kv(j):
        rows = pl.ds(pl.multiple_of(j * blk, blk), blk)
        return kb_ref[:, rows, :], vb_ref[:, rows, :]

    carry = jnp.zeros((nh, blk, 1), F32)
    acc = jnp.zeros((nh, blk, SB_HEAD_DIM), F32)
    carry, acc = _sb_block(q3, *kv(qi), bias, carry, acc, neg_upper, causal)

    def step(j, state):
        return _sb_block(q3, *kv(j), bias, state[0], state[1], neg_upper, None)

    rem = qi % SB_UNROLL
    carry, acc = lax.fori_loop(0, rem, lambda it, state: step(qi - 1 - it, state), (carry, acc))

    def body(it, state):
        j = qi - 1 - rem - SB_UNROLL * it
        for u in range(SB_UNROLL):
            state = step(j - u, state)
        return state

    carry, acc = lax.fori_loop(0, qi // SB_UNROLL, body, (carry, acc))
    o_ref[...] = jnp.concatenate([acc[h] for h in range(nh)], axis=1).astype(o_ref.dtype)


def sb_prompt(l, proj3, sb_bias):
    b, t, _ = proj3.shape
    blk = min(ATTN_BLOCK, t)
    wid = SB_GROUP * SB_HEAD_DIM
    ngrp = W_MIX // wid
    c0 = OFF_C // wid
    return _call(
        functools.partial(_sb_prompt_kernel, blk=blk, n_blk=t // blk), l, (b, ngrp, t // blk),
        [pl.BlockSpec((None, 1, SB_HEADS), lambda b, h, i, l: (l[0], 0, 0), memory_space=pltpu.SMEM),
         pl.BlockSpec((None, blk, wid), lambda b, h, i, l: (b, i, c0 + h)),
         pl.BlockSpec((None, t, wid), lambda b, h, i, l: (b, 0, c0 + ngrp + h)),
         pl.BlockSpec((None, t, wid), lambda b, h, i, l: (b, 0, c0 + 2 * ngrp + h))],
        pl.BlockSpec((None, blk, wid), lambda b, h, i, l: (b, i, h)),
        jax.ShapeDtypeStruct((b, t, W_MIX), BF16), (sb_bias, proj3, proj3, proj3),
        scratch=[pltpu.VMEM((SB_GROUP, t, SB_HEAD_DIM), BF16)] * 2,
        dims=("parallel", "parallel", "arbitrary"))


def _sb_decode_kernel(q_ref, bias_ref, *refs, n_steps, pages):
    k_refs, v_refs = refs[:pages], refs[pages:2 * pages]
    o_ref, carry_ref, acc_ref = refs[2 * pages:]
    p = pl.program_id(1)

    @pl.when(p == 0)
    def _():
        carry_ref[...] = jnp.zeros_like(carry_ref)
        acc_ref[...] = jnp.zeros_like(acc_ref)

    q = (q_ref[...] * SB_SCALE).astype(BF16)
    bias = bias_ref[...]
    z = jnp.concatenate([_dot(q, k_ref[...].reshape(W_MIX, PAGE_SIZE)) + bias for k_ref in k_refs], axis=0)
    log_fail = -_softplus(z)
    row = lax.broadcasted_iota(jnp.int32, (PAGE_SIZE, PAGE_SIZE), 0)
    col = lax.broadcasted_iota(jnp.int32, (PAGE_SIZE, PAGE_SIZE), 1)
    upper = (row > col).astype(BF16)
    later = _dot(log_fail, upper)
    total = later[:, 0:1] + log_fail[:, 0:1]
    arg = z + log_fail + later
    carry = carry_ref[...]
    acc = acc_ref[...]
    for i in reversed(range(pages)):
        rows = slice(i * SB_HEADS, (i + 1) * SB_HEADS)
        wts = jnp.exp(arg[rows] + carry)
        acc = acc + _dot_nt(wts, v_refs[i][...].reshape(W_MIX, PAGE_SIZE))
        carry = carry + total[rows]
    carry_ref[...] = carry
    acc_ref[...] = acc

    @pl.when(p == n_steps - 1)
    def _():
        head = lax.broadcasted_iota(jnp.int32, acc.shape, 0)
        col_head = lax.broadcasted_iota(jnp.int32, acc.shape, 1) // SB_HEAD_DIM
        o_ref[...] = jnp.sum(jnp.where(head == col_head, acc, 0.0), axis=0, keepdims=True)


def sb_decode(l, q_blocks, sb_bias_col, cache_kt, cache_vt, page_table):
    b = q_blocks.shape[0]
    n_pages = page_table.shape[1]
    pages = DECODE_PAGES if n_pages % DECODE_PAGES == 0 else 1
    n_steps = n_pages // pages

    def page(i):
        return pl.BlockSpec((None, None, SB_HEADS, SB_HEAD_DIM, PAGE_SIZE),
                            lambda b, p, l, pt: (l[0], pt[b, n_pages - pages * (p + 1) + i], 0, 0, 0))

    return _call(
        functools.partial(_sb_decode_kernel, n_steps=n_steps, pages=pages), l, (b, n_steps),
        [pl.BlockSpec((None, SB_HEADS, W_MIX), lambda b, p, l, pt: (b, 0, 0)),
         pl.BlockSpec((None, SB_HEADS, 1), lambda b, p, l, pt: (l[0], 0, 0))]
        + [page(i) for i in range(pages)] * 2,
        pl.BlockSpec((None, 1, W_MIX), lambda b, p, l, pt: (b, 0, 0)),
        jax.ShapeDtypeStruct((b, 1, W_MIX), F32),
        (q_blocks, sb_bias_col) + (cache_kt,) * pages + (cache_vt,) * pages,
        scratch=[pltpu.VMEM((SB_HEADS, 1), F32), pltpu.VMEM((SB_HEADS, W_MIX), F32)],
        dims=("parallel", "arbitrary"), prefetch=(page_table,))


def _wkv_inputs(pr_ref, pk_ref, pv_ref, pl_ref, sh0_ref, mu_ref, w0_ref, dup_ref, a0_ref, aup_ref,
                gup_ref, ext_ref, *, tt, t_valid):
    i = pl.program_id(1)

    @pl.when(i == 0)
    def _():
        ext_ref[0:SHIFT_HALO, :] = sh0_ref[...]

    @pl.when(i > 0)
    def _():
        ext_ref[0:SHIFT_HALO, :] = ext_ref[tt:tt + SHIFT_HALO, :]

    rows = slice(SHIFT_HALO, SHIFT_HALO + tt)
    ext_ref[rows, 0:W_MIX] = pr_ref[...]
    ext_ref[rows, W_MIX:2 * W_MIX] = pk_ref[...]
    ext_ref[rows, 2 * W_MIX:3 * W_MIX] = pv_ref[...]
    ext_ref[rows, 3 * W_MIX:N_D] = pl_ref[...]
    p = ext_ref[rows, :]
    prev = ext_ref[SHIFT_HALO - 1:SHIFT_HALO - 1 + tt, :]
    xs = p + (prev - p) * mu_ref[...]
    o = 3 * W_MIX
    w_dn = xs[:, o:o + DECAY_LORA]
    a_dn = xs[:, o + DECAY_LORA:o + DECAY_LORA + A_LORA]
    g_dn = xs[:, o + DECAY_LORA + A_LORA:N_D]
    w_log = -_softplus(-(w0_ref[...] + _dot(jnp.tanh(w_dn), dup_ref[...]))) - 0.5
    log_decay = -jnp.exp(w_log)
    alpha = _sigmoid(a0_ref[...] + _dot(a_dn, aup_ref[...]))
    k = xs[:, W_MIX:2 * W_MIX]
    v = xs[:, 2 * W_MIX:3 * W_MIX]
    if t_valid is not None:
        valid = (i * tt + lax.broadcasted_iota(jnp.int32, (tt, 1), 0)) < t_valid
        k = jnp.where(valid, k, 0.0)
        v = jnp.where(valid, v, 0.0)
        log_decay = jnp.where(valid, log_decay, 0.0)
    return xs[:, 0:W_MIX], k, v, log_decay, alpha, _dot(_sigmoid(g_dn), gup_ref[...])


def _bdot(a, b):
    return lax.dot_general(a.astype(BF16), b.astype(BF16), (((2,), (1,)), ((0,), (0,))),
                           preferred_element_type=F32)


def _bdot_nt(a, b):
    return lax.dot_general(a.astype(BF16), b.astype(BF16), (((2,), (2,)), ((0,), (0,))),
                           preferred_element_type=F32)


def _bdot_tn(a, b):
    return lax.dot_general(a.astype(BF16), b.astype(BF16), (((1,), (1,)), ((0,), (0,))),
                           preferred_element_type=F32)


def _heads(x):
    return jnp.stack([x[:, h * RWKV_HEAD_DIM:(h + 1) * RWKV_HEAD_DIM] for h in range(RWKV_HEADS)], axis=0)


def _wkv_chunk_prepare(r, k, v, ld, al, kk_w, ka_w, rk_w):
    ch = r.shape[0]
    row = lax.broadcasted_iota(jnp.int32, (ch, ch), 0)
    col = lax.broadcasted_iota(jnp.int32, (ch, ch), 1)
    incl = row >= col
    eye = (row == col).astype(F32)
    row2 = lax.broadcasted_iota(jnp.int32, (2 * ch, 2 * ch), 0)
    col2 = lax.broadcasted_iota(jnp.int32, (2 * ch, 2 * ch), 1) & (ch - 1)
    tri = (row2 & (ch - 1)) + (row2 // ch) > col2
    hi, lo = _split_bf16(ld)
    lower = incl.astype(BF16)
    cum = (jnp.dot(lower, hi, preferred_element_type=F32)
           + jnp.dot(lower, lo, preferred_element_type=F32))
    cum_end = cum[ch - 1:ch, :]
    g_in = jnp.exp(cum)
    g_ex = jnp.exp(cum - ld)
    g_inv = jnp.exp(-cum)
    g_end = jnp.exp(cum_end - cum)
    g_tot = jnp.exp(cum_end)
    kk = k * kk_w
    k2 = k * (1.0 + (al - 1.0) * ka_w)
    sums = _heads(jnp.concatenate([kk, r * k2 * rk_w], axis=0))
    kk3 = sums[:, 0:ch]
    unit = 1.0 / jnp.maximum(jnp.sqrt(jnp.sum(kk3 * kk3, axis=-1, keepdims=True)), 1e-12)
    rk_sum = jnp.sum(sums[:, ch:2 * ch], axis=-1, keepdims=True)

    def scaled(top, bottom):
        x = _heads(jnp.concatenate([top, bottom], axis=0))
        return jnp.concatenate([x[:, 0:ch] * unit, x[:, ch:2 * ch]], axis=1)

    b = kk * al
    a2 = scaled(-kk * g_ex, r * g_in)
    bm = scaled(b * g_inv, k2 * g_inv)
    bk_end = scaled(b * g_end, k2 * g_end)
    v3 = _heads(v)
    prod = jnp.where(tri, _bdot_nt(a2, bm), 0.0)
    x = prod[:, 0:ch, 0:ch]
    inv = eye + x
    n_sq = ch.bit_length() - 2
    for _ in range(n_sq):
        x = _bdot(x, x)
        inv = inv + _bdot(inv, x)
    ak_v = _bdot(prod[:, 0:ch], jnp.concatenate([jnp.zeros_like(v3), v3], axis=1))
    return dict(a2=a2, prod_r=prod[:, ch:2 * ch], inv=inv, ak_v=ak_v, v3=v3, bk_end=bk_end,
                g_tot=_heads(g_tot), bonus=rk_sum * v3)


def _wkv_chunk_advance(p, s, g, ln_g, ln_b):
    ch = p['inv'].shape[1]
    u = _bdot(p['inv'], _bdot_nt(p['a2'][:, 0:ch], s) + p['ak_v'])
    uv = jnp.concatenate([u, p['v3']], axis=1)
    y3 = _bdot_nt(p['a2'][:, ch:2 * ch], s) + _bdot(p['prod_r'], uv)
    s_new = s * p['g_tot'] + _bdot_tn(uv, p['bk_end'])
    mu = jnp.mean(y3, axis=-1, keepdims=True)
    yc = y3 - mu
    var = jnp.mean(yc * yc, axis=-1, keepdims=True)
    out3 = yc * lax.rsqrt(var + GN_EPS) * _heads(ln_g) + _heads(ln_b) + p['bonus']
    return jnp.concatenate([out3[h] for h in range(RWKV_HEADS)], axis=1) * g, s_new


def _wkv_kernel(pr_ref, pk_ref, pv_ref, pl_ref, sh0_ref, mu_ref, w0_ref, dup_ref, a0_ref, aup_ref, gup_ref,
                s0_ref, kk_ref, ka_ref, rk_ref, lg_ref, lb_ref, y_ref, so_ref, s_ref, ext_ref,
                *, ch, n_steps, t_valid):
    c = pl.program_id(1)

    @pl.when(c == 0)
    def _():
        s_ref[...] = s0_ref[...]

    tt = pr_ref.shape[0]
    r, k, v, ld, al, g = _wkv_inputs(pr_ref, pk_ref, pv_ref, pl_ref, sh0_ref, mu_ref, w0_ref, dup_ref,
                                     a0_ref, aup_ref, gup_ref, ext_ref, tt=tt, t_valid=t_valid)
    chunks = [slice(j * ch, (j + 1) * ch) for j in range(tt // ch)]
    prepared = [_wkv_chunk_prepare(r[rows], k[rows], v[rows], ld[rows], al[rows],
                                   kk_ref[...], ka_ref[...], rk_ref[...]) for rows in chunks]
    s = s_ref[...]
    for rows, p in zip(chunks, prepared):
        y, s = _wkv_chunk_advance(p, s, g[rows], lg_ref[...], lb_ref[...])
        y_ref[rows, :] = y.astype(y_ref.dtype)
    s_ref[...] = s

    @pl.when(c == n_steps - 1)
    def _():
        so_ref[...] = s


def wkv_mixer(l, proj3, shift0, s0, lw, t_valid):
    b, t, _ = proj3.shape
    ch = min(WKV_CHUNK, t)
    rows = ch * WKV_CHUNKS_PER_STEP if t % (ch * WKV_CHUNKS_PER_STEP) == 0 else ch
    cd = OFF_D // W_MIX
    col = lambda j: pl.BlockSpec((None, rows, W_MIX), lambda b, c, l: (b, c, cd + j))
    st = pl.BlockSpec((None, RWKV_HEADS, RWKV_HEAD_DIM, RWKV_HEAD_DIM), lambda b, c, l: (b, 0, 0, 0))
    vec = lambda n: pl.BlockSpec((None, 1, n), lambda b, c, l: (l[0], 0, 0))
    up = lambda n: pl.BlockSpec((None, n, W_MIX), lambda b, c, l: (l[0], 0, 0))
    return _call(
        functools.partial(_wkv_kernel, ch=ch, n_steps=t // rows, t_valid=t_valid), l, (b, t // rows),
        [col(0), col(1), col(2),
         pl.BlockSpec((None, rows, N_LORA), lambda b, c, l: (b, c, (OFF_D + 3 * W_MIX) // N_LORA)),
         pl.BlockSpec((None, SHIFT_HALO, N_D), lambda b, c, l: (b, 0, 0)),
         vec(N_D), vec(W_MIX), up(DECAY_LORA), vec(W_MIX), up(A_LORA), up(GATE_LORA), st]
        + [vec(W_MIX)] * 5,
        [pl.BlockSpec((None, rows, W_MIX), lambda b, c, l: (b, c, 0)), st],
        [jax.ShapeDtypeStruct((b, t, W_MIX), BF16),
         jax.ShapeDtypeStruct((b, RWKV_HEADS, RWKV_HEAD_DIM, RWKV_HEAD_DIM), F32)],
        (proj3, proj3, proj3, proj3, shift0, lw['shift_mu'], lw['decay_w0'], lw['decay_up'], lw['a0'],
         lw['a_up'], lw['g_up'], s0, lw['k_k'], lw['k_a'], lw['r_k'], lw['lnx_g'], lw['lnx_b']),
        scratch=[pltpu.VMEM((RWKV_HEADS, RWKV_HEAD_DIM, RWKV_HEAD_DIM), F32),
                 pltpu.VMEM((SHIFT_HALO + rows, N_D), F32)],
        dims=("parallel", "arbitrary"))


def _split_heads_kernel(k_ref, v_ref, ko_ref, vo_ref):
    ko_ref[...] = k_ref[...].T.reshape(ko_ref.shape)
    vo_ref[...] = v_ref[...].T.reshape(vo_ref.shape)


def split_heads(l, proj3):
    b, t, _ = proj3.shape
    tt = min(512, t)
    col = lambda j: pl.BlockSpec((None, tt, W_MIX), lambda b, i, l: (b, i, OFF_C // W_MIX + j))
    out = pl.BlockSpec((None, SB_HEADS, SB_HEAD_DIM, tt), lambda b, i, l: (b, 0, 0, i))
    shape = jax.ShapeDtypeStruct((b, SB_HEADS, SB_HEAD_DIM, t), F32)
    return _call(_split_heads_kernel, l, (b, t // tt), [col(1), col(2)], [out, out], [shape, shape],
                 (proj3, proj3), dims=("parallel", "parallel"))


def _mixers(l, proj3, pool0, conv0, shift0, wkv0, pos0, t_valid, attend, lw):
    b, t, _ = proj3.shape
    y_a = pool_mixer(l, proj3, pool0, lw['w_pool'], lw['pool_scale'], pos0)
    y_b, glu = conv_mixer(l, proj3, conv0, lw['conv_w'], lw['conv_b'], lw['ln_g'], lw['ln_b'])
    y_c = attend(l, proj3)
    y_d, wkv_new = wkv_mixer(l, proj3, shift0, wkv0, lw, t_valid)
    return [y.reshape(b * t, W_MIX) for y in (y_a, y_b, y_c, y_d)], glu, wkv_new


def _trunk_layer(l, xp, xs, mix_p, mix_s, lw):
    dp, ds = xp.shape, xs.shape
    xp, xs = xp.reshape(-1, dp[2]), xs.reshape(-1, ds[2])

    def ffn(xp, xs, g, w_a, w_b):
        act_p, act_s, w_b16 = ffn_in(l, rmsnorm(l, xp, g, BF16), rmsnorm(l, xs, g, BF16), w_a, w_b)
        return matmul(l, act_p, act_s, w_b16, tn=512, res=xp, res_s=xs, scale=FFN_RESIDUAL, w_is_layer=True)

    xp, xs = ffn(xp, xs, lw['g_ffn1'], lw['w_ffn1_in'], lw['w_ffn1_out'])
    hp, hs = rmsnorm(l, xp, lw['g_mix'], BF16), rmsnorm(l, xs, lw['g_mix'], BF16)
    proj_p, proj_s = matmul(l, hp, hs, lw['w_in'], tn=256, n=OFF_G, tm=2048)
    proj_p, proj_s = proj_p.reshape(dp[0], dp[1], OFF_G), proj_s.reshape(ds[0], ds[1], OFF_G)
    ys_p, glu_p, wkv_p = mix_p(l, proj_p)
    ys_s, glu_s, wkv_s = mix_s(l, proj_s)
    merged_p, merged_s = merge(l, hp, ys_p, hs, ys_s, lw['w_in'], lw['w_branch'])
    xp, xs = matmul(l, merged_p, merged_s, lw['w_out'], tn=512, res=xp, res_s=xs, scale=1.0, tm=2048)
    xp, xs = ffn(xp, xs, lw['g_ffn2'], lw['w_ffn2_in'], lw['w_ffn2_out'])
    return xp.reshape(dp), xs.reshape(ds), (proj_p, glu_p, wkv_p), (proj_s, glu_s, wkv_s)


def _front_pad(rows, height):
    return jnp.pad(rows, ((0, 0), (height - rows.shape[1], 0), (0, 0)))


def kernel(x_prompt, x_sample, cache_k, cache_v, state_pool, state_conv, state_shift, state_wkv,
           page_table, g_ffn1, w_ffn1_in, w_ffn1_out, g_mix, w_in, w_pool, pool_scale, conv_w,
           conv_b, ln_g, ln_b, sb_bias, shift_mu, decay_w0, decay_up, a0, a_up, g_up, k_k, k_a, r_k,
           lnx_g, lnx_b, w_branch, w_out, g_ffn2, w_ffn2_in, w_ffn2_out, g_final):
    depth = w_in.shape[0]
    bp, tp, d = x_prompt.shape
    bs, ts, _ = x_sample.shape
    assert d == D_MODEL and ts == 1
    past_len = page_table.shape[1] * PAGE_SIZE
    row = lambda a: a.reshape(depth, 1, -1)
    lw = {
        'g_ffn1': row(g_ffn1), 'w_ffn1_in': w_ffn1_in, 'w_ffn1_out': w_ffn1_out, 'g_mix': row(g_mix),
        'w_in': w_in, 'w_pool': w_pool, 'pool_scale': row(pool_scale), 'conv_w': conv_w,
        'conv_b': row(conv_b), 'ln_g': row(ln_g), 'ln_b': row(ln_b), 'sb_bias': row(sb_bias),
        'shift_mu': row(shift_mu), 'decay_w0': row(decay_w0), 'decay_up': decay_up, 'a0': row(a0),
        'a_up': a_up, 'g_up': g_up, 'k_k': row(k_k), 'k_a': row(k_a), 'r_k': row(r_k),
        'lnx_g': row(lnx_g), 'lnx_b': row(lnx_b), 'w_branch': w_branch, 'w_out': w_out,
        'g_ffn2': row(g_ffn2), 'w_ffn2_in': w_ffn2_in, 'w_ffn2_out': w_ffn2_out,
    }
    sb_bias_col = sb_bias.reshape(depth, SB_HEADS, 1)
    cache_kt = jnp.transpose(cache_k, (0, 1, 3, 4, 2))
    cache_vt = jnp.transpose(cache_v, (0, 1, 3, 4, 2))
    own_head = (jnp.arange(W_MIX)[None, :] // SB_HEAD_DIM) == jnp.arange(SB_HEADS)[:, None]
    xs0 = jnp.pad(x_sample, ((0, 0), (0, SAMPLE_ROWS - ts), (0, 0)))

    def layer(carry, li):
        xp, xs = carry
        l = li.reshape(1)
        pool_s = lax.dynamic_index_in_dim(state_pool, li, 0, keepdims=False)
        conv_s = lax.dynamic_index_in_dim(state_conv, li, 0, keepdims=False)
        shift_s = lax.dynamic_index_in_dim(state_shift, li, 0, keepdims=False)
        wkv_s = lax.dynamic_index_in_dim(state_wkv, li, 0, keepdims=False)

        def attend_sample(l, p3):
            q_blocks = jnp.where(own_head, p3[:, 0:1, OFF_C:OFF_C + W_MIX], 0.0)
            y = sb_decode(l, q_blocks, sb_bias_col, cache_kt, cache_vt, page_table)
            return jnp.pad(y, ((0, 0), (0, SAMPLE_ROWS - 1), (0, 0))).astype(BF16)

        mix_p = lambda l, p3: _mixers(
            l, p3, jnp.zeros((bp, POOL_HALO, W_MIX), F32), jnp.zeros((bp, CONV_HALO, W_MIX), F32),
            jnp.zeros((bp, SHIFT_HALO, N_D), F32),
            jnp.zeros((bp, RWKV_HEADS, RWKV_HEAD_DIM, RWKV_HEAD_DIM), F32), 0, None,
            lambda l, p3: sb_prompt(l, p3, lw['sb_bias']), lw)
        mix_s = lambda l, p3: _mixers(
            l, p3, _front_pad(pool_s, POOL_HALO), _front_pad(conv_s, CONV_HALO),
            _front_pad(shift_s, SHIFT_HALO), wkv_s, past_len, ts, attend_sample, lw)
        xp, xs, (proj_p, glu_p, wkv_p), (proj_s, glu_s, wkv_s_new) = _trunk_layer(l, xp, xs, mix_p, mix_s, lw)

        k_p, v_p = split_heads(l, proj_p)
        k_s, v_s = (proj_s[:, :ts, OFF_C + j * W_MIX:OFF_C + (j + 1) * W_MIX].reshape(
            bs, ts, SB_HEADS, SB_HEAD_DIM) for j in (1, 2))
        outs = (
            k_p, v_p, proj_p[:, tp - POOL_BUF:, OFF_A:OFF_A + W_MIX], glu_p[:, tp - CONV_BUF:],
            proj_p[:, tp - 1:, OFF_D:OFF_D + N_D], wkv_p,
            k_s, v_s,
            jnp.concatenate([pool_s, proj_s[:, :ts, OFF_A:OFF_A + W_MIX]], axis=1)[:, -POOL_BUF:],
            jnp.concatenate([conv_s, glu_s[:, :ts]], axis=1)[:, -CONV_BUF:],
            proj_s[:, ts - 1:ts, OFF_D:OFF_D + N_D], wkv_s_new)
        return (xp, xs), outs

    (xp, xs), st = lax.scan(layer, (x_prompt, xs0), jnp.arange(depth, dtype=jnp.int32))
    zero = jnp.zeros((1,), jnp.int32)
    g_fin = g_final.reshape(1, 1, d)
    y_prompt = rmsnorm(zero, xp.reshape(bp * tp, d), g_fin, F32).reshape(bp, tp, d)
    y_sample = rmsnorm(zero, xs.reshape(bs * SAMPLE_ROWS, d), g_fin, F32).reshape(bs, SAMPLE_ROWS, d)[:, :ts]
    to_cache_layout = lambda a: jnp.transpose(a, (0, 1, 4, 2, 3))
    return (y_prompt, y_sample, to_cache_layout(st[0]), to_cache_layout(st[1])) + tuple(st[2:])
```

```python
import functools

import jax
import jax.numpy as jnp
from jax import lax
from jax.experimental import pallas as pl
from jax.experimental.pallas import tpu as pltpu

F32 = jnp.float32
BF16 = jnp.bfloat16
LOG2E = 1.4426950408889634

LANES = 128
SUBLANES = 8
VMEM_LIMIT_BYTES = 56 * 1024 * 1024

D_MODEL = 2048
W_MIX = D_MODEL // 4
POOL_WINDOWS = (2, 4, 8, 16)
POOL_GROUP = W_MIX // len(POOL_WINDOWS)
POOL_BUF = max(POOL_WINDOWS) - 1
CONV_WIDTH = 31
CONV_BUF = CONV_WIDTH - 1
SB_HEADS = 8
SB_HEAD_DIM = W_MIX // SB_HEADS
SB_SCALE = SB_HEAD_DIM ** -0.5
RWKV_HEAD_DIM = 64
RWKV_HEADS = W_MIX // RWKV_HEAD_DIM
DECAY_LORA = 64
A_LORA = 64
GATE_LORA = 128
N_LORA = DECAY_LORA + A_LORA + GATE_LORA
N_D = 3 * W_MIX + N_LORA
N_BRANCH = 4
OFF_A = 0
OFF_B = OFF_A + W_MIX
OFF_C = OFF_B + 2 * W_MIX
OFF_D = OFF_C + 3 * W_MIX
OFF_G = OFF_D + N_D
N_IN = OFF_G + N_BRANCH * D_MODEL
D_FF = ((8 * D_MODEL // 3 + 127) // 128) * 128
FFN_RESIDUAL = 0.5
RMS_EPS = 1e-6
LN_EPS = 1e-5
GN_EPS = 64e-5
PAGE_SIZE = 128

POOL_HALO = 16
CONV_HALO = 32
SHIFT_HALO = 8
SAMPLE_ROWS = 8
ATTN_BLOCK = 256
SB_GROUP = 4
SB_UNROLL = 4
DECODE_PAGES = 16
FFN_ROW_TILE = 4096
WKV_CHUNK = 128
WKV_CHUNKS_PER_STEP = 2


def _call(kernel, l, grid, in_specs, out_specs, out_shape, args, scratch=(), dims=None, prefetch=()):
    n_pre = 1 + len(prefetch)

    def body(*refs):
        kernel(*refs[n_pre:])

    grid_spec = pltpu.PrefetchScalarGridSpec(
        num_scalar_prefetch=n_pre, grid=grid, in_specs=in_specs, out_specs=out_specs,
        scratch_shapes=list(scratch))
    fn = kernel.func if isinstance(kernel, functools.partial) else kernel
    return pl.pallas_call(
        body, grid_spec=grid_spec, out_shape=out_shape, name=fn.__name__.strip("_"),
        compiler_params=pltpu.CompilerParams(
            dimension_semantics=dims or ("arbitrary",) * len(grid),
            vmem_limit_bytes=VMEM_LIMIT_BYTES),
    )(l, *prefetch, *args)


def _sigmoid(x):
    return 1.0 / (1.0 + jnp.exp(-x))


def _softplus(x):
    return jnp.maximum(x, 0.0) + jnp.log1p(jnp.exp(-jnp.abs(x)))


def _split_bf16(x):
    hi = x.astype(BF16)
    lo = (x - hi.astype(F32)).astype(BF16)
    return hi, lo


def _dot(a, b):
    return jnp.dot(a.astype(BF16), b.astype(BF16), preferred_element_type=F32)


def _dot_nt(a, b):
    return lax.dot_general(a.astype(BF16), b.astype(BF16), (((1,), (1,)), ((), ())),
                           preferred_element_type=F32)


def _rmsnorm_kernel(x_ref, g_ref, o_ref):
    x = x_ref[...]
    y = x * lax.rsqrt(jnp.mean(x * x, axis=-1, keepdims=True) + RMS_EPS)
    o_ref[...] = (y * g_ref[...]).astype(o_ref.dtype)


def rmsnorm(l, x, g, out_dtype):
    m, d = x.shape
    tm = min(1024, m)
    return _call(
        _rmsnorm_kernel, l, (m // tm,),
        [pl.BlockSpec((tm, d), lambda i, l: (i, 0)),
         pl.BlockSpec((None, 1, d), lambda i, l: (l[0], 0, 0))],
        pl.BlockSpec((tm, d), lambda i, l: (i, 0)),
        jax.ShapeDtypeStruct((m, d), out_dtype), (x, g), dims=("parallel",))


def _side_spec(rows, cols, col_block):
    if col_block:
        return pl.BlockSpec((rows, cols), lambda i, j, l: (0, j))
    return pl.BlockSpec((rows, cols), lambda i, j, l: (0, 0))


def _side_out(rows, cols):
    return pl.BlockSpec((None, rows, cols), lambda i, j, l: (i, 0, j))


def _on_first_row_tile(out_ref, compute):
    @pl.when(pl.program_id(0) == 0)
    def _():
        out_ref[...] = compute().astype(out_ref.dtype)

    @pl.when(pl.program_id(0) != 0)
    def _():
        out_ref[...] = jnp.zeros_like(out_ref)


def _ffn_in_kernel(x_ref, wg_ref, wu_ref, xs_ref, wn_ref, o_ref, os_ref, wn16_ref):
    w = jnp.concatenate([wg_ref[...], wu_ref[...]], axis=1).astype(BF16)

    def act(x):
        r = jnp.dot(x, w, preferred_element_type=F32)
        g = r[:, :LANES]
        return g * _sigmoid(g) * r[:, LANES:]

    o_ref[...] = act(x_ref[...]).astype(o_ref.dtype)
    _on_first_row_tile(os_ref, lambda: act(xs_ref[...]))
    wn16_ref[...] = wn_ref[...].astype(BF16)


def ffn_in(l, x, xs, w, w_next):
    m, d = x.shape
    ms = xs.shape[0]
    d_ff = w.shape[2] // 2
    nb = d_ff // LANES
    n_next = w_next.shape[2]
    tm = min(FFN_ROW_TILE, m)
    out, out_s, w_next16 = _call(
        _ffn_in_kernel, l, (m // tm, nb),
        [pl.BlockSpec((tm, d), lambda i, j, l: (i, 0), pipeline_mode=pl.Buffered(1)),
         pl.BlockSpec((None, d, LANES), lambda i, j, l: (l[0], 0, j)),
         pl.BlockSpec((None, d, LANES), lambda i, j, l: (l[0], 0, j + nb)),
         _side_spec(ms, d, False),
         pl.BlockSpec((None, LANES, n_next), lambda i, j, l: (l[0], j, 0))],
        [pl.BlockSpec((tm, LANES), lambda i, j, l: (i, j)), _side_out(ms, LANES),
         pl.BlockSpec((None, LANES, n_next), lambda i, j, l: (i, j, 0))],
        [jax.ShapeDtypeStruct((m, d_ff), BF16), jax.ShapeDtypeStruct((m // tm, ms, d_ff), BF16),
         jax.ShapeDtypeStruct((m // tm, d_ff, n_next), BF16)],
        (x, w, w, xs, w_next), dims=("arbitrary", "arbitrary"))
    return out, out_s[0], w_next16


def _mm_kernel(a_ref, w_ref, as_ref, o_ref, os_ref):
    w = w_ref[...].astype(BF16)
    o_ref[...] = jnp.dot(a_ref[...], w, preferred_element_type=F32)
    _on_first_row_tile(os_ref, lambda: jnp.dot(as_ref[...], w, preferred_element_type=F32))


def _mm_res_kernel(a_ref, w_ref, r_ref, as_ref, rs_ref, o_ref, os_ref, *, scale):
    w = w_ref[...].astype(BF16)
    o_ref[...] = r_ref[...] + scale * jnp.dot(a_ref[...], w, preferred_element_type=F32)
    _on_first_row_tile(
        os_ref, lambda: rs_ref[...] + scale * jnp.dot(as_ref[...], w, preferred_element_type=F32))


def matmul(l, a, a_s, w, *, tn, res=None, res_s=None, scale=1.0, n=None, tm=1024, w_is_layer=False):
    m, k = a.shape
    ms = a_s.shape[0]
    n = w.shape[2] if n is None else n
    tm = min(tm, m)
    a_spec = pl.BlockSpec((tm, k), lambda i, j, l: (i, 0))
    if w_is_layer:
        w_spec = pl.BlockSpec((None, k, tn), lambda i, j, l: (0, 0, j))
    else:
        w_spec = pl.BlockSpec((None, k, tn), lambda i, j, l: (l[0], 0, j))
    o_spec = pl.BlockSpec((tm, tn), lambda i, j, l: (i, j))
    out_specs = [o_spec, _side_out(ms, tn)]
    out_shape = [jax.ShapeDtypeStruct((m, n), F32), jax.ShapeDtypeStruct((m // tm, ms, n), F32)]
    dims = ("arbitrary", "arbitrary")
    if res is None:
        out, out_s = _call(_mm_kernel, l, (m // tm, n // tn), [a_spec, w_spec, _side_spec(ms, k, False)],
                           out_specs, out_shape, (a, w, a_s), dims=dims)
    else:
        out, out_s = _call(
            functools.partial(_mm_res_kernel, scale=scale), l, (m // tm, n // tn),
            [a_spec, w_spec, o_spec, _side_spec(ms, k, False), _side_spec(ms, tn, True)],
            out_specs, out_shape, (a, w, res, a_s, res_s), dims=dims)
    return out, out_s[0]


def _merge_kernel(h_ref, ya_ref, yb_ref, yc_ref, yd_ref, g0_ref, g1_ref, g2_ref, g3_ref, wb_ref,
                  hs_ref, sa_ref, sb_ref, sc_ref, sd_ref, o_ref, os_ref):
    wg = [g_ref[...].astype(BF16) for g_ref in (g0_ref, g1_ref, g2_ref, g3_ref)]
    wb = [wb_ref[n].astype(BF16) for n in range(N_BRANCH)]

    def gated_sum(h, ys):
        acc = None
        for n in range(N_BRANCH):
            gate = jnp.dot(h, wg[n], preferred_element_type=F32)
            t = _sigmoid(gate) * jnp.dot(ys[n], wb[n], preferred_element_type=F32)
            acc = t if acc is None else acc + t
        return acc

    o_ref[...] = gated_sum(h_ref[...], [r[...] for r in (ya_ref, yb_ref, yc_ref, yd_ref)]).astype(o_ref.dtype)
    _on_first_row_tile(os_ref, lambda: gated_sum(hs_ref[...], [r[...] for r in (sa_ref, sb_ref, sc_ref, sd_ref)]))


def merge(l, h, ys, h_s, ys_s, w_in, w_branch):
    m = h.shape[0]
    ms = h_s.shape[0]
    tn = 256
    tm = min(1024, m)
    nj = D_MODEL // tn
    y_spec = pl.BlockSpec((tm, W_MIX), lambda i, j, l: (i, 0))
    g_specs = [pl.BlockSpec((None, D_MODEL, tn), functools.partial(
        lambda i, j, l, n: (l[0], 0, OFF_G // tn + n * nj + j), n=n)) for n in range(N_BRANCH)]
    out, out_s = _call(
        _merge_kernel, l, (m // tm, nj),
        [pl.BlockSpec((tm, D_MODEL), lambda i, j, l: (i, 0))] + [y_spec] * N_BRANCH + g_specs
        + [pl.BlockSpec((None, N_BRANCH, W_MIX, tn), lambda i, j, l: (l[0], 0, 0, j))]
        + [_side_spec(ms, D_MODEL, False)] + [_side_spec(ms, W_MIX, False)] * N_BRANCH,
        [pl.BlockSpec((tm, tn), lambda i, j, l: (i, j)), _side_out(ms, tn)],
        [jax.ShapeDtypeStruct((m, D_MODEL), BF16),
         jax.ShapeDtypeStruct((m // tm, ms, D_MODEL), BF16)],
        (h, *ys, w_in, w_in, w_in, w_in, w_branch, h_s, *ys_s), dims=("arbitrary", "arbitrary"))
    return out, out_s[0]


def _pool_kernel(u_ref, t0_ref, wp_ref, ps_ref, y_ref, ext_ref, *, tt, pos0):
    i = pl.program_id(1)

    @pl.when(i == 0)
    def _():
        ext_ref[0:POOL_HALO, :] = t0_ref[...]

    @pl.when(i > 0)
    def _():
        ext_ref[0:POOL_HALO, :] = ext_ref[tt:tt + POOL_HALO, :]

    u = u_ref[...]
    ext_ref[POOL_HALO:POOL_HALO + tt, :] = u
    pos = pos0 + i * tt + lax.broadcasted_iota(jnp.int32, (tt, 1), 0)
    ys = []
    for gi, win in enumerate(POOL_WINDOWS):
        cs = slice(gi * POOL_GROUP, (gi + 1) * POOL_GROUP)
        wsum = u[:, cs]
        for k in range(1, win):
            wsum = wsum + ext_ref[POOL_HALO - k:POOL_HALO - k + tt, cs]
        count = jnp.minimum(win, pos + 1).astype(F32)
        d = wsum / count - u[:, cs]
        ys.append(_dot(d, wp_ref[gi]))
    y_ref[...] = (jnp.concatenate(ys, axis=1) * ps_ref[...]).astype(y_ref.dtype)


def pool_mixer(l, proj3, tail0, w_pool, pool_scale, pos0):
    b, t, _ = proj3.shape
    tt = min(512, t)
    return _call(
        functools.partial(_pool_kernel, tt=tt, pos0=pos0), l, (b, t // tt),
        [pl.BlockSpec((None, tt, W_MIX), lambda b, i, l: (b, i, OFF_A // W_MIX)),
         pl.BlockSpec((None, POOL_HALO, W_MIX), lambda b, i, l: (b, 0, 0)),
         pl.BlockSpec((None, len(POOL_WINDOWS), POOL_GROUP, POOL_GROUP), lambda b, i, l: (l[0], 0, 0, 0)),
         pl.BlockSpec((None, 1, W_MIX), lambda b, i, l: (l[0], 0, 0))],
        pl.BlockSpec((None, tt, W_MIX), lambda b, i, l: (b, i, 0)),
        jax.ShapeDtypeStruct((b, t, W_MIX), BF16), (proj3, tail0, w_pool, pool_scale),
        scratch=[pltpu.VMEM((POOL_HALO + tt, W_MIX), F32)], dims=("parallel", "arbitrary"))


def _conv_kernel(val_ref, gate_ref, t0_ref, w_ref, cb_ref, lg_ref, lb_ref, y_ref, u_ref, ext_ref, sh_ref,
                 *, tt):
    i = pl.program_id(1)

    @pl.when(i == 0)
    def _():
        ext_ref[0:CONV_HALO, :] = t0_ref[...]

    @pl.when(i > 0)
    def _():
        ext_ref[0:CONV_HALO, :] = ext_ref[tt:tt + CONV_HALO, :]

    u = val_ref[...] * _sigmoid(gate_ref[...])
    u_ref[...] = u
    ext_ref[CONV_HALO:CONV_HALO + tt, :] = u
    first = CONV_HALO - CONV_BUF
    last = first + CONV_WIDTH - 1
    for b in range(SUBLANES):
        top = max(s for s in range(first, last + 1) if s % SUBLANES == b)
        sh_ref[b, 0:top - b + tt, :] = ext_ref[b:top + tt, :]
    acc = cb_ref[...]
    for j in range(CONV_WIDTH):
        s = first + j
        base = s - s % SUBLANES
        acc = acc + sh_ref[s % SUBLANES, base:base + tt, :] * w_ref[j:j + 1, :]
    y = acc
    mu = jnp.mean(y, axis=-1, keepdims=True)
    yc = y - mu
    var = jnp.mean(yc * yc, axis=-1, keepdims=True)
    yn = yc * lax.rsqrt(var + LN_EPS) * lg_ref[...] + lb_ref[...]
    y_ref[...] = (yn * _sigmoid(yn)).astype(y_ref.dtype)


def conv_mixer(l, proj3, tail0, conv_w, conv_b, ln_g, ln_b):
    b, t, _ = proj3.shape
    tt = min(512, t)
    vec = pl.BlockSpec((None, 1, W_MIX), lambda b, i, l: (l[0], 0, 0))
    blk = pl.BlockSpec((None, tt, W_MIX), lambda b, i, l: (b, i, 0))
    return _call(
        functools.partial(_conv_kernel, tt=tt), l, (b, t // tt),
        [pl.BlockSpec((None, tt, W_MIX), lambda b, i, l: (b, i, OFF_B // W_MIX)),
         pl.BlockSpec((None, tt, W_MIX), lambda b, i, l: (b, i, OFF_B // W_MIX + 1)),
         pl.BlockSpec((None, CONV_HALO, W_MIX), lambda b, i, l: (b, 0, 0)),
         pl.BlockSpec((None, CONV_WIDTH, W_MIX), lambda b, i, l: (l[0], 0, 0)),
         vec, vec, vec],
        [blk, blk],
        [jax.ShapeDtypeStruct((b, t, W_MIX), BF16), jax.ShapeDtypeStruct((b, t, W_MIX), F32)],
        (proj3, proj3, tail0, conv_w, conv_b, ln_g, ln_b),
        scratch=[pltpu.VMEM((CONV_HALO + tt, W_MIX), F32),
                 pltpu.VMEM((SUBLANES, CONV_HALO + tt, W_MIX), F32)], dims=("parallel", "arbitrary"))


def _sb_heads(x):
    n = x.shape[1] // SB_HEAD_DIM
    return jnp.stack([x[:, h * SB_HEAD_DIM:(h + 1) * SB_HEAD_DIM] for h in range(n)], axis=0)


def _sb_block(q3, k3, v3, bias, carry, acc, neg_upper, mask):
    nh, tq, _ = q3.shape
    tk = k3.shape[1]
    z = _bdot_nt(q3, k3)
    z = jnp.stack([z[h] + bias[h] for h in range(nh)], axis=0)
    fail = jnp.maximum(z, 0.0) + jnp.log(1.0 + jnp.exp2(jnp.abs(z) * (-LOG2E)))
    if mask is not None:
        fail = jnp.where(mask, fail, 0.0)
    log_hit = z - fail
    fail_first = fail[:, :, 0:1]
    later = jnp.dot(fail.astype(BF16).reshape(nh * tq, tk), neg_upper,
                    preferred_element_type=F32).reshape(nh, tq, tk)
    wts = jnp.exp(log_hit + later + carry)
    if mask is not None:
        wts = jnp.where(mask, wts, 0.0)
    acc = acc + _bdot(wts, v3)
    carry = carry + later[:, :, 0:1] - fail_first
    return carry, acc


def _sb_prompt_kernel(bias_ref, q_ref, k_ref, v_ref, o_ref, kb_ref, vb_ref, *, blk, n_blk):
    qi = pl.program_id(2)
    grp = pl.program_id(1)

    @pl.when(qi == 0)
    def _():
        def fill(j, _):
            rows = pl.ds(pl.multiple_of(j * blk, blk), blk)
            kb_ref[:, rows, :] = _sb_heads(k_ref[rows, :].astype(BF16))
            vb_ref[:, rows, :] = _sb_heads(v_ref[rows, :].astype(BF16))
            return 0
        lax.fori_loop(0, n_blk, fill, 0)

    row = lax.broadcasted_iota(jnp.int32, (blk, blk), 0)
    col = lax.broadcasted_iota(jnp.int32, (blk, blk), 1)
    causal = row > col
    neg_upper = jnp.where(causal, -1.0, 0.0).astype(BF16)
    q3 = _sb_heads((q_ref[...] * SB_SCALE).astype(BF16))
    nh = q3.shape[0]
    bias = [bias_ref[0, grp * nh + h] for h in range(nh)]

    def kv(j):
        rows = pl.ds(pl.multiple_of(j * blk, blk), blk)
        return kb_ref[:, rows, :], vb_ref[:, rows, :]

    carry = jnp.zeros((nh, blk, 1), F32)
    acc = jnp.zeros((nh, blk, SB_HEAD_DIM), F32)
    carry, acc = _sb_block(q3, *kv(qi), bias, carry, acc, neg_upper, causal)

    def step(j, state):
        return _sb_block(q3, *kv(j), bias, state[0], state[1], neg_upper, None)

    rem = qi % SB_UNROLL
    carry, acc = lax.fori_loop(0, rem, lambda it, state: step(qi - 1 - it, state), (carry, acc))

    def body(it, state):
        j = qi - 1 - rem - SB_UNROLL * it
        for u in range(SB_UNROLL):
            state = step(j - u, state)
        return state

    carry, acc = lax.fori_loop(0, qi // SB_UNROLL, body, (carry, acc))
    o_ref[...] = jnp.concatenate([acc[h] for h in range(nh)], axis=1).astype(o_ref.dtype)


def sb_prompt(l, proj3, sb_bias):
    b, t, _ = proj3.shape
    blk = min(ATTN_BLOCK, t)
    wid = SB_GROUP * SB_HEAD_DIM
    ngrp = W_MIX // wid
    c0 = OFF_C // wid
    return _call(
        functools.partial(_sb_prompt_kernel, blk=blk, n_blk=t // blk), l, (b, ngrp, t // blk),
        [pl.BlockSpec((None, 1, SB_HEADS), lambda b, h, i, l: (l[0], 0, 0), memory_space=pltpu.SMEM),
         pl.BlockSpec((None, blk, wid), lambda b, h, i, l: (b, i, c0 + h)),
         pl.BlockSpec((None, t, wid), lambda b, h, i, l: (b, 0, c0 + ngrp + h)),
         pl.BlockSpec((None, t, wid), lambda b, h, i, l: (b, 0, c0 + 2 * ngrp + h))],
        pl.BlockSpec((None, blk, wid), lambda b, h, i, l: (b, i, h)),
        jax.ShapeDtypeStruct((b, t, W_MIX), BF16), (sb_bias, proj3, proj3, proj3),
        scratch=[pltpu.VMEM((SB_GROUP, t, SB_HEAD_DIM), BF16)] * 2,
        dims=("parallel", "parallel", "arbitrary"))


def _sb_decode_kernel(q_ref, bias_ref, *refs, n_steps, pages):
    k_refs, v_refs = refs[:pages], refs[pages:2 * pages]
    o_ref, carry_ref, acc_ref = refs[2 * pages:]
    p = pl.program_id(1)

    @pl.when(p == 0)
    def _():
        carry_ref[...] = jnp.zeros_like(carry_ref)
        acc_ref[...] = jnp.zeros_like(acc_ref)

    q = (q_ref[...] * SB_SCALE).astype(BF16)
    bias = bias_ref[...]
    z = jnp.concatenate([_dot(q, k_ref[...].reshape(W_MIX, PAGE_SIZE)) + bias for k_ref in k_refs], axis=0)
    log_fail = -_softplus(z)
    row = lax.broadcasted_iota(jnp.int32, (PAGE_SIZE, PAGE_SIZE), 0)
    col = lax.broadcasted_iota(jnp.int32, (PAGE_SIZE, PAGE_SIZE), 1)
    upper = (row > col).astype(BF16)
    later = _dot(log_fail, upper)
    total = later[:, 0:1] + log_fail[:, 0:1]
    arg = z + log_fail + later
    carry = carry_ref[...]
    acc = acc_ref[...]
    for i in reversed(range(pages)):
        rows = slice(i * SB_HEADS, (i + 1) * SB_HEADS)
        wts = jnp.exp(arg[rows] + carry)
        acc = acc + _dot_nt(wts, v_refs[i][...].reshape(W_MIX, PAGE_SIZE))
        carry = carry + total[rows]
    carry_ref[...] = carry
    acc_ref[...] = acc

    @pl.when(p == n_steps - 1)
    def _():
        head = lax.broadcasted_iota(jnp.int32, acc.shape, 0)
        col_head = lax.broadcasted_iota(jnp.int32, acc.shape, 1) // SB_HEAD_DIM
        o_ref[...] = jnp.sum(jnp.where(head == col_head, acc, 0.0), axis=0, keepdims=True)


def sb_decode(l, q_blocks, sb_bias_col, cache_kt, cache_vt, page_table):
    b = q_blocks.shape[0]
    n_pages = page_table.shape[1]
    pages = DECODE_PAGES if n_pages % DECODE_PAGES == 0 else 1
    n_steps = n_pages // pages

    def page(i):
        return pl.BlockSpec((None, None, SB_HEADS, SB_HEAD_DIM, PAGE_SIZE),
                            lambda b, p, l, pt: (l[0], pt[b, n_pages - pages * (p + 1) + i], 0, 0, 0))

    return _call(
        functools.partial(_sb_decode_kernel, n_steps=n_steps, pages=pages), l, (b, n_steps),
        [pl.BlockSpec((None, SB_HEADS, W_MIX), lambda b, p, l, pt: (b, 0, 0)),
         pl.BlockSpec((None, SB_HEADS, 1), lambda b, p, l, pt: (l[0], 0, 0))]
        + [page(i) for i in range(pages)] * 2,
        pl.BlockSpec((None, 1, W_MIX), lambda b, p, l, pt: (b, 0, 0)),
        jax.ShapeDtypeStruct((b, 1, W_MIX), F32),
        (q_blocks, sb_bias_col) + (cache_kt,) * pages + (cache_vt,) * pages,
        scratch=[pltpu.VMEM((SB_HEADS, 1), F32), pltpu.VMEM((SB_HEADS, W_MIX), F32)],
        dims=("parallel", "arbitrary"), prefetch=(page_table,))


def _wkv_inputs(pr_ref, pk_ref, pv_ref, pl_ref, sh0_ref, mu_ref, w0_ref, dup_ref, a0_ref, aup_ref,
                gup_ref, ext_ref, *, tt, t_valid):
    i = pl.program_id(1)

    @pl.when(i == 0)
    def _():
        ext_ref[0:SHIFT_HALO, :] = sh0_ref[...]

    @pl.when(i > 0)
    def _():
        ext_ref[0:SHIFT_HALO, :] = ext_ref[tt:tt + SHIFT_HALO, :]

    rows = slice(SHIFT_HALO, SHIFT_HALO + tt)
    ext_ref[rows, 0:W_MIX] = pr_ref[...]
    ext_ref[rows, W_MIX:2 * W_MIX] = pk_ref[...]
    ext_ref[rows, 2 * W_MIX:3 * W_MIX] = pv_ref[...]
    ext_ref[rows, 3 * W_MIX:N_D] = pl_ref[...]
    p = ext_ref[rows, :]
    prev = ext_ref[SHIFT_HALO - 1:SHIFT_HALO - 1 + tt, :]
    xs = p + (prev - p) * mu_ref[...]
    o = 3 * W_MIX
    w_dn = xs[:, o:o + DECAY_LORA]
    a_dn = xs[:, o + DECAY_LORA:o + DECAY_LORA + A_LORA]
    g_dn = xs[:, o + DECAY_LORA + A_LORA:N_D]
    w_log = -_softplus(-(w0_ref[...] + _dot(jnp.tanh(w_dn), dup_ref[...]))) - 0.5
    log_decay = -jnp.exp(w_log)
    alpha = _sigmoid(a0_ref[...] + _dot(a_dn, aup_ref[...]))
    k = xs[:, W_MIX:2 * W_MIX]
    v = xs[:, 2 * W_MIX:3 * W_MIX]
    if t_valid is not None:
        valid = (i * tt + lax.broadcasted_iota(jnp.int32, (tt, 1), 0)) < t_valid
        k = jnp.where(valid, k, 0.0)
        v = jnp.where(valid, v, 0.0)
        log_decay = jnp.where(valid, log_decay, 0.0)
    return xs[:, 0:W_MIX], k, v, log_decay, alpha, _dot(_sigmoid(g_dn), gup_ref[...])


def _bdot(a, b):
    return lax.dot_general(a.astype(BF16), b.astype(BF16), (((2,), (1,)), ((0,), (0,))),
                           preferred_element_type=F32)


def _bdot_nt(a, b):
    return lax.dot_general(a.astype(BF16), b.astype(BF16), (((2,), (2,)), ((0,), (0,))),
                           preferred_element_type=F32)


def _bdot_tn(a, b):
    return lax.dot_general(a.astype(BF16), b.astype(BF16), (((1,), (1,)), ((0,), (0,))),
                           preferred_element_type=F32)


def _heads(x):
    return jnp.stack([x[:, h * RWKV_HEAD_DIM:(h + 1) * RWKV_HEAD_DIM] for h in range(RWKV_HEADS)], axis=0)


def _wkv_chunk_prepare(r, k, v, ld, al, kk_w, ka_w, rk_w):
    ch = r.shape[0]
    row = lax.broadcasted_iota(jnp.int32, (ch, ch), 0)
    col = lax.broadcasted_iota(jnp.int32, (ch, ch), 1)
    incl = row >= col
    eye = (row == col).astype(F32)
    row2 = lax.broadcasted_iota(jnp.int32, (2 * ch, 2 * ch), 0)
    col2 = lax.broadcasted_iota(jnp.int32, (2 * ch, 2 * ch), 1) & (ch - 1)
    tri = (row2 & (ch - 1)) + (row2 // ch) > col2
    hi, lo = _split_bf16(ld)
    lower = incl.astype(BF16)
    cum = (jnp.dot(lower, hi, preferred_element_type=F32)
           + jnp.dot(lower, lo, preferred_element_type=F32))
    cum_end = cum[ch - 1:ch, :]
    g_in = jnp.exp(cum)
    g_ex = jnp.exp(cum - ld)
    g_inv = jnp.exp(-cum)
    g_end = jnp.exp(cum_end - cum)
    g_tot = jnp.exp(cum_end)
    kk = k * kk_w
    k2 = k * (1.0 + (al - 1.0) * ka_w)
    sums = _heads(jnp.concatenate([kk, r * k2 * rk_w], axis=0))
    kk3 = sums[:, 0:ch]
    unit = 1.0 / jnp.maximum(jnp.sqrt(jnp.sum(kk3 * kk3, axis=-1, keepdims=True)), 1e-12)
    rk_sum = jnp.sum(sums[:, ch:2 * ch], axis=-1, keepdims=True)

    def scaled(top, bottom):
        x = _heads(jnp.concatenate([top, bottom], axis=0))
        return jnp.concatenate([x[:, 0:ch] * unit, x[:, ch:2 * ch]], axis=1)

    b = kk * al
    a2 = scaled(-kk * g_ex, r * g_in)
    bm = scaled(b * g_inv, k2 * g_inv)
    bk_end = scaled(b * g_end, k2 * g_end)
    v3 = _heads(v)
    prod = jnp.where(tri, _bdot_nt(a2, bm), 0.0)
    x = prod[:, 0:ch, 0:ch]
    inv = eye + x
    n_sq = ch.bit_length() - 2
    for _ in range(n_sq):
        x = _bdot(x, x)
        inv = inv + _bdot(inv, x)
    ak_v = _bdot(prod[:, 0:ch], jnp.concatenate([jnp.zeros_like(v3), v3], axis=1))
    return dict(a2=a2, prod_r=prod[:, ch:2 * ch], inv=inv, ak_v=ak_v, v3=v3, bk_end=bk_end,
                g_tot=_heads(g_tot), bonus=rk_sum * v3)


def _wkv_chunk_advance(p, s, g, ln_g, ln_b):
    ch = p['inv'].shape[1]
    u = _bdot(p['inv'], _bdot_nt(p['a2'][:, 0:ch], s) + p['ak_v'])
    uv = jnp.concatenate([u, p['v3']], axis=1)
    y3 = _bdot_nt(p['a2'][:, ch:2 * ch], s) + _bdot(p['prod_r'], uv)
    s_new = s * p['g_tot'] + _bdot_tn(uv, p['bk_end'])
    mu = jnp.mean(y3, axis=-1, keepdims=True)
    yc = y3 - mu
    var = jnp.mean(yc * yc, axis=-1, keepdims=True)
    out3 = yc * lax.rsqrt(var + GN_EPS) * _heads(ln_g) + _heads(ln_b) + p['bonus']
    return jnp.concatenate([out3[h] for h in range(RWKV_HEADS)], axis=1) * g, s_new


def _wkv_kernel(pr_ref, pk_ref, pv_ref, pl_ref, sh0_ref, mu_ref, w0_ref, dup_ref, a0_ref, aup_ref, gup_ref,
                s0_ref, kk_ref, ka_ref, rk_ref, lg_ref, lb_ref, y_ref, so_ref, s_ref, ext_ref,
                *, ch, n_steps, t_valid):
    c = pl.program_id(1)

    @pl.when(c == 0)
    def _():
        s_ref[...] = s0_ref[...]

    tt = pr_ref.shape[0]
    r, k, v, ld, al, g = _wkv_inputs(pr_ref, pk_ref, pv_ref, pl_ref, sh0_ref, mu_ref, w0_ref, dup_ref,
                                     a0_ref, aup_ref, gup_ref, ext_ref, tt=tt, t_valid=t_valid)
    chunks = [slice(j * ch, (j + 1) * ch) for j in range(tt // ch)]
    prepared = [_wkv_chunk_prepare(r[rows], k[rows], v[rows], ld[rows], al[rows],
                                   kk_ref[...], ka_ref[...], rk_ref[...]) for rows in chunks]
    s = s_ref[...]
    for rows, p in zip(chunks, prepared):
        y, s = _wkv_chunk_advance(p, s, g[rows], lg_ref[...], lb_ref[...])
        y_ref[rows, :] = y.astype(y_ref.dtype)
    s_ref[...] = s

    @pl.when(c == n_steps - 1)
    def _():
        so_ref[...] = s


def wkv_mixer(l, proj3, shift0, s0, lw, t_valid):
    b, t, _ = proj3.shape
    ch = min(WKV_CHUNK, t)
    rows = ch * WKV_CHUNKS_PER_STEP if t % (ch * WKV_CHUNKS_PER_STEP) == 0 else ch
    cd = OFF_D // W_MIX
    col = lambda j: pl.BlockSpec((None, rows, W_MIX), lambda b, c, l: (b, c, cd + j))
    st = pl.BlockSpec((None, RWKV_HEADS, RWKV_HEAD_DIM, RWKV_HEAD_DIM), lambda b, c, l: (b, 0, 0, 0))
    vec = lambda n: pl.BlockSpec((None, 1, n), lambda b, c, l: (l[0], 0, 0))
    up = lambda n: pl.BlockSpec((None, n, W_MIX), lambda b, c, l: (l[0], 0, 0))
    return _call(
        functools.partial(_wkv_kernel, ch=ch, n_steps=t // rows, t_valid=t_valid), l, (b, t // rows),
        [col(0), col(1), col(2),
         pl.BlockSpec((None, rows, N_LORA), lambda b, c, l: (b, c, (OFF_D + 3 * W_MIX) // N_LORA)),
         pl.BlockSpec((None, SHIFT_HALO, N_D), lambda b, c, l: (b, 0, 0)),
         vec(N_D), vec(W_MIX), up(DECAY_LORA), vec(W_MIX), up(A_LORA), up(GATE_LORA), st]
        + [vec(W_MIX)] * 5,
        [pl.BlockSpec((None, rows, W_MIX), lambda b, c, l: (b, c, 0)), st],
        [jax.ShapeDtypeStruct((b, t, W_MIX), BF16),
         jax.ShapeDtypeStruct((b, RWKV_HEADS, RWKV_HEAD_DIM, RWKV_HEAD_DIM), F32)],
        (proj3, proj3, proj3, proj3, shift0, lw['shift_mu'], lw['decay_w0'], lw['decay_up'], lw['a0'],
         lw['a_up'], lw['g_up'], s0, lw['k_k'], lw['k_a'], lw['r_k'], lw['lnx_g'], lw['lnx_b']),
        scratch=[pltpu.VMEM((RWKV_HEADS, RWKV_HEAD_DIM, RWKV_HEAD_DIM), F32),
                 pltpu.VMEM((SHIFT_HALO + rows, N_D), F32)],
        dims=("parallel", "arbitrary"))


def _split_heads_kernel(k_ref, v_ref, ko_ref, vo_ref):
    ko_ref[...] = k_ref[...].T.reshape(ko_ref.shape)
    vo_ref[...] = v_ref[...].T.reshape(vo_ref.shape)


def split_heads(l, proj3):
    b, t, _ = proj3.shape
    tt = min(512, t)
    col = lambda j: pl.BlockSpec((None, tt, W_MIX), lambda b, i, l: (b, i, OFF_C // W_MIX + j))
    out = pl.BlockSpec((None, SB_HEADS, SB_HEAD_DIM, tt), lambda b, i, l: (b, 0, 0, i))
    shape = jax.ShapeDtypeStruct((b, SB_HEADS, SB_HEAD_DIM, t), F32)
    return _call(_split_heads_kernel, l, (b, t // tt), [col(1), col(2)], [out, out], [shape, shape],
                 (proj3, proj3), dims=("parallel", "parallel"))


def _mixers(l, proj3, pool0, conv0, shift0, wkv0, pos0, t_valid, attend, lw):
    b, t, _ = proj3.shape
    y_a = pool_mixer(l, proj3, pool0, lw['w_pool'], lw['pool_scale'], pos0)
    y_b, glu = conv_mixer(l, proj3, conv0, lw['conv_w'], lw['conv_b'], lw['ln_g'], lw['ln_b'])
    y_c = attend(l, proj3)
    y_d, wkv_new = wkv_mixer(l, proj3, shift0, wkv0, lw, t_valid)
    return [y.reshape(b * t, W_MIX) for y in (y_a, y_b, y_c, y_d)], glu, wkv_new


def _trunk_layer(l, xp, xs, mix_p, mix_s, lw):
    dp, ds = xp.shape, xs.shape
    xp, xs = xp.reshape(-1, dp[2]), xs.reshape(-1, ds[2])

    def ffn(xp, xs, g, w_a, w_b):
        act_p, act_s, w_b16 = ffn_in(l, rmsnorm(l, xp, g, BF16), rmsnorm(l, xs, g, BF16), w_a, w_b)
        return matmul(l, act_p, act_s, w_b16, tn=512, res=xp, res_s=xs, scale=FFN_RESIDUAL, w_is_layer=True)

    xp, xs = ffn(xp, xs, lw['g_ffn1'], lw['w_ffn1_in'], lw['w_ffn1_out'])
    hp, hs = rmsnorm(l, xp, lw['g_mix'], BF16), rmsnorm(l, xs, lw['g_mix'], BF16)
    proj_p, proj_s = matmul(l, hp, hs, lw['w_in'], tn=256, n=OFF_G, tm=2048)
    proj_p, proj_s = proj_p.reshape(dp[0], dp[1], OFF_G), proj_s.reshape(ds[0], ds[1], OFF_G)
    ys_p, glu_p, wkv_p = mix_p(l, proj_p)
    ys_s, glu_s, wkv_s = mix_s(l, proj_s)
    merged_p, merged_s = merge(l, hp, ys_p, hs, ys_s, lw['w_in'], lw['w_branch'])
    xp, xs = matmul(l, merged_p, merged_s, lw['w_out'], tn=512, res=xp, res_s=xs, scale=1.0, tm=2048)
    xp, xs = ffn(xp, xs, lw['g_ffn2'], lw['w_ffn2_in'], lw['w_ffn2_out'])
    return xp.reshape(dp), xs.reshape(ds), (proj_p, glu_p, wkv_p), (proj_s, glu_s, wkv_s)


def _front_pad(rows, height):
    return jnp.pad(rows, ((0, 0), (height - rows.shape[1], 0), (0, 0)))


def kernel(x_prompt, x_sample, cache_k, cache_v, state_pool, state_conv, state_shift, state_wkv,
           page_table, g_ffn1, w_ffn1_in, w_ffn1_out, g_mix, w_in, w_pool, pool_scale, conv_w,
           conv_b, ln_g, ln_b, sb_bias, shift_mu, decay_w0, decay_up, a0, a_up, g_up, k_k, k_a, r_k,
           lnx_g, lnx_b, w_branch, w_out, g_ffn2, w_ffn2_in, w_ffn2_out, g_final):
    depth = w_in.shape[0]
    bp, tp, d = x_prompt.shape
    bs, ts, _ = x_sample.shape
    assert d == D_MODEL and ts == 1
    past_len = page_table.shape[1] * PAGE_SIZE
    row = lambda a: a.reshape(depth, 1, -1)
    lw = {
        'g_ffn1': row(g_ffn1), 'w_ffn1_in': w_ffn1_in, 'w_ffn1_out': w_ffn1_out, 'g_mix': row(g_mix),
        'w_in': w_in, 'w_pool': w_pool, 'pool_scale': row(pool_scale), 'conv_w': conv_w,
        'conv_b': row(conv_b), 'ln_g': row(ln_g), 'ln_b': row(ln_b), 'sb_bias': row(sb_bias),
        'shift_mu': row(shift_mu), 'decay_w0': row(decay_w0), 'decay_up': decay_up, 'a0': row(a0),
        'a_up': a_up, 'g_up': g_up, 'k_k': row(k_k), 'k_a': row(k_a), 'r_k': row(r_k),
        'lnx_g': row(lnx_g), 'lnx_b': row(lnx_b), 'w_branch': w_branch, 'w_out': w_out,
        'g_ffn2': row(g_ffn2), 'w_ffn2_in': w_ffn2_in, 'w_ffn2_out': w_ffn2_out,
    }
    sb_bias_col = sb_bias.reshape(depth, SB_HEADS, 1)
    cache_kt = jnp.transpose(cache_k, (0, 1, 3, 4, 2))
    cache_vt = jnp.transpose(cache_v, (0, 1, 3, 4, 2))
    own_head = (jnp.arange(W_MIX)[None, :] // SB_HEAD_DIM) == jnp.arange(SB_HEADS)[:, None]
    xs0 = jnp.pad(x_sample, ((0, 0), (0, SAMPLE_ROWS - ts), (0, 0)))

    def layer(carry, li):
        xp, xs = carry
        l = li.reshape(1)
        pool_s = lax.dynamic_index_in_dim(state_pool, li, 0, keepdims=False)
        conv_s = lax.dynamic_index_in_dim(state_conv, li, 0, keepdims=False)
        shift_s = lax.dynamic_index_in_dim(state_shift, li, 0, keepdims=False)
        wkv_s = lax.dynamic_index_in_dim(state_wkv, li, 0, keepdims=False)

        def attend_sample(l, p3):
            q_blocks = jnp.where(own_head, p3[:, 0:1, OFF_C:OFF_C + W_MIX], 0.0)
            y = sb_decode(l, q_blocks, sb_bias_col, cache_kt, cache_vt, page_table)
            return jnp.pad(y, ((0, 0), (0, SAMPLE_ROWS - 1), (0, 0))).astype(BF16)

        mix_p = lambda l, p3: _mixers(
            l, p3, jnp.zeros((bp, POOL_HALO, W_MIX), F32), jnp.zeros((bp, CONV_HALO, W_MIX), F32),
            jnp.zeros((bp, SHIFT_HALO, N_D), F32),
            jnp.zeros((bp, RWKV_HEADS, RWKV_HEAD_DIM, RWKV_HEAD_DIM), F32), 0, None,
            lambda l, p3: sb_prompt(l, p3, lw['sb_bias']), lw)
        mix_s = lambda l, p3: _mixers(
            l, p3, _front_pad(pool_s, POOL_HALO), _front_pad(conv_s, CONV_HALO),
            _front_pad(shift_s, SHIFT_HALO), wkv_s, past_len, ts, attend_sample, lw)
        xp, xs, (proj_p, glu_p, wkv_p), (proj_s, glu_s, wkv_s_new) = _trunk_layer(l, xp, xs, mix_p, mix_s, lw)

        k_p, v_p = split_heads(l, proj_p)
        k_s, v_s = (proj_s[:, :ts, OFF_C + j * W_MIX:OFF_C + (j + 1) * W_MIX].reshape(
            bs, ts, SB_HEADS, SB_HEAD_DIM) for j in (1, 2))
        outs = (
            k_p, v_p, proj_p[:, tp - POOL_BUF:, OFF_A:OFF_A + W_MIX], glu_p[:, tp - CONV_BUF:],
            proj_p[:, tp - 1:, OFF_D:OFF_D + N_D], wkv_p,
            k_s, v_s,
            jnp.concatenate([pool_s, proj_s[:, :ts, OFF_A:OFF_A + W_MIX]], axis=1)[:, -POOL_BUF:],
            jnp.concatenate([conv_s, glu_s[:, :ts]], axis=1)[:, -CONV_BUF:],
            proj_s[:, ts - 1:ts, OFF_D:OFF_D + N_D], wkv_s_new)
        return (xp, xs), outs

    (xp, xs), st = lax.scan(layer, (x_prompt, xs0), jnp.arange(depth, dtype=jnp.int32))
    zero = jnp.zeros((1,), jnp.int32)
    g_fin = g_final.reshape(1, 1, d)
    y_prompt = rmsnorm(zero, xp.reshape(bp * tp, d), g_fin, F32).reshape(bp, tp, d)
    y_sample = rmsnorm(zero, xs.reshape(bs * SAMPLE_ROWS, d), g_fin, F32).reshape(bs, SAMPLE_ROWS, d)[:, :ts]
    to_cache_layout = lambda a: jnp.transpose(a, (0, 1, 4, 2, 3))
    return (y_prompt, y_sample, to_cache_layout(st[0]), to_cache_layout(st[1])) + tuple(st[2:])
```
